```python
import math
import jax
import jax.numpy as jnp
from jax import lax
import numpy as np

D_MODEL = 2048
BATCH = 2
SEQ = 8192
DEPTH = 4

GRID_W = 64
CTX_LEN = 256

HY_W = 1024
HY_ORDER = 2
HY_EMB = 33
HY_BANDS = (HY_EMB - 1) // 2
HY_HID = 64
HY_FAST_DECAY = 0.3
HY_SLOW_DECAY = 1.5
HY_TARGET = 1e-2

MLA_HEADS = 8
MLA_NOPE = 128
MLA_ROPE = 64
MLA_V = 128
MLA_Q_RANK = 512
MLA_KV_RANK = 256
MLA_SCALE = (MLA_NOPE + MLA_ROPE) ** -0.5
Q_BLOCK = 128
ROPE_BASE = 10000.0

O_HY = 3 * HY_W
O_Q = O_HY + MLA_Q_RANK
O_KV = O_Q + MLA_KV_RANK
IN_W = O_KV + MLA_ROPE

S5_GC = 16
S5_NG = D_MODEL // S5_GC
S5_P = 64
S5_GB = 16
S5_NBLK = S5_NG // S5_GB

PEER_HEADS = 8
PEER_NK = 128
PEER_N = PEER_NK * PEER_NK
PEER_DK = 256
PEER_TOPK = 16
PEER_BLOCK = 128

N_EVEN = (DEPTH + 1) // 2
N_ODD = DEPTH // 2
DN_ALPHA = (2.0 * DEPTH) ** 0.25
DN_BETA = (8.0 * DEPTH) ** -0.25
LN_EPS = 1e-5
RMS_EPS = 1e-6
F32 = jnp.float32

kernel_name = 'hybrid_hyena_mla_s5_peer_dit'


def layer_norm(x, g, b):
    xf = x.astype(F32)
    xc = xf - jnp.mean(xf, axis=-1, keepdims=True)
    var = jnp.mean(xc * xc, axis=-1, keepdims=True)
    return (xc * lax.rsqrt(var + LN_EPS) * g + b).astype(x.dtype)


def rms_norm(x, g):
    xf = x.astype(F32)
    return (xf * lax.rsqrt(jnp.mean(xf * xf, axis=-1, keepdims=True) + RMS_EPS) * g).astype(x.dtype)


def axial_rope(L):
    rows = L // GRID_W
    row = jnp.repeat(jnp.arange(rows), GRID_W).astype(F32)
    col = jnp.tile(jnp.arange(GRID_W), rows).astype(F32)
    half = MLA_ROPE // 2
    inv = ROPE_BASE ** (-jnp.arange(0, half, 2, dtype=F32) / half)
    ar, ac = row[:, None] * inv, col[:, None] * inv
    ang = jnp.concatenate([ar, ar, ac, ac], axis=-1)
    return jnp.cos(ang), jnp.sin(ang)


def apply_axial_rope(x, cos, sin):
    xs = x.reshape(x.shape[:-1] + (2, 2, MLA_ROPE // 4))
    rot = jnp.concatenate([-xs[..., 1:, :], xs[..., :1, :]], axis=-2).reshape(x.shape)
    return (x * cos + rot * sin).astype(x.dtype)


def short_conv3(z, w, b):
    L = z.shape[1]
    zp = jnp.pad(z, ((0, 0), (1, 1), (0, 0)))
    return zp[:, :L] * w[0] + zp[:, 1:L + 1] * w[1] + zp[:, 2:] * w[2] + b


def hyena_pos_features(L):
    t = jnp.linspace(0.0, 1.0, L, dtype=F32)[:, None]
    w = (2.0 * math.pi / L) * jnp.arange(L, dtype=F32)[:, None]
    f = jnp.linspace(1e-4, HY_BANDS - 1, HY_BANDS, dtype=F32)[None, :]
    return jnp.concatenate([t, jnp.cos(f * w), -jnp.sin(f * w)], axis=-1)


def hyena_filter_spectrum(L, w1, b1, w2, b2, w3, b3, w4, freq):
    act = lambda a: jnp.sin(freq * a)
    hdn = act(hyena_pos_features(L) @ w1 + b1)
    hdn = act(hdn @ w2 + b2)
    hdn = act(hdn @ w3 + b3)
    h = (hdn @ w4).astype(F32).reshape(L, 2, HY_ORDER, HY_W)
    deltas = jnp.abs(jnp.linspace(math.log(HY_TARGET) / HY_SLOW_DECAY, math.log(HY_TARGET) / HY_FAST_DECAY,
                                  HY_ORDER * HY_W, dtype=F32)).reshape(HY_ORDER, HY_W)
    t = jnp.linspace(0.0, 1.0, L, dtype=F32)[:, None, None, None]
    h = h * jnp.exp(-t * deltas)
    h = h / jnp.sum(jnp.abs(h), axis=(0, 1), keepdims=True)
    k = jnp.concatenate([h[:, 0], jnp.zeros((1, HY_ORDER, HY_W), F32), h[1:, 1][::-1]], axis=0)
    return jnp.fft.rfft(k, axis=0)


def fft_long_conv(u, k_f):
    L = u.shape[1]
    u_f = jnp.fft.rfft(u.astype(F32), n=2 * L, axis=1)
    return jnp.fft.irfft(u_f * k_f[None], n=2 * L, axis=1)[:, :L]


def hyena_mixer(z, conv_w, conv_b, w1, b1, w2, b2, w3, b3, w4, freq, bias):
    L = z.shape[1]
    zc = short_conv3(z, conv_w, conv_b)
    x1, x2, v = jnp.split(zc, 3, axis=-1)
    k_f = hyena_filter_spectrum(L, w1, b1, w2, b2, w3, b3, w4, freq)
    y = v.astype(F32)
    for o, gate in enumerate((x1, x2)):
        y = gate * (fft_long_conv(y, k_f[:, o]) + y * bias[o])
    return y.astype(z.dtype)


def mla_heads(z, q_norm, w_uq, kv_norm, w_ukv):
    Bn, L, _ = z.shape
    q = (rms_norm(z[..., O_HY:O_Q], q_norm) @ w_uq).reshape(Bn, L, MLA_HEADS, MLA_NOPE + MLA_ROPE)
    kv = (rms_norm(z[..., O_Q:O_KV], kv_norm) @ w_ukv).reshape(Bn, L, MLA_HEADS, MLA_NOPE + MLA_V)
    return q[..., :MLA_NOPE], q[..., MLA_NOPE:], kv[..., :MLA_NOPE], z[..., O_KV:], kv[..., MLA_NOPE:]


def mla_attend(qn, qr, kn, kr, v):
    s = jnp.einsum('bqhd,bkhd->bhqk', qn, kn) + jnp.einsum('bqhr,bkr->bhqk', qr, kr)
    p = jax.nn.softmax(s.astype(F32) * MLA_SCALE, axis=-1).astype(v.dtype)
    return jnp.einsum('bhqk,bkhd->bqhd', p, v)


def even_mixer(in_ctx, in_lat, cos, sin, need_ctx, w_in, hy, mla, w_o):
    z_ctx = in_ctx @ w_in
    z_lat = in_lat @ w_in
    cqn, cqr, ckn, ckr, cv = mla_heads(z_ctx, *mla)
    lqn, lqr, lkn, lkr, lv = mla_heads(z_lat, *mla)
    lqr = apply_axial_rope(lqr, cos[:, None, :], sin[:, None, :])
    lkr = apply_axial_rope(lkr, cos, sin)
    kn = jnp.concatenate([ckn, lkn], axis=1)
    kr = jnp.concatenate([ckr, lkr], axis=1)
    vv = jnp.concatenate([cv, lv], axis=1)
    Bn, L = in_lat.shape[:2]
    nb = L // Q_BLOCK
    blocks = lambda t: t.reshape((Bn, nb, Q_BLOCK) + t.shape[2:]).swapaxes(0, 1)
    att = lax.map(lambda q: mla_attend(q[0], q[1], kn, kr, vv), (blocks(lqn), blocks(lqr)))
    att = att.swapaxes(0, 1).reshape(Bn, L, MLA_HEADS * MLA_V)
    y_lat = jnp.concatenate([hyena_mixer(z_lat[..., :O_HY], *hy), att.astype(z_lat.dtype)], axis=-1) @ w_o
    y_ctx = None
    if need_ctx:
        att_c = mla_attend(cqn, cqr, ckn, ckr, cv).reshape(Bn, in_ctx.shape[1], MLA_HEADS * MLA_V)
        y_ctx = jnp.concatenate([hyena_mixer(z_ctx[..., :O_HY], *hy), att_c.astype(z_ctx.dtype)], axis=-1) @ w_o
    return y_ctx, y_lat


def _scan_op(e1, e2):
    a1, b1 = e1
    a2, b2 = e2
    return a1 * a2, a2 * b1 + b2


def linear_scan(a_bar, bu, s0):
    bu = bu.at[:, 0].add(a_bar * s0)
    a = jnp.broadcast_to(a_bar, bu.shape)
    _, s = lax.associative_scan(_scan_op, (a, bu), axis=1)
    return s


def _rev(s, d):
    return s[:, ::-1] if d == 1 else s


def s5_mixer(in_ctx, in_lat, need_ctx, lam_re, lam_im, log_dt, b_re, b_im, c_re, c_im, d_skip, w_o, w_g):
    def to_blocks(h):
        Bn, L, _ = h.shape
        return h.astype(F32).reshape(Bn, L, S5_NBLK, S5_GB, S5_GC).transpose(2, 0, 1, 3, 4)

    def from_blocks(y):
        nbk, Bn, L = y.shape[:3]
        return y.transpose(1, 2, 0, 3, 4).reshape(Bn, L, D_MODEL)

    def par_blocks(a):
        a = a.astype(F32)
        return a.reshape((2, S5_NBLK, S5_GB) + a.shape[2:]).swapaxes(0, 1)

    lam = lax.complex(par_blocks(lam_re), par_blocks(lam_im))
    dt = jnp.exp(par_blocks(log_dt))
    bmat = lax.complex(par_blocks(b_re), par_blocks(b_im))
    cmat = lax.complex(par_blocks(c_re), par_blocks(c_im))

    def block_fn(args):
        u_ctx, u_lat, lm, dtb, bm, cm = args
        a_bar = jnp.exp(lm * dtb[..., None])
        b_bar = ((a_bar - 1.0) / lm)[..., None] * bm
        zero = jnp.zeros((u_ctx.shape[0], S5_GB, S5_P), jnp.complex64)
        outs_ctx, outs_lat = [], []
        for d in range(2):
            bu_ctx = jnp.einsum('blgc,gpc->blgp', _rev(u_ctx, d).astype(jnp.complex64), b_bar[d])
            bu_lat = jnp.einsum('blgc,gpc->blgp', _rev(u_lat, d).astype(jnp.complex64), b_bar[d])
            s_ctx = linear_scan(a_bar[d], bu_ctx, zero)
            s_lat = linear_scan(a_bar[d], bu_lat, s_ctx[:, -1])
            outs_lat.append(_rev(jnp.einsum('blgp,gcp->blgc', s_lat, cm[d]).real, d))
            if need_ctx:
                outs_ctx.append(_rev(jnp.einsum('blgp,gcp->blgc', s_ctx, cm[d]).real, d))
        y_lat = outs_lat[0] + outs_lat[1]
        if need_ctx:
            return y_lat, outs_ctx[0] + outs_ctx[1]
        return (y_lat,)

    outs = lax.map(block_fn, (to_blocks(in_ctx), to_blocks(in_lat), lam, dt, bmat, cmat))

    def finish(yb, h):
        y = from_blocks(yb) + d_skip * h.astype(F32)
        y = jax.nn.gelu(y).astype(h.dtype)
        return (y @ w_o) * jax.nn.sigmoid(y @ w_g)

    y_lat = finish(outs[0], in_lat)
    y_ctx = finish(outs[1], in_ctx) if need_ctx else None
    return y_ctx, y_lat


def peer_ffn(h, w_q, keys, u_tab, v_tab):
    Bn, L, D = h.shape
    hb = h.reshape(Bn * L // PEER_BLOCK, PEER_BLOCK, D)

    def block_fn(xb):
        q = (xb @ w_q).reshape(PEER_BLOCK, PEER_HEADS, 2, PEER_DK // 2)
        s = jnp.einsum('thsk,hsnk->thsn', q, keys).astype(F32)
        top_s, top_i = lax.top_k(s, PEER_TOPK)
        cand = top_s[:, :, 0, :, None] + top_s[:, :, 1, None, :]
        cand_s, cand_j = lax.top_k(cand.reshape(PEER_BLOCK, PEER_HEADS, PEER_TOPK * PEER_TOPK), PEER_TOPK)
        idx = (jnp.take_along_axis(top_i[:, :, 0], cand_j // PEER_TOPK, axis=-1) * PEER_NK
               + jnp.take_along_axis(top_i[:, :, 1], cand_j % PEER_TOPK, axis=-1))
        g = jax.nn.softmax(cand_s, axis=-1)
        act = jax.nn.gelu(jnp.einsum('thkd,td->thk', u_tab[idx], xb).astype(F32))
        return jnp.einsum('thk,thkd->td', (g * act).astype(xb.dtype), v_tab[idx])

    return lax.map(block_fn, hb).reshape(Bn, L, D)


def setup_inputs(seed: int = 0) -> dict:
    key = jax.random.key(seed)
    ks = list(jax.random.split(key, 48))

    def nrm(shape, scale):
        return jax.random.normal(ks.pop(), shape, F32) * scale

    def gain(shape):
        return 1.0 + nrm(shape, 0.02)

    D = D_MODEL
    return {
        'x': nrm((BATCH, SEQ, D), 1.0),
        'c': nrm((BATCH, D), 1.0),
        'ctx': nrm((BATCH, CTX_LEN, D), 1.0),
        'c_ctx': nrm((D,), 1.0),
        'mod_w': nrm((DEPTH, D, 6 * D), 0.5 * D ** -0.5),
        'mod_b': nrm((DEPTH, 6 * D), 0.02),
        'ln_mix_g': gain((DEPTH, D)),
        'ln_mix_b': nrm((DEPTH, D), 0.02),
        'ln_ffn_g': gain((DEPTH, D)),
        'ln_ffn_b': nrm((DEPTH, D), 0.02),
        'ev_w_in': nrm((N_EVEN, D, IN_W), D ** -0.5),
        'ev_conv_w': nrm((N_EVEN, 3, O_HY), 0.5),
        'ev_conv_b': nrm((N_EVEN, O_HY), 0.02),
        'hy_w1': nrm((N_EVEN, HY_EMB, HY_HID), HY_EMB ** -0.5),
        'hy_b1': nrm((N_EVEN, HY_HID), 0.02),
        'hy_w2': nrm((N_EVEN, HY_HID, HY_HID), HY_HID ** -0.5),
        'hy_b2': nrm((N_EVEN, HY_HID), 0.02),
        'hy_w3': nrm((N_EVEN, HY_HID, HY_HID), HY_HID ** -0.5),
        'hy_b3': nrm((N_EVEN, HY_HID), 0.02),
        'hy_w4': nrm((N_EVEN, HY_HID, 2 * HY_ORDER * HY_W), HY_HID ** -0.5),
        'hy_freq': gain((N_EVEN, HY_HID)),
        'hy_bias': nrm((N_EVEN, HY_ORDER, HY_W), 0.1),
        'mla_q_norm': gain((N_EVEN, MLA_Q_RANK)),
        'mla_w_uq': nrm((N_EVEN, MLA_Q_RANK, MLA_HEADS * (MLA_NOPE + MLA_ROPE)), MLA_Q_RANK ** -0.5),
        'mla_kv_norm': gain((N_EVEN, MLA_KV_RANK)),
        'mla_w_ukv': nrm((N_EVEN, MLA_KV_RANK, MLA_HEADS * (MLA_NOPE + MLA_V)), MLA_KV_RANK ** -0.5),
        'ev_w_o': nrm((N_EVEN, D, D), DN_BETA * D ** -0.5),
        's5_lam_re': -0.5 + nrm((N_ODD, 2, S5_NG, S5_P), 0.01),
        's5_lam_im': math.pi * jnp.arange(S5_P, dtype=F32) + nrm((N_ODD, 2, S5_NG, S5_P), 0.01),
        's5_log_dt': jax.random.uniform(ks.pop(), (N_ODD, 2, S5_NG), F32, math.log(1e-3), math.log(1e-1)),
        's5_b_re': nrm((N_ODD, 2, S5_NG, S5_P, S5_GC), (2 * S5_GC) ** -0.5),
        's5_b_im': nrm((N_ODD, 2, S5_NG, S5_P, S5_GC), (2 * S5_GC) ** -0.5),
        's5_c_re': nrm((N_ODD, 2, S5_NG, S5_GC, S5_P), S5_P ** -0.5),
        's5_c_im': nrm((N_ODD, 2, S5_NG, S5_GC, S5_P), S5_P ** -0.5),
        's5_d': nrm((N_ODD, D), 1.0),
        'od_w_o': nrm((N_ODD, D, D), DN_BETA * D ** -0.5),
        'od_w_g': nrm((N_ODD, D, D), D ** -0.5),
        'peer_w_q': nrm((DEPTH, D, PEER_HEADS * PEER_DK), D ** -0.5),
        'peer_keys': nrm((DEPTH, PEER_HEADS, 2, PEER_NK, PEER_DK // 2), (PEER_DK // 2) ** -0.5),
        'peer_u': nrm((DEPTH, PEER_N, D), D ** -0.5),
        'peer_v': nrm((DEPTH, PEER_N, D), DN_BETA * PEER_HEADS ** -0.5),
    }


def reference(x, c, ctx, c_ctx, mod_w, mod_b, ln_mix_g, ln_mix_b, ln_ffn_g, ln_ffn_b,
              ev_w_in, ev_conv_w, ev_conv_b, hy_w1, hy_b1, hy_w2, hy_b2, hy_w3, hy_b3, hy_w4,
              hy_freq, hy_bias, mla_q_norm, mla_w_uq, mla_kv_norm, mla_w_ukv, ev_w_o,
              s5_lam_re, s5_lam_im, s5_log_dt, s5_b_re, s5_b_im, s5_c_re, s5_c_im, s5_d,
              od_w_o, od_w_g, peer_w_q, peer_keys, peer_u, peer_v):
    cos, sin = axial_rope(x.shape[1])
    act_lat = jax.nn.silu(c)
    act_ctx = jax.nn.silu(c_ctx)
    h_lat, h_ctx = x, ctx
    for layer in range(DEPTH):
        need_ctx = layer < DEPTH - 1
        m_lat = jnp.split((act_lat @ mod_w[layer] + mod_b[layer])[:, None, :], 6, axis=-1)
        m_ctx = jnp.split((act_ctx @ mod_w[layer] + mod_b[layer])[None, None, :], 6, axis=-1)
        in_lat = h_lat * (1.0 + m_lat[1]) + m_lat[0]
        in_ctx = h_ctx * (1.0 + m_ctx[1]) + m_ctx[0]
        i = layer // 2
        if layer % 2 == 0:
            hy = (ev_conv_w[i], ev_conv_b[i], hy_w1[i], hy_b1[i], hy_w2[i], hy_b2[i], hy_w3[i], hy_b3[i],
                  hy_w4[i], hy_freq[i], hy_bias[i])
            mla = (mla_q_norm[i], mla_w_uq[i], mla_kv_norm[i], mla_w_ukv[i])
            y_ctx, y_lat = even_mixer(in_ctx, in_lat, cos, sin, need_ctx, ev_w_in[i], hy, mla, ev_w_o[i])
        else:
            y_ctx, y_lat = s5_mixer(in_ctx, in_lat, need_ctx, s5_lam_re[i], s5_lam_im[i], s5_log_dt[i],
                                    s5_b_re[i], s5_b_im[i], s5_c_re[i], s5_c_im[i], s5_d[i],
                                    od_w_o[i], od_w_g[i])
        h_lat = layer_norm(DN_ALPHA * h_lat + m_lat[2] * y_lat, ln_mix_g[layer], ln_mix_b[layer])
        f_lat = peer_ffn(h_lat * (1.0 + m_lat[4]) + m_lat[3], peer_w_q[layer], peer_keys[layer],
                         peer_u[layer], peer_v[layer])
        h_lat = layer_norm(DN_ALPHA * h_lat + m_lat[5] * f_lat, ln_ffn_g[layer], ln_ffn_b[layer])
        if need_ctx:
            h_ctx = layer_norm(DN_ALPHA * h_ctx + m_ctx[2] * y_ctx, ln_mix_g[layer], ln_mix_b[layer])
            f_ctx = peer_ffn(h_ctx * (1.0 + m_ctx[4]) + m_ctx[3], peer_w_q[layer], peer_keys[layer],
                             peer_u[layer], peer_v[layer])
            h_ctx = layer_norm(DN_ALPHA * h_ctx + m_ctx[5] * f_ctx, ln_ffn_g[layer], ln_ffn_b[layer])
    return h_lat
```

```python
import math
import functools
import jax
import jax.numpy as jnp
from jax import lax
from jax.experimental import pallas as pl
from jax.experimental.pallas import tpu as pltpu

D_MODEL = 2048
BATCH = 2
SEQ = 8192
DEPTH = 4

GRID_W = 64
CTX_LEN = 256

HY_W = 1024
HY_ORDER = 2
HY_EMB = 33
HY_BANDS = (HY_EMB - 1) // 2
HY_HID = 64
HY_FAST_DECAY = 0.3
HY_SLOW_DECAY = 1.5
HY_TARGET = 1e-2

MLA_HEADS = 8
MLA_NOPE = 128
MLA_ROPE = 64
MLA_V = 128
MLA_Q_RANK = 512
MLA_KV_RANK = 256
MLA_SCALE = (MLA_NOPE + MLA_ROPE) ** -0.5
Q_BLOCK = 128
ROPE_BASE = 10000.0

O_HY = 3 * HY_W
O_Q = O_HY + MLA_Q_RANK
O_KV = O_Q + MLA_KV_RANK
IN_W = O_KV + MLA_ROPE

S5_GC = 16
S5_NG = D_MODEL // S5_GC
S5_P = 64
S5_GB = 16
S5_NBLK = S5_NG // S5_GB

PEER_HEADS = 8
PEER_NK = 128
PEER_N = PEER_NK * PEER_NK
PEER_DK = 256
PEER_TOPK = 16
PEER_BLOCK = 128

N_EVEN = (DEPTH + 1) // 2
N_ODD = DEPTH // 2
DN_ALPHA = (2.0 * DEPTH) ** 0.25
DN_BETA = (8.0 * DEPTH) ** -0.25
LN_EPS = 1e-5
RMS_EPS = 1e-6
F32 = jnp.float32


def _ln_kernel(x_ref, g_ref, b_ref, o_ref):
    xf = x_ref[...]
    xc = xf - jnp.mean(xf, axis=-1, keepdims=True)
    var = jnp.mean(xc * xc, axis=-1, keepdims=True)
    o_ref[...] = xc * lax.rsqrt(var + LN_EPS) * g_ref[...] + b_ref[...]


def layer_norm_pallas(x, g, b):
    shp = x.shape
    x2 = x.reshape(-1, shp[-1])
    n, d = x2.shape
    tb = 256
    out = pl.pallas_call(
        _ln_kernel,
        grid=(n // tb,),
        in_specs=[pl.BlockSpec((tb, d), lambda i: (i, 0)),
                  pl.BlockSpec((1, d), lambda i: (0, 0)),
                  pl.BlockSpec((1, d), lambda i: (0, 0))],
        out_specs=pl.BlockSpec((tb, d), lambda i: (i, 0)),
        out_shape=jax.ShapeDtypeStruct((n, d), F32),
        name="layer_norm",
    )(x2, g.reshape(1, d), b.reshape(1, d))
    return out.reshape(shp)


def layer_norm(x, g, b):
    xf = x.astype(F32)
    xc = xf - jnp.mean(xf, axis=-1, keepdims=True)
    var = jnp.mean(xc * xc, axis=-1, keepdims=True)
    return (xc * lax.rsqrt(var + LN_EPS) * g + b).astype(x.dtype)


def rms_norm(x, g):
    xf = x.astype(F32)
    return (xf * lax.rsqrt(jnp.mean(xf * xf, axis=-1, keepdims=True) + RMS_EPS) * g).astype(x.dtype)


def axial_rope(L):
    rows = L // GRID_W
    row = jnp.repeat(jnp.arange(rows), GRID_W).astype(F32)
    col = jnp.tile(jnp.arange(GRID_W), rows).astype(F32)
    half = MLA_ROPE // 2
    inv = ROPE_BASE ** (-jnp.arange(0, half, 2, dtype=F32) / half)
    ar, ac = row[:, None] * inv, col[:, None] * inv
    ang = jnp.concatenate([ar, ar, ac, ac], axis=-1)
    return jnp.cos(ang), jnp.sin(ang)


def apply_axial_rope(x, cos, sin):
    xs = x.reshape(x.shape[:-1] + (2, 2, MLA_ROPE // 4))
    rot = jnp.concatenate([-xs[..., 1:, :], xs[..., :1, :]], axis=-2).reshape(x.shape)
    return (x * cos + rot * sin).astype(x.dtype)


def short_conv3(z, w, b):
    L = z.shape[1]
    zp = jnp.pad(z, ((0, 0), (1, 1), (0, 0)))
    return zp[:, :L] * w[0] + zp[:, 1:L + 1] * w[1] + zp[:, 2:] * w[2] + b


def hyena_pos_features(L):
    t = jnp.linspace(0.0, 1.0, L, dtype=F32)[:, None]
    w = (2.0 * math.pi / L) * jnp.arange(L, dtype=F32)[:, None]
    f = jnp.linspace(1e-4, HY_BANDS - 1, HY_BANDS, dtype=F32)[None, :]
    return jnp.concatenate([t, jnp.cos(f * w), -jnp.sin(f * w)], axis=-1)


def hyena_filter_spectrum(L, w1, b1, w2, b2, w3, b3, w4, freq):
    act = lambda a: jnp.sin(freq * a)
    hdn = act(hyena_pos_features(L) @ w1 + b1)
    hdn = act(hdn @ w2 + b2)
    hdn = act(hdn @ w3 + b3)
    h = (hdn @ w4).astype(F32).reshape(L, 2, HY_ORDER, HY_W)
    deltas = jnp.abs(jnp.linspace(math.log(HY_TARGET) / HY_SLOW_DECAY, math.log(HY_TARGET) / HY_FAST_DECAY,
                                  HY_ORDER * HY_W, dtype=F32)).reshape(HY_ORDER, HY_W)
    t = jnp.linspace(0.0, 1.0, L, dtype=F32)[:, None, None, None]
    h = h * jnp.exp(-t * deltas)
    h = h / jnp.sum(jnp.abs(h), axis=(0, 1), keepdims=True)
    k = jnp.concatenate([h[:, 0], jnp.zeros((1, HY_ORDER, HY_W), F32), h[1:, 1][::-1]], axis=0)
    return jnp.fft.rfft(k, axis=0)


def fft_long_conv(u, k_f):
    L = u.shape[1]
    u_f = jnp.fft.rfft(u.astype(F32), n=2 * L, axis=1)
    return jnp.fft.irfft(u_f * k_f[None], n=2 * L, axis=1)[:, :L]


def hyena_mixer(z, conv_w, conv_b, w1, b1, w2, b2, w3, b3, w4, freq, bias):
    L = z.shape[1]
    zc = short_conv3(z, conv_w, conv_b)
    x1, x2, v = jnp.split(zc, 3, axis=-1)
    k_f = hyena_filter_spectrum(L, w1, b1, w2, b2, w3, b3, w4, freq)
    y = v.astype(F32)
    for o, gate in enumerate((x1, x2)):
        y = gate * (fft_long_conv(y, k_f[:, o]) + y * bias[o])
    return y.astype(z.dtype)


def mla_heads(z, q_norm, w_uq, kv_norm, w_ukv):
    Bn, L, _ = z.shape
    q = (rms_norm(z[..., O_HY:O_Q], q_norm) @ w_uq).reshape(Bn, L, MLA_HEADS, MLA_NOPE + MLA_ROPE)
    kv = (rms_norm(z[..., O_Q:O_KV], kv_norm) @ w_ukv).reshape(Bn, L, MLA_HEADS, MLA_NOPE + MLA_V)
    return q[..., :MLA_NOPE], q[..., MLA_NOPE:], kv[..., :MLA_NOPE], z[..., O_KV:], kv[..., MLA_NOPE:]


def mla_attend(qn, qr, kn, kr, v):
    s = jnp.einsum('bqhd,bkhd->bhqk', qn, kn) + jnp.einsum('bqhr,bkr->bhqk', qr, kr)
    p = jax.nn.softmax(s.astype(F32) * MLA_SCALE, axis=-1).astype(v.dtype)
    return jnp.einsum('bhqk,bkhd->bqhd', p, v)


def even_mixer(in_ctx, in_lat, cos, sin, need_ctx, w_in, hy, mla, w_o):
    z_ctx = in_ctx @ w_in
    z_lat = in_lat @ w_in
    cqn, cqr, ckn, ckr, cv = mla_heads(z_ctx, *mla)
    lqn, lqr, lkn, lkr, lv = mla_heads(z_lat, *mla)
    lqr = apply_axial_rope(lqr, cos[:, None, :], sin[:, None, :])
    lkr = apply_axial_rope(lkr, cos, sin)
    kn = jnp.concatenate([ckn, lkn], axis=1)
    kr = jnp.concatenate([ckr, lkr], axis=1)
    vv = jnp.concatenate([cv, lv], axis=1)
    Bn, L = in_lat.shape[:2]
    nb = L // Q_BLOCK
    blocks = lambda t: t.reshape((Bn, nb, Q_BLOCK) + t.shape[2:]).swapaxes(0, 1)
    att = lax.map(lambda q: mla_attend(q[0], q[1], kn, kr, vv), (blocks(lqn), blocks(lqr)))
    att = att.swapaxes(0, 1).reshape(Bn, L, MLA_HEADS * MLA_V)
    y_lat = jnp.concatenate([hyena_mixer(z_lat[..., :O_HY], *hy), att.astype(z_lat.dtype)], axis=-1) @ w_o
    y_ctx = None
    if need_ctx:
        att_c = mla_attend(cqn, cqr, ckn, ckr, cv).reshape(Bn, in_ctx.shape[1], MLA_HEADS * MLA_V)
        y_ctx = jnp.concatenate([hyena_mixer(z_ctx[..., :O_HY], *hy), att_c.astype(z_ctx.dtype)], axis=-1) @ w_o
    return y_ctx, y_lat


def _scan_op(e1, e2):
    a1, b1 = e1
    a2, b2 = e2
    return a1 * a2, a2 * b1 + b2


def linear_scan(a_bar, bu, s0):
    bu = bu.at[:, 0].add(a_bar * s0)
    a = jnp.broadcast_to(a_bar, bu.shape)
    _, s = lax.associative_scan(_scan_op, (a, bu), axis=1)
    return s


def _rev(s, d):
    return s[:, ::-1] if d == 1 else s


def s5_mixer(in_ctx, in_lat, need_ctx, lam_re, lam_im, log_dt, b_re, b_im, c_re, c_im, d_skip, w_o, w_g):
    def to_blocks(h):
        Bn, L, _ = h.shape
        return h.astype(F32).reshape(Bn, L, S5_NBLK, S5_GB, S5_GC).transpose(2, 0, 1, 3, 4)

    def from_blocks(y):
        nbk, Bn, L = y.shape[:3]
        return y.transpose(1, 2, 0, 3, 4).reshape(Bn, L, D_MODEL)

    def par_blocks(a):
        a = a.astype(F32)
        return a.reshape((2, S5_NBLK, S5_GB) + a.shape[2:]).swapaxes(0, 1)

    lam = lax.complex(par_blocks(lam_re), par_blocks(lam_im))
    dt = jnp.exp(par_blocks(log_dt))
    bmat = lax.complex(par_blocks(b_re), par_blocks(b_im))
    cmat = lax.complex(par_blocks(c_re), par_blocks(c_im))

    def block_fn(args):
        u_ctx, u_lat, lm, dtb, bm, cm = args
        a_bar = jnp.exp(lm * dtb[..., None])
        b_bar = ((a_bar - 1.0) / lm)[..., None] * bm
        zero = jnp.zeros((u_ctx.shape[0], S5_GB, S5_P), jnp.complex64)
        outs_ctx, outs_lat = [], []
        for d in range(2):
            bu_ctx = jnp.einsum('blgc,gpc->blgp', _rev(u_ctx, d).astype(jnp.complex64), b_bar[d])
            bu_lat = jnp.einsum('blgc,gpc->blgp', _rev(u_lat, d).astype(jnp.complex64), b_bar[d])
            s_ctx = linear_scan(a_bar[d], bu_ctx, zero)
            s_lat = linear_scan(a_bar[d], bu_lat, s_ctx[:, -1])
            outs_lat.append(_rev(jnp.einsum('blgp,gcp->blgc', s_lat, cm[d]).real, d))
            if need_ctx:
                outs_ctx.append(_rev(jnp.einsum('blgp,gcp->blgc', s_ctx, cm[d]).real, d))
        y_lat = outs_lat[0] + outs_lat[1]
        if need_ctx:
            return y_lat, outs_ctx[0] + outs_ctx[1]
        return (y_lat,)

    outs = lax.map(block_fn, (to_blocks(in_ctx), to_blocks(in_lat), lam, dt, bmat, cmat))

    def finish(yb, h):
        y = from_blocks(yb) + d_skip * h.astype(F32)
        y = jax.nn.gelu(y).astype(h.dtype)
        return (y @ w_o) * jax.nn.sigmoid(y @ w_g)

    y_lat = finish(outs[0], in_lat)
    y_ctx = finish(outs[1], in_ctx) if need_ctx else None
    return y_ctx, y_lat


def peer_ffn(h, w_q, keys, u_tab, v_tab):
    Bn, L, D = h.shape
    hb = h.reshape(Bn * L // PEER_BLOCK, PEER_BLOCK, D)

    def block_fn(xb):
        q = (xb @ w_q).reshape(PEER_BLOCK, PEER_HEADS, 2, PEER_DK // 2)
        s = jnp.einsum('thsk,hsnk->thsn', q, keys).astype(F32)
        top_s, top_i = lax.top_k(s, PEER_TOPK)
        cand = top_s[:, :, 0, :, None] + top_s[:, :, 1, None, :]
        cand_s, cand_j = lax.top_k(cand.reshape(PEER_BLOCK, PEER_HEADS, PEER_TOPK * PEER_TOPK), PEER_TOPK)
        idx = (jnp.take_along_axis(top_i[:, :, 0], cand_j // PEER_TOPK, axis=-1) * PEER_NK
               + jnp.take_along_axis(top_i[:, :, 1], cand_j % PEER_TOPK, axis=-1))
        g = jax.nn.softmax(cand_s, axis=-1)
        act = jax.nn.gelu(jnp.einsum('thkd,td->thk', u_tab[idx], xb).astype(F32))
        return jnp.einsum('thk,thkd->td', (g * act).astype(xb.dtype), v_tab[idx])

    return lax.map(block_fn, hb).reshape(Bn, L, D)


def kernel(x, c, ctx, c_ctx, mod_w, mod_b, ln_mix_g, ln_mix_b, ln_ffn_g, ln_ffn_b,
           ev_w_in, ev_conv_w, ev_conv_b, hy_w1, hy_b1, hy_w2, hy_b2, hy_w3, hy_b3, hy_w4,
           hy_freq, hy_bias, mla_q_norm, mla_w_uq, mla_kv_norm, mla_w_ukv, ev_w_o,
           s5_lam_re, s5_lam_im, s5_log_dt, s5_b_re, s5_b_im, s5_c_re, s5_c_im, s5_d,
           od_w_o, od_w_g, peer_w_q, peer_keys, peer_u, peer_v):
    cos, sin = axial_rope(x.shape[1])
    act_lat = jax.nn.silu(c)
    act_ctx = jax.nn.silu(c_ctx)
    h_lat, h_ctx = x, ctx
    for layer in range(DEPTH):
        need_ctx = layer < DEPTH - 1
        m_lat = jnp.split((act_lat @ mod_w[layer] + mod_b[layer])[:, None, :], 6, axis=-1)
        m_ctx = jnp.split((act_ctx @ mod_w[layer] + mod_b[layer])[None, None, :], 6, axis=-1)
        in_lat = h_lat * (1.0 + m_lat[1]) + m_lat[0]
        in_ctx = h_ctx * (1.0 + m_ctx[1]) + m_ctx[0]
        i = layer // 2
        if layer % 2 == 0:
            hy = (ev_conv_w[i], ev_conv_b[i], hy_w1[i], hy_b1[i], hy_w2[i], hy_b2[i], hy_w3[i], hy_b3[i],
                  hy_w4[i], hy_freq[i], hy_bias[i])
            mla = (mla_q_norm[i], mla_w_uq[i], mla_kv_norm[i], mla_w_ukv[i])
            y_ctx, y_lat = even_mixer(in_ctx, in_lat, cos, sin, need_ctx, ev_w_in[i], hy, mla, ev_w_o[i])
        else:
            y_ctx, y_lat = s5_mixer(in_ctx, in_lat, need_ctx, s5_lam_re[i], s5_lam_im[i], s5_log_dt[i],
                                    s5_b_re[i], s5_b_im[i], s5_c_re[i], s5_c_im[i], s5_d[i],
                                    od_w_o[i], od_w_g[i])
        h_lat = layer_norm(DN_ALPHA * h_lat + m_lat[2] * y_lat, ln_mix_g[layer], ln_mix_b[layer])
        f_lat = peer_ffn(h_lat * (1.0 + m_lat[4]) + m_lat[3], peer_w_q[layer], peer_keys[layer],
                         peer_u[layer], peer_v[layer])
        pre = DN_ALPHA * h_lat + m_lat[5] * f_lat
        if layer == DEPTH - 1:
            h_lat = layer_norm_pallas(pre, ln_ffn_g[layer], ln_ffn_b[layer])
        else:
            h_lat = layer_norm(pre, ln_ffn_g[layer], ln_ffn_b[layer])
        if need_ctx:
            h_ctx = layer_norm(DN_ALPHA * h_ctx + m_ctx[2] * y_ctx, ln_mix_g[layer], ln_mix_b[layer])
            f_ctx = peer_ffn(h_ctx * (1.0 + m_ctx[4]) + m_ctx[3], peer_w_q[layer], peer_keys[layer],
                             peer_u[layer], peer_v[layer])
            h_ctx = layer_norm(DN_ALPHA * h_ctx + m_ctx[5] * f_ctx, ln_ffn_g[layer], ln_ffn_b[layer])
    return h_lat
```

```python
import math
import functools
import jax
import jax.numpy as jnp
from jax import lax
from jax.experimental import pallas as pl
from jax.experimental.pallas import tpu as pltpu

D_MODEL = 2048
BATCH = 2
SEQ = 8192
DEPTH = 4

GRID_W = 64
CTX_LEN = 256

HY_W = 1024
HY_ORDER = 2
HY_EMB = 33
HY_BANDS = (HY_EMB - 1) // 2
HY_HID = 64
HY_FAST_DECAY = 0.3
HY_SLOW_DECAY = 1.5
HY_TARGET = 1e-2

MLA_HEADS = 8
MLA_NOPE = 128
MLA_ROPE = 64
MLA_V = 128
MLA_Q_RANK = 512
MLA_KV_RANK = 256
MLA_SCALE = (MLA_NOPE + MLA_ROPE) ** -0.5
Q_BLOCK = 128
ROPE_BASE = 10000.0

O_HY = 3 * HY_W
O_Q = O_HY + MLA_Q_RANK
O_KV = O_Q + MLA_KV_RANK
IN_W = O_KV + MLA_ROPE

S5_GC = 16
S5_NG = D_MODEL // S5_GC
S5_P = 64
S5_GB = 16
S5_NBLK = S5_NG // S5_GB

PEER_HEADS = 8
PEER_NK = 128
PEER_N = PEER_NK * PEER_NK
PEER_DK = 256
PEER_TOPK = 16
PEER_BLOCK = 128

N_EVEN = (DEPTH + 1) // 2
N_ODD = DEPTH // 2
DN_ALPHA = (2.0 * DEPTH) ** 0.25
DN_BETA = (8.0 * DEPTH) ** -0.25
LN_EPS = 1e-5
RMS_EPS = 1e-6
F32 = jnp.float32


def _ln_kernel(x_ref, g_ref, b_ref, o_ref):
    xf = x_ref[...]
    xc = xf - jnp.mean(xf, axis=-1, keepdims=True)
    var = jnp.mean(xc * xc, axis=-1, keepdims=True)
    o_ref[...] = xc * lax.rsqrt(var + LN_EPS) * g_ref[...] + b_ref[...]


def layer_norm_pallas(x, g, b):
    shp = x.shape
    x2 = x.reshape(-1, shp[-1])
    n, d = x2.shape
    tb = 256
    out = pl.pallas_call(
        _ln_kernel,
        grid=(n // tb,),
        in_specs=[pl.BlockSpec((tb, d), lambda i: (i, 0)),
                  pl.BlockSpec((1, d), lambda i: (0, 0)),
                  pl.BlockSpec((1, d), lambda i: (0, 0))],
        out_specs=pl.BlockSpec((tb, d), lambda i: (i, 0)),
        out_shape=jax.ShapeDtypeStruct((n, d), F32),
        name="layer_norm",
    )(x2, g.reshape(1, d), b.reshape(1, d))
    return out.reshape(shp)


BF16 = jnp.bfloat16
HI = lax.Precision.HIGHEST
VMEM_LIMIT = 56 * 1024 * 1024

S5_T = 16
S5_CB = 8


def _s5_operators(lam_re, lam_im, log_dt, b_re, b_im, c_re, c_im):
    T = S5_T
    ng, p, gc = b_re.shape[1:]
    dt = jnp.exp(log_dt)[..., None]
    zr, zi = lam_re * dt, lam_im * dt
    k = jnp.arange(T + 1, dtype=F32)[:, None, None, None]
    mag = jnp.exp(zr[None] * k)
    pr, pi = mag * jnp.cos(zi[None] * k), mag * jnp.sin(zi[None] * k)
    ar, ai = pr[1], pi[1]
    den = lam_re * lam_re + lam_im * lam_im
    cr = ((ar - 1.0) * lam_re + ai * lam_im) / den
    ci = (ai * lam_re - (ar - 1.0) * lam_im) / den
    bbr = cr[..., None] * b_re - ci[..., None] * b_im
    bbi = cr[..., None] * b_im + ci[..., None] * b_re
    car = c_re[None] * pr[:, :, :, None, :] - c_im[None] * pi[:, :, :, None, :]
    cai = c_re[None] * pi[:, :, :, None, :] + c_im[None] * pr[:, :, :, None, :]
    kk = (jnp.einsum('tdgcp,dgpe->tdgce', car[:T], bbr, precision=HI)
          - jnp.einsum('tdgcp,dgpe->tdgce', cai[:T], bbi, precision=HI))
    kf, kb = kk[:, 0], kk[:, 1]
    kfull = jnp.concatenate([kb[1:][::-1], (kf[0] + kb[0])[None], kf[1:]], axis=0)
    jj = jnp.arange(T)
    idx = jj[None, :] - jj[:, None] + (T - 1)
    m = kfull[idx]
    m = m.transpose(2, 0, 4, 1, 3).reshape(ng, T * gc, T * gc)
    def e_w(d, pw_idx):
        er = pr[pw_idx, d][..., None] * bbr[d][None] - pi[pw_idx, d][..., None] * bbi[d][None]
        ei = pr[pw_idx, d][..., None] * bbi[d][None] + pi[pw_idx, d][..., None] * bbr[d][None]
        w = jnp.concatenate([er, ei], axis=2)
        return w.transpose(1, 0, 3, 2).reshape(ng, T * gc, 2 * p)
    w_e = jnp.concatenate([e_w(0, T - 1 - jj), e_w(1, jj)], axis=-1)
    def s_w(d, pw_idx):
        w = jnp.concatenate([car[pw_idx, d], -cai[pw_idx, d]], axis=-1)
        return w.transpose(1, 3, 0, 2).reshape(ng, 2 * p, T * gc)
    w_y = jnp.concatenate([m, s_w(0, jj + 1), s_w(1, T - jj)], axis=1)
    a1 = jnp.concatenate([pr[T], pr[T]], axis=-1)
    a2 = jnp.concatenate([-pi[T], pi[T]], axis=-1)
    return w_e, w_y, a1, a2


def _s5_end_state_kernel(x_ref, w_ref, ef_ref, eb_ref):
    e = jnp.dot(x_ref[0], w_ref[0], precision=HI, preferred_element_type=F32)
    half = e.shape[1] // 2
    ef_ref[...] = e[:, :half]
    eb_ref[...] = e[:, half:]


def _s5_carry_kernel(ef_ref, eb_ref, a1_ref, a2_ref, s0f_ref, s0b_ref, sf_scr, sb_scr):
    @pl.when(pl.program_id(1) == 0)
    def _():
        sf_scr[...] = jnp.zeros_like(sf_scr)
        sb_scr[...] = jnp.zeros_like(sb_scr)

    half = sf_scr.shape[-1] // 2

    def step(s, e, d):
        return a1_ref[d] * s + a2_ref[d] * pltpu.roll(s, half, 1) + e

    for r in range(S5_CB):
        s = sf_scr[...]
        s0f_ref[r] = s
        sf_scr[...] = step(s, ef_ref[r], 0)
    for r in reversed(range(S5_CB)):
        s = sb_scr[...]
        s0b_ref[r] = s
        sb_scr[...] = step(s, eb_ref[r], 1)


def _s5_output_kernel(x_ref, s0f_ref, s0b_ref, w_ref, y_ref):
    kx = x_ref.shape[-1]
    ks = s0f_ref.shape[-1]
    w = w_ref[0]
    y = jnp.dot(x_ref[0], w[:kx], precision=HI, preferred_element_type=F32)
    y = y + jnp.dot(s0f_ref[...], w[kx:kx + ks], precision=HI, preferred_element_type=F32)
    y = y + jnp.dot(s0b_ref[...], w[kx + ks:], precision=HI, preferred_element_type=F32)
    y_ref[0] = y


def s5_scan(in_ctx, in_lat, lam_re, lam_im, log_dt, b_re, b_im, c_re, c_im):
    T = S5_T
    bn, lc, d = in_ctx.shape
    ll = in_lat.shape[1]
    ng, p, gc = b_re.shape[1:]
    w_e, w_y, a1, a2 = _s5_operators(lam_re, lam_im, log_dt, b_re, b_im, c_re, c_im)
    u = jnp.concatenate([in_ctx, in_lat], axis=1)
    nch = (lc + ll) // T
    rows = bn * nch
    kx, ks = T * gc, 2 * p
    xg = u.reshape(bn, nch, T, ng, gc).transpose(3, 0, 1, 2, 4).reshape(ng, rows, kx)

    ef, eb = pl.pallas_call(
        _s5_end_state_kernel,
        grid=(ng,),
        in_specs=[pl.BlockSpec((1, rows, kx), lambda g: (g, 0, 0)),
                  pl.BlockSpec((1, kx, 2 * ks), lambda g: (g, 0, 0))],
        out_specs=[pl.BlockSpec((rows, ks), lambda g: (0, g)),
                   pl.BlockSpec((rows, ks), lambda g: (0, g))],
        out_shape=[jax.ShapeDtypeStruct((rows, ng * ks), F32)] * 2,
        name="s5_end_state",
    )(xg, w_e)

    nblk = nch // S5_CB
    cblk = (lc // T) // S5_CB
    fwd_map = lambda b, j: (b * nblk + j, 0, 0)
    bwd_map = lambda b, j: (b * nblk + jnp.where(j < cblk, cblk - 1 - j, nblk + cblk - 1 - j), 0, 0)
    s0f, s0b = pl.pallas_call(
        _s5_carry_kernel,
        grid=(bn, nblk),
        in_specs=[pl.BlockSpec((S5_CB, ng, ks), fwd_map),
                  pl.BlockSpec((S5_CB, ng, ks), bwd_map),
                  pl.BlockSpec((2, ng, ks), lambda b, j: (0, 0, 0)),
                  pl.BlockSpec((2, ng, ks), lambda b, j: (0, 0, 0))],
        out_specs=[pl.BlockSpec((S5_CB, ng, ks), fwd_map),
                   pl.BlockSpec((S5_CB, ng, ks), bwd_map)],
        out_shape=[jax.ShapeDtypeStruct((rows, ng, ks), F32)] * 2,
        scratch_shapes=[pltpu.VMEM((ng, ks), F32), pltpu.VMEM((ng, ks), F32)],
        name="s5_carry",
    )(ef.reshape(rows, ng, ks), eb.reshape(rows, ng, ks), a1, a2)

    yg = pl.pallas_call(
        _s5_output_kernel,
        grid=(ng,),
        in_specs=[pl.BlockSpec((1, rows, kx), lambda g: (g, 0, 0)),
                  pl.BlockSpec((rows, ks), lambda g: (0, g)),
                  pl.BlockSpec((rows, ks), lambda g: (0, g)),
                  pl.BlockSpec((1, kx + 2 * ks, kx), lambda g: (g, 0, 0))],
        out_specs=pl.BlockSpec((1, rows, kx), lambda g: (g, 0, 0)),
        out_shape=jax.ShapeDtypeStruct((ng, rows, kx), F32),
        name="s5_output",
    )(xg, s0f.reshape(rows, ng * ks), s0b.reshape(rows, ng * ks), w_y)
    y = yg.reshape(ng, bn, nch, T, gc).transpose(1, 2, 3, 0, 4).reshape(bn, lc + ll, d)
    return y[:, :lc], y[:, lc:]


PEER_TS = 256
PEER_TB = 512
PEER_EB = 1024
LANES = 128
NEG = -3.0e38
_NN = (((1,), (0,)), ((), ()))
_NT = (((1,), (1,)), ((), ()))


def _split_bf16(a):
    hi = a.astype(BF16)
    return hi, (a - hi.astype(F32)).astype(BF16)


def _dot3(ah, al, bh, bl, dims):
    f = lambda a, b: lax.dot_general(a, b, dims, preferred_element_type=F32)
    return f(ah, bh) + f(al, bh) + f(ah, bl)


def _peer_cells():
    return PEER_TOPK + 8 * (PEER_TOPK // 2 - 1) + 8


def _peer_score_kernel(x_ref, wh_ref, wl_ref, kh_ref, kl_ref,
                       s1_ref, w1_ref, s2_ref, e2_ref, tau_ref,
                       st_scr, a1_scr, a2_scr, c_scr):
    nh, _, nk, dk = kh_ref.shape
    tb = x_ref.shape[0]
    xh, xl = _split_bf16(x_ref[...])
    q = _dot3(xh, xl, wh_ref[...], wl_ref[...], _NN)
    qh, ql = _split_bf16(q)
    for h in range(nh):
        for s in range(2):
            c0 = (2 * h + s) * dk
            st_scr[2 * h + s] = _dot3(kh_ref[h, s], kl_ref[h, s],
                                      qh[:, c0:c0 + dk], ql[:, c0:c0 + dk], _NT)

    def top_values(s, out_scr):
        for r in range(PEER_TOPK):
            m = jnp.max(s, axis=0, keepdims=True)
            out_scr[r:r + 1, :] = m
            s = jnp.where(s == m, NEG, s)

    nsub = tb // LANES
    half = PEER_TOPK // 2

    def body(it, carry):
        h = it // nsub
        lanes = pl.ds(pl.multiple_of((it % nsub) * LANES, LANES), LANES)
        s1 = st_scr[2 * h, :, lanes]
        s2 = st_scr[2 * h + 1, :, lanes]
        top_values(s1, a1_scr)
        top_values(s2, a2_scr)
        a1 = a1_scr[...]
        a2 = a2_scr[...]
        c_scr[0:PEER_TOPK, :] = a1[0:1] + a2
        for k in range(1, half):
            c_scr[PEER_TOPK + 8 * (k - 1):PEER_TOPK + 8 * k, :] = a1[k:k + 1] + a2[0:8]
        c_scr[PEER_TOPK + 8 * (half - 1):, :] = a1[half:] + a2[0:1]
        c = c_scr[...]
        m0 = a1[0:1] + a2[0:1]
        z = jnp.zeros_like(m0)
        m = m0
        for r in range(PEER_TOPK):
            m = jnp.max(c, axis=0, keepdims=True)
            z = z + jnp.exp(m - m0)
            c = jnp.where(c == m, NEG, c)
        s1_ref[h, :, lanes] = s1
        w1_ref[h, :, lanes] = jnp.exp(s1 - a1[0:1]) / z
        s2_ref[h, :, lanes] = s2
        e2_ref[h, :, lanes] = jnp.exp(s2 - a2[0:1])
        tau_ref[h, :, lanes] = jnp.broadcast_to(m, (8, LANES))
        return carry

    lax.fori_loop(0, nh * nsub, body, 0)


def _peer_expert_kernel(xt_ref, u_ref, vt_ref, s1_ref, w1_ref, s2_ref, e2_ref, tau_ref,
                        o_ref, acc_scr, a_scr, p_scr):
    e = pl.program_id(1)

    @pl.when(e == 0)
    def _():
        acc_scr[...] = jnp.zeros_like(acc_scr)

    a_scr[...] = jnp.dot(u_ref[...], xt_ref[...], preferred_element_type=F32)
    nh, nk, tb = s2_ref.shape
    ni = a_scr.shape[0] // nk

    def body(ts, carry):
        lanes = pl.ds(pl.multiple_of(ts * LANES, LANES), LANES)
        for ii in range(ni):
            rows = slice(ii * nk, (ii + 1) * nk)
            g = jnp.zeros((nk, LANES), F32)
            for h in range(nh):
                pair = s2_ref[h, :, lanes] + s1_ref[h, ii:ii + 1, lanes]
                keep = pair >= tau_ref[h, 0:1, lanes]
                g = g + jnp.where(keep, e2_ref[h, :, lanes], 0.0) * w1_ref[h, ii:ii + 1, lanes]
            p_scr[rows, lanes] = (g * jax.nn.gelu(a_scr[rows, lanes])).astype(BF16)
        return carry

    lax.fori_loop(0, tb // LANES, body, 0)
    acc_scr[...] += jnp.dot(vt_ref[...], p_scr[...], preferred_element_type=F32)

    @pl.when(e == pl.num_programs(1) - 1)
    def _():
        o_ref[...] = acc_scr[...].T


def peer_ffn_pallas(xin, w_q, keys, u_bf, vt_bf):
    n, d = xin.shape
    nh, _, nk, dk = keys.shape
    ne = u_bf.shape[0]
    ts = min(PEER_TS, n)
    tb = min(PEER_TB, n)
    eb = min(PEER_EB, ne)
    wh, wl = _split_bf16(w_q)
    kh, kl = _split_bf16(keys)
    feat = jax.ShapeDtypeStruct((nh, nk, n), F32)
    feat_spec = pl.BlockSpec((nh, nk, ts), lambda i: (0, 0, i))
    s1, w1, s2, e2, tau = pl.pallas_call(
        _peer_score_kernel,
        grid=(n // ts,),
        in_specs=[pl.BlockSpec((ts, d), lambda i: (i, 0)),
                  pl.BlockSpec(wh.shape, lambda i: (0, 0)),
                  pl.BlockSpec(wl.shape, lambda i: (0, 0)),
                  pl.BlockSpec(kh.shape, lambda i: (0, 0, 0, 0)),
                  pl.BlockSpec(kl.shape, lambda i: (0, 0, 0, 0))],
        out_specs=[feat_spec, feat_spec, feat_spec, feat_spec,
                   pl.BlockSpec((nh, 8, ts), lambda i: (0, 0, i))],
        out_shape=[feat, feat, feat, feat, jax.ShapeDtypeStruct((nh, 8, n), F32)],
        scratch_shapes=[pltpu.VMEM((2 * nh, nk, ts), F32),
                        pltpu.VMEM((PEER_TOPK, LANES), F32),
                        pltpu.VMEM((PEER_TOPK, LANES), F32),
                        pltpu.VMEM((_peer_cells(), LANES), F32)],
        compiler_params=pltpu.CompilerParams(vmem_limit_bytes=VMEM_LIMIT),
        name="peer_scores",
    )(xin, wh, wl, kh, kl)

    ni = eb // nk
    row_spec = pl.BlockSpec((nh, ni, tb), lambda t, e: (0, e, t))
    col_spec = pl.BlockSpec((nh, nk, tb), lambda t, e: (0, 0, t))
    return pl.pallas_call(
        _peer_expert_kernel,
        grid=(n // tb, ne // eb),
        in_specs=[pl.BlockSpec((d, tb), lambda t, e: (0, t)),
                  pl.BlockSpec((eb, d), lambda t, e: (e, 0)),
                  pl.BlockSpec((d, eb), lambda t, e: (0, e)),
                  row_spec, row_spec, col_spec, col_spec,
                  pl.BlockSpec((nh, 8, tb), lambda t, e: (0, 0, t))],
        out_specs=pl.BlockSpec((tb, d), lambda t, e: (t, 0)),
        out_shape=jax.ShapeDtypeStruct((n, d), F32),
        scratch_shapes=[pltpu.VMEM((d, tb), F32),
                        pltpu.VMEM((eb, tb), F32),
                        pltpu.VMEM((eb, tb), BF16)],
        compiler_params=pltpu.CompilerParams(vmem_limit_bytes=VMEM_LIMIT,
                                             dimension_semantics=("arbitrary", "arbitrary")),
        name="peer_experts",
    )(xin.astype(BF16).T, u_bf, vt_bf, s1, w1, s2, e2, tau)


def layer_norm(x, g, b):
    xf = x.astype(F32)
    xc = xf - jnp.mean(xf, axis=-1, keepdims=True)
    var = jnp.mean(xc * xc, axis=-1, keepdims=True)
    return (xc * lax.rsqrt(var + LN_EPS) * g + b).astype(x.dtype)


def rms_norm(x, g):
    xf = x.astype(F32)
    return (xf * lax.rsqrt(jnp.mean(xf * xf, axis=-1, keepdims=True) + RMS_EPS) * g).astype(x.dtype)


def axial_rope(L):
    rows = L // GRID_W
    row = jnp.repeat(jnp.arange(rows), GRID_W).astype(F32)
    col = jnp.tile(jnp.arange(GRID_W), rows).astype(F32)
    half = MLA_ROPE // 2
    inv = ROPE_BASE ** (-jnp.arange(0, half, 2, dtype=F32) / half)
    ar, ac = row[:, None] * inv, col[:, None] * inv
    ang = jnp.concatenate([ar, ar, ac, ac], axis=-1)
    return jnp.cos(ang), jnp.sin(ang)


def apply_axial_rope(x, cos, sin):
    xs = x.reshape(x.shape[:-1] + (2, 2, MLA_ROPE // 4))
    rot = jnp.concatenate([-xs[..., 1:, :], xs[..., :1, :]], axis=-2).reshape(x.shape)
    return (x * cos + rot * sin).astype(x.dtype)


def short_conv3(z, w, b):
    L = z.shape[1]
    zp = jnp.pad(z, ((0, 0), (1, 1), (0, 0)))
    return zp[:, :L] * w[0] + zp[:, 1:L + 1] * w[1] + zp[:, 2:] * w[2] + b


def hyena_pos_features(L):
    t = jnp.linspace(0.0, 1.0, L, dtype=F32)[:, None]
    w = (2.0 * math.pi / L) * jnp.arange(L, dtype=F32)[:, None]
    f = jnp.linspace(1e-4, HY_BANDS - 1, HY_BANDS, dtype=F32)[None, :]
    return jnp.concatenate([t, jnp.cos(f * w), -jnp.sin(f * w)], axis=-1)


def hyena_filter_spectrum(L, w1, b1, w2, b2, w3, b3, w4, freq):
    act = lambda a: jnp.sin(freq * a)
    hdn = act(hyena_pos_features(L) @ w1 + b1)
    hdn = act(hdn @ w2 + b2)
    hdn = act(hdn @ w3 + b3)
    h = (hdn @ w4).astype(F32).reshape(L, 2, HY_ORDER, HY_W)
    deltas = jnp.abs(jnp.linspace(math.log(HY_TARGET) / HY_SLOW_DECAY, math.log(HY_TARGET) / HY_FAST_DECAY,
                                  HY_ORDER * HY_W, dtype=F32)).reshape(HY_ORDER, HY_W)
    t = jnp.linspace(0.0, 1.0, L, dtype=F32)[:, None, None, None]
    h = h * jnp.exp(-t * deltas)
    h = h / jnp.sum(jnp.abs(h), axis=(0, 1), keepdims=True)
    k = jnp.concatenate([h[:, 0], jnp.zeros((1, HY_ORDER, HY_W), F32), h[1:, 1][::-1]], axis=0)
    return jnp.fft.rfft(k, axis=0)


def fft_long_conv(u, k_f):
    L = u.shape[1]
    u_f = jnp.fft.rfft(u.astype(F32), n=2 * L, axis=1)
    return jnp.fft.irfft(u_f * k_f[None], n=2 * L, axis=1)[:, :L]


def hyena_mixer(z, conv_w, conv_b, w1, b1, w2, b2, w3, b3, w4, freq, bias):
    L = z.shape[1]
    zc = short_conv3(z, conv_w, conv_b)
    x1, x2, v = jnp.split(zc, 3, axis=-1)
    k_f = hyena_filter_spectrum(L, w1, b1, w2, b2, w3, b3, w4, freq)
    y = v.astype(F32)
    for o, gate in enumerate((x1, x2)):
        y = gate * (fft_long_conv(y, k_f[:, o]) + y * bias[o])
    return y.astype(z.dtype)


def mla_heads(z, q_norm, w_uq, kv_norm, w_ukv):
    Bn, L, _ = z.shape
    q = (rms_norm(z[..., O_HY:O_Q], q_norm) @ w_uq).reshape(Bn, L, MLA_HEADS, MLA_NOPE + MLA_ROPE)
    kv = (rms_norm(z[..., O_Q:O_KV], kv_norm) @ w_ukv).reshape(Bn, L, MLA_HEADS, MLA_NOPE + MLA_V)
    return q[..., :MLA_NOPE], q[..., MLA_NOPE:], kv[..., :MLA_NOPE], z[..., O_KV:], kv[..., MLA_NOPE:]


def mla_attend(qn, qr, kn, kr, v):
    s = jnp.einsum('bqhd,bkhd->bhqk', qn, kn) + jnp.einsum('bqhr,bkr->bhqk', qr, kr)
    p = jax.nn.softmax(s.astype(F32) * MLA_SCALE, axis=-1).astype(v.dtype)
    return jnp.einsum('bhqk,bkhd->bqhd', p, v)


def even_mixer(in_ctx, in_lat, cos, sin, need_ctx, w_in, hy, mla, w_o):
    z_ctx = in_ctx @ w_in
    z_lat = in_lat @ w_in
    cqn, cqr, ckn, ckr, cv = mla_heads(z_ctx, *mla)
    lqn, lqr, lkn, lkr, lv = mla_heads(z_lat, *mla)
    lqr = apply_axial_rope(lqr, cos[:, None, :], sin[:, None, :])
    lkr = apply_axial_rope(lkr, cos, sin)
    kn = jnp.concatenate([ckn, lkn], axis=1)
    kr = jnp.concatenate([ckr, lkr], axis=1)
    vv = jnp.concatenate([cv, lv], axis=1)
    Bn, L = in_lat.shape[:2]
    nb = L // Q_BLOCK
    blocks = lambda t: t.reshape((Bn, nb, Q_BLOCK) + t.shape[2:]).swapaxes(0, 1)
    att = lax.map(lambda q: mla_attend(q[0], q[1], kn, kr, vv), (blocks(lqn), blocks(lqr)))
    att = att.swapaxes(0, 1).reshape(Bn, L, MLA_HEADS * MLA_V)
    y_lat = jnp.concatenate([hyena_mixer(z_lat[..., :O_HY], *hy), att.astype(z_lat.dtype)], axis=-1) @ w_o
    y_ctx = None
    if need_ctx:
        att_c = mla_attend(cqn, cqr, ckn, ckr, cv).reshape(Bn, in_ctx.shape[1], MLA_HEADS * MLA_V)
        y_ctx = jnp.concatenate([hyena_mixer(z_ctx[..., :O_HY], *hy), att_c.astype(z_ctx.dtype)], axis=-1) @ w_o
    return y_ctx, y_lat


def _scan_op(e1, e2):
    a1, b1 = e1
    a2, b2 = e2
    return a1 * a2, a2 * b1 + b2


def linear_scan(a_bar, bu, s0):
    bu = bu.at[:, 0].add(a_bar * s0)
    a = jnp.broadcast_to(a_bar, bu.shape)
    _, s = lax.associative_scan(_scan_op, (a, bu), axis=1)
    return s


def _rev(s, d):
    return s[:, ::-1] if d == 1 else s


def s5_mixer(in_ctx, in_lat, need_ctx, lam_re, lam_im, log_dt, b_re, b_im, c_re, c_im, d_skip, w_o, w_g):
    s_ctx, s_lat = s5_scan(in_ctx, in_lat, lam_re, lam_im, log_dt, b_re, b_im, c_re, c_im)

    def finish(ys, h):
        y = ys + d_skip * h.astype(F32)
        y = jax.nn.gelu(y).astype(h.dtype)
        return (y @ w_o) * jax.nn.sigmoid(y @ w_g)

    y_lat = finish(s_lat, in_lat)
    y_ctx = finish(s_ctx, in_ctx) if need_ctx else None
    return y_ctx, y_lat


def peer_ffn(h, w_q, keys, u_tab, v_tab):
    Bn, L, D = h.shape
    hb = h.reshape(Bn * L // PEER_BLOCK, PEER_BLOCK, D)

    def block_fn(xb):
        q = (xb @ w_q).reshape(PEER_BLOCK, PEER_HEADS, 2, PEER_DK // 2)
        s = jnp.einsum('thsk,hsnk->thsn', q, keys).astype(F32)
        top_s, top_i = lax.top_k(s, PEER_TOPK)
        cand = top_s[:, :, 0, :, None] + top_s[:, :, 1, None, :]
        cand_s, cand_j = lax.top_k(cand.reshape(PEER_BLOCK, PEER_HEADS, PEER_TOPK * PEER_TOPK), PEER_TOPK)
        idx = (jnp.take_along_axis(top_i[:, :, 0], cand_j // PEER_TOPK, axis=-1) * PEER_NK
               + jnp.take_along_axis(top_i[:, :, 1], cand_j % PEER_TOPK, axis=-1))
        g = jax.nn.softmax(cand_s, axis=-1)
        act = jax.nn.gelu(jnp.einsum('thkd,td->thk', u_tab[idx], xb).astype(F32))
        return jnp.einsum('thk,thkd->td', (g * act).astype(xb.dtype), v_tab[idx])

    return lax.map(block_fn, hb).reshape(Bn, L, D)


def kernel(x, c, ctx, c_ctx, mod_w, mod_b, ln_mix_g, ln_mix_b, ln_ffn_g, ln_ffn_b,
           ev_w_in, ev_conv_w, ev_conv_b, hy_w1, hy_b1, hy_w2, hy_b2, hy_w3, hy_b3, hy_w4,
           hy_freq, hy_bias, mla_q_norm, mla_w_uq, mla_kv_norm, mla_w_ukv, ev_w_o,
           s5_lam_re, s5_lam_im, s5_log_dt, s5_b_re, s5_b_im, s5_c_re, s5_c_im, s5_d,
           od_w_o, od_w_g, peer_w_q, peer_keys, peer_u, peer_v):
    cos, sin = axial_rope(x.shape[1])
    act_lat = jax.nn.silu(c)
    act_ctx = jax.nn.silu(c_ctx)
    h_lat, h_ctx = x, ctx
    for layer in range(DEPTH):
        need_ctx = layer < DEPTH - 1
        m_lat = jnp.split((act_lat @ mod_w[layer] + mod_b[layer])[:, None, :], 6, axis=-1)
        m_ctx = jnp.split((act_ctx @ mod_w[layer] + mod_b[layer])[None, None, :], 6, axis=-1)
        in_lat = h_lat * (1.0 + m_lat[1]) + m_lat[0]
        in_ctx = h_ctx * (1.0 + m_ctx[1]) + m_ctx[0]
        i = layer // 2
        if layer % 2 == 0:
            hy = (ev_conv_w[i], ev_conv_b[i], hy_w1[i], hy_b1[i], hy_w2[i], hy_b2[i], hy_w3[i], hy_b3[i],
                  hy_w4[i], hy_freq[i], hy_bias[i])
            mla = (mla_q_norm[i], mla_w_uq[i], mla_kv_norm[i], mla_w_ukv[i])
            y_ctx, y_lat = even_mixer(in_ctx, in_lat, cos, sin, need_ctx, ev_w_in[i], hy, mla, ev_w_o[i])
        else:
            y_ctx, y_lat = s5_mixer(in_ctx, in_lat, need_ctx, s5_lam_re[i], s5_lam_im[i], s5_log_dt[i],
                                    s5_b_re[i], s5_b_im[i], s5_c_re[i], s5_c_im[i], s5_d[i],
                                    od_w_o[i], od_w_g[i])
        h_lat = layer_norm(DN_ALPHA * h_lat + m_lat[2] * y_lat, ln_mix_g[layer], ln_mix_b[layer])
        bn, ll, dm = h_lat.shape
        xin = (h_lat * (1.0 + m_lat[4]) + m_lat[3]).reshape(bn * ll, dm)
        if need_ctx:
            h_ctx = layer_norm(DN_ALPHA * h_ctx + m_ctx[2] * y_ctx, ln_mix_g[layer], ln_mix_b[layer])
            xin_ctx = (h_ctx * (1.0 + m_ctx[4]) + m_ctx[3]).reshape(-1, dm)
            xin = jnp.concatenate([xin, xin_ctx], axis=0)
        f_all = peer_ffn_pallas(xin, peer_w_q[layer], peer_keys[layer],
                                peer_u[layer].astype(BF16), peer_v[layer].T.astype(BF16))
        f_lat = f_all[:bn * ll].reshape(bn, ll, dm)
        pre = DN_ALPHA * h_lat + m_lat[5] * f_lat
        if layer == DEPTH - 1:
            h_lat = layer_norm_pallas(pre, ln_ffn_g[layer], ln_ffn_b[layer])
        else:
            h_lat = layer_norm(pre, ln_ffn_g[layer], ln_ffn_b[layer])
        if need_ctx:
            f_ctx = f_all[bn * ll:].reshape(h_ctx.shape)
            h_ctx = layer_norm(DN_ALPHA * h_ctx + m_ctx[5] * f_ctx, ln_ffn_g[layer], ln_ffn_b[layer])
    return h_lat
```

```python
import math
import functools
import jax
import jax.numpy as jnp
from jax import lax
from jax.experimental import pallas as pl
from jax.experimental.pallas import tpu as pltpu

D_MODEL = 2048
BATCH = 2
SEQ = 8192
DEPTH = 4

GRID_W = 64
CTX_LEN = 256

HY_W = 1024
HY_ORDER = 2
HY_EMB = 33
HY_BANDS = (HY_EMB - 1) // 2
HY_HID = 64
HY_FAST_DECAY = 0.3
HY_SLOW_DECAY = 1.5
HY_TARGET = 1e-2

MLA_HEADS = 8
MLA_NOPE = 128
MLA_ROPE = 64
MLA_V = 128
MLA_Q_RANK = 512
MLA_KV_RANK = 256
MLA_SCALE = (MLA_NOPE + MLA_ROPE) ** -0.5
Q_BLOCK = 128
ROPE_BASE = 10000.0

O_HY = 3 * HY_W
O_Q = O_HY + MLA_Q_RANK
O_KV = O_Q + MLA_KV_RANK
IN_W = O_KV + MLA_ROPE

S5_GC = 16
S5_NG = D_MODEL // S5_GC
S5_P = 64
S5_GB = 16
S5_NBLK = S5_NG // S5_GB

PEER_HEADS = 8
PEER_NK = 128
PEER_N = PEER_NK * PEER_NK
PEER_DK = 256
PEER_TOPK = 16
PEER_BLOCK = 128

N_EVEN = (DEPTH + 1) // 2
N_ODD = DEPTH // 2
DN_ALPHA = (2.0 * DEPTH) ** 0.25
DN_BETA = (8.0 * DEPTH) ** -0.25
LN_EPS = 1e-5
RMS_EPS = 1e-6
F32 = jnp.float32


def _ln_kernel(x_ref, g_ref, b_ref, o_ref):
    xf = x_ref[...]
    xc = xf - jnp.mean(xf, axis=-1, keepdims=True)
    var = jnp.mean(xc * xc, axis=-1, keepdims=True)
    o_ref[...] = xc * lax.rsqrt(var + LN_EPS) * g_ref[...] + b_ref[...]


def layer_norm_pallas(x, g, b):
    shp = x.shape
    x2 = x.reshape(-1, shp[-1])
    n, d = x2.shape
    tb = 256
    out = pl.pallas_call(
        _ln_kernel,
        grid=(n // tb,),
        in_specs=[pl.BlockSpec((tb, d), lambda i: (i, 0)),
                  pl.BlockSpec((1, d), lambda i: (0, 0)),
                  pl.BlockSpec((1, d), lambda i: (0, 0))],
        out_specs=pl.BlockSpec((tb, d), lambda i: (i, 0)),
        out_shape=jax.ShapeDtypeStruct((n, d), F32),
        name="layer_norm",
    )(x2, g.reshape(1, d), b.reshape(1, d))
    return out.reshape(shp)


BF16 = jnp.bfloat16
HI = lax.Precision.HIGHEST
VMEM_LIMIT = 56 * 1024 * 1024

S5_T = 16
S5_CB = 8


def _s5_operators(lam_re, lam_im, log_dt, b_re, b_im, c_re, c_im):
    T = S5_T
    ng, p, gc = b_re.shape[1:]
    dt = jnp.exp(log_dt)[..., None]
    zr, zi = lam_re * dt, lam_im * dt
    k = jnp.arange(T + 1, dtype=F32)[:, None, None, None]
    mag = jnp.exp(zr[None] * k)
    pr, pi = mag * jnp.cos(zi[None] * k), mag * jnp.sin(zi[None] * k)
    ar, ai = pr[1], pi[1]
    den = lam_re * lam_re + lam_im * lam_im
    cr = ((ar - 1.0) * lam_re + ai * lam_im) / den
    ci = (ai * lam_re - (ar - 1.0) * lam_im) / den
    bbr = cr[..., None] * b_re - ci[..., None] * b_im
    bbi = cr[..., None] * b_im + ci[..., None] * b_re
    car = c_re[None] * pr[:, :, :, None, :] - c_im[None] * pi[:, :, :, None, :]
    cai = c_re[None] * pi[:, :, :, None, :] + c_im[None] * pr[:, :, :, None, :]
    kk = (jnp.einsum('tdgcp,dgpe->tdgce', car[:T], bbr, precision=HI)
          - jnp.einsum('tdgcp,dgpe->tdgce', cai[:T], bbi, precision=HI))
    kf, kb = kk[:, 0], kk[:, 1]
    kfull = jnp.concatenate([kb[1:][::-1], (kf[0] + kb[0])[None], kf[1:]], axis=0)
    jj = jnp.arange(T)
    idx = jj[None, :] - jj[:, None] + (T - 1)
    m = kfull[idx]
    m = m.transpose(2, 0, 4, 1, 3).reshape(ng, T * gc, T * gc)
    def e_w(d, pw_idx):
        er = pr[pw_idx, d][..., None] * bbr[d][None] - pi[pw_idx, d][..., None] * bbi[d][None]
        ei = pr[pw_idx, d][..., None] * bbi[d][None] + pi[pw_idx, d][..., None] * bbr[d][None]
        w = jnp.concatenate([er, ei], axis=2)
        return w.transpose(1, 0, 3, 2).reshape(ng, T * gc, 2 * p)
    w_e = jnp.concatenate([e_w(0, T - 1 - jj), e_w(1, jj)], axis=-1)
    def s_w(d, pw_idx):
        w = jnp.concatenate([car[pw_idx, d], -cai[pw_idx, d]], axis=-1)
        return w.transpose(1, 3, 0, 2).reshape(ng, 2 * p, T * gc)
    w_y = jnp.concatenate([m, s_w(0, jj + 1), s_w(1, T - jj)], axis=1)
    a1 = jnp.concatenate([pr[T], pr[T]], axis=-1)
    a2 = jnp.concatenate([-pi[T], pi[T]], axis=-1)
    return w_e, w_y, a1, a2


def _s5_end_state_kernel(x_ref, w_ref, ef_ref, eb_ref):
    e = jnp.dot(x_ref[0], w_ref[0], precision=HI, preferred_element_type=F32)
    half = e.shape[1] // 2
    ef_ref[...] = e[:, :half]
    eb_ref[...] = e[:, half:]


def _s5_carry_kernel(ef_ref, eb_ref, a1_ref, a2_ref, s0f_ref, s0b_ref, sf_scr, sb_scr):
    @pl.when(pl.program_id(1) == 0)
    def _():
        sf_scr[...] = jnp.zeros_like(sf_scr)
        sb_scr[...] = jnp.zeros_like(sb_scr)

    half = sf_scr.shape[-1] // 2

    def step(s, e, d):
        return a1_ref[d] * s + a2_ref[d] * pltpu.roll(s, half, 1) + e

    for r in range(S5_CB):
        s = sf_scr[...]
        s0f_ref[r] = s
        sf_scr[...] = step(s, ef_ref[r], 0)
    for r in reversed(range(S5_CB)):
        s = sb_scr[...]
        s0b_ref[r] = s
        sb_scr[...] = step(s, eb_ref[r], 1)


def _s5_output_kernel(x_ref, s0f_ref, s0b_ref, w_ref, y_ref):
    kx = x_ref.shape[-1]
    ks = s0f_ref.shape[-1]
    w = w_ref[0]
    y = jnp.dot(x_ref[0], w[:kx], precision=HI, preferred_element_type=F32)
    y = y + jnp.dot(s0f_ref[...], w[kx:kx + ks], precision=HI, preferred_element_type=F32)
    y = y + jnp.dot(s0b_ref[...], w[kx + ks:], precision=HI, preferred_element_type=F32)
    y_ref[0] = y


def s5_scan(in_ctx, in_lat, lam_re, lam_im, log_dt, b_re, b_im, c_re, c_im):
    T = S5_T
    bn, lc, d = in_ctx.shape
    ll = in_lat.shape[1]
    ng, p, gc = b_re.shape[1:]
    w_e, w_y, a1, a2 = _s5_operators(lam_re, lam_im, log_dt, b_re, b_im, c_re, c_im)
    u = jnp.concatenate([in_ctx, in_lat], axis=1)
    nch = (lc + ll) // T
    rows = bn * nch
    kx, ks = T * gc, 2 * p
    xg = u.reshape(bn, nch, T, ng, gc).transpose(3, 0, 1, 2, 4).reshape(ng, rows, kx)

    ef, eb = pl.pallas_call(
        _s5_end_state_kernel,
        grid=(ng,),
        in_specs=[pl.BlockSpec((1, rows, kx), lambda g: (g, 0, 0)),
                  pl.BlockSpec((1, kx, 2 * ks), lambda g: (g, 0, 0))],
        out_specs=[pl.BlockSpec((rows, ks), lambda g: (0, g)),
                   pl.BlockSpec((rows, ks), lambda g: (0, g))],
        out_shape=[jax.ShapeDtypeStruct((rows, ng * ks), F32)] * 2,
        name="s5_end_state",
    )(xg, w_e)

    nblk = nch // S5_CB
    cblk = (lc // T) // S5_CB
    fwd_map = lambda b, j: (b * nblk + j, 0, 0)
    bwd_map = lambda b, j: (b * nblk + jnp.where(j < cblk, cblk - 1 - j, nblk + cblk - 1 - j), 0, 0)
    s0f, s0b = pl.pallas_call(
        _s5_carry_kernel,
        grid=(bn, nblk),
        in_specs=[pl.BlockSpec((S5_CB, ng, ks), fwd_map),
                  pl.BlockSpec((S5_CB, ng, ks), bwd_map),
                  pl.BlockSpec((2, ng, ks), lambda b, j: (0, 0, 0)),
                  pl.BlockSpec((2, ng, ks), lambda b, j: (0, 0, 0))],
        out_specs=[pl.BlockSpec((S5_CB, ng, ks), fwd_map),
                   pl.BlockSpec((S5_CB, ng, ks), bwd_map)],
        out_shape=[jax.ShapeDtypeStruct((rows, ng, ks), F32)] * 2,
        scratch_shapes=[pltpu.VMEM((ng, ks), F32), pltpu.VMEM((ng, ks), F32)],
        name="s5_carry",
    )(ef.reshape(rows, ng, ks), eb.reshape(rows, ng, ks), a1, a2)

    yg = pl.pallas_call(
        _s5_output_kernel,
        grid=(ng,),
        in_specs=[pl.BlockSpec((1, rows, kx), lambda g: (g, 0, 0)),
                  pl.BlockSpec((rows, ks), lambda g: (0, g)),
                  pl.BlockSpec((rows, ks), lambda g: (0, g)),
                  pl.BlockSpec((1, kx + 2 * ks, kx), lambda g: (g, 0, 0))],
        out_specs=pl.BlockSpec((1, rows, kx), lambda g: (g, 0, 0)),
        out_shape=jax.ShapeDtypeStruct((ng, rows, kx), F32),
        name="s5_output",
    )(xg, s0f.reshape(rows, ng * ks), s0b.reshape(rows, ng * ks), w_y)
    y = yg.reshape(ng, bn, nch, T, gc).transpose(1, 2, 3, 0, 4).reshape(bn, lc + ll, d)
    return y[:, :lc], y[:, lc:]


PEER_TS = 256
PEER_TB = 512
PEER_EB = 1024
LANES = 128
NEG = -3.0e38
_NN = (((1,), (0,)), ((), ()))
_NT = (((1,), (1,)), ((), ()))


def _split_bf16(a):
    hi = a.astype(BF16)
    return hi, (a - hi.astype(F32)).astype(BF16)


def _dot3(ah, al, bh, bl, dims):
    f = lambda a, b: lax.dot_general(a, b, dims, preferred_element_type=F32)
    return f(ah, bh) + f(al, bh) + f(ah, bl)


def _peer_cells():
    return PEER_TOPK + 8 * (PEER_TOPK // 2 - 1) + 8


def _peer_score_kernel(x_ref, wh_ref, wl_ref, kh_ref, kl_ref,
                       s1_ref, w1_ref, s2_ref, e2_ref, tau_ref,
                       st_scr, a1_scr, a2_scr, c_scr):
    nh, _, nk, dk = kh_ref.shape
    tb = x_ref.shape[0]
    xh, xl = _split_bf16(x_ref[...])
    q = _dot3(xh, xl, wh_ref[...], wl_ref[...], _NN)
    qh, ql = _split_bf16(q)
    for h in range(nh):
        for s in range(2):
            c0 = (2 * h + s) * dk
            st_scr[2 * h + s] = _dot3(kh_ref[h, s], kl_ref[h, s],
                                      qh[:, c0:c0 + dk], ql[:, c0:c0 + dk], _NT)

    def top_values(s, out_scr):
        for r in range(PEER_TOPK):
            m = jnp.max(s, axis=0, keepdims=True)
            out_scr[r:r + 1, :] = m
            s = jnp.where(s == m, NEG, s)

    nsub = tb // LANES
    half = PEER_TOPK // 2

    def body(it, carry):
        h = it // nsub
        lanes = pl.ds(pl.multiple_of((it % nsub) * LANES, LANES), LANES)
        s1 = st_scr[2 * h, :, lanes]
        s2 = st_scr[2 * h + 1, :, lanes]
        top_values(s1, a1_scr)
        top_values(s2, a2_scr)
        a1 = a1_scr[...]
        a2 = a2_scr[...]
        c_scr[0:PEER_TOPK, :] = a1[0:1] + a2
        for k in range(1, half):
            c_scr[PEER_TOPK + 8 * (k - 1):PEER_TOPK + 8 * k, :] = a1[k:k + 1] + a2[0:8]
        c_scr[PEER_TOPK + 8 * (half - 1):, :] = a1[half:] + a2[0:1]
        c = c_scr[...]
        m0 = a1[0:1] + a2[0:1]
        z = jnp.zeros_like(m0)
        m = m0
        for r in range(PEER_TOPK):
            m = jnp.max(c, axis=0, keepdims=True)
            z = z + jnp.exp(m - m0)
            c = jnp.where(c == m, NEG, c)
        s1_ref[h, :, lanes] = s1
        w1_ref[h, :, lanes] = jnp.exp(s1 - a1[0:1]) / z
        s2_ref[h, :, lanes] = s2
        e2_ref[h, :, lanes] = jnp.exp(s2 - a2[0:1])
        tau_ref[h, :, lanes] = jnp.broadcast_to(m, (8, LANES))
        return carry

    lax.fori_loop(0, nh * nsub, body, 0)


def _peer_expert_kernel(xt_ref, u_ref, vt_ref, s1_ref, w1_ref, s2_ref, e2_ref, tau_ref,
                        o_ref, acc_scr, a_scr, p_scr):
    e = pl.program_id(1)
    last = pl.num_programs(1) - 1
    slot = e % 2
    nh, nk, tb = s2_ref.shape
    ni = a_scr.shape[0] // nk

    @pl.when(e == 0)
    def _():
        acc_scr[...] = jnp.zeros_like(acc_scr)
        p_scr[1] = jnp.zeros(p_scr.shape[1:], BF16)

    @pl.when(e < last)
    def _():
        acc_scr[...] += jnp.dot(vt_ref[...], p_scr[1 - slot], preferred_element_type=F32)
        a_scr[...] = jnp.dot(u_ref[...], xt_ref[...], preferred_element_type=F32)
        for ts in range(tb // LANES):
            lanes = slice(ts * LANES, (ts + 1) * LANES)
            for ii in range(ni):
                rows = slice(ii * nk, (ii + 1) * nk)
                g = jnp.zeros((nk, LANES), F32)
                for h in range(nh):
                    pair = s2_ref[h, :, lanes] + s1_ref[h, ii:ii + 1, lanes]
                    keep = pair >= tau_ref[h, 0:1, lanes]
                    g = g + jnp.where(keep, e2_ref[h, :, lanes], 0.0) * w1_ref[h, ii:ii + 1, lanes]
                p_scr[slot, rows, lanes] = (g * jax.nn.gelu(a_scr[rows, lanes])).astype(BF16)

    @pl.when(e == last)
    def _():
        acc = acc_scr[...] + jnp.dot(vt_ref[...], p_scr[1 - slot], preferred_element_type=F32)
        o_ref[...] = acc.T


def peer_ffn_pallas(xin, w_q, keys, u_bf, vt_bf):
    n, d = xin.shape
    nh, _, nk, dk = keys.shape
    ne = u_bf.shape[0]
    ts = min(PEER_TS, n)
    tb = min(PEER_TB, n)
    eb = min(PEER_EB, ne)
    wh, wl = _split_bf16(w_q)
    kh, kl = _split_bf16(keys)
    feat = jax.ShapeDtypeStruct((nh, nk, n), F32)
    feat_spec = pl.BlockSpec((nh, nk, ts), lambda i: (0, 0, i))
    s1, w1, s2, e2, tau = pl.pallas_call(
        _peer_score_kernel,
        grid=(n // ts,),
        in_specs=[pl.BlockSpec((ts, d), lambda i: (i, 0)),
                  pl.BlockSpec(wh.shape, lambda i: (0, 0)),
                  pl.BlockSpec(wl.shape, lambda i: (0, 0)),
                  pl.BlockSpec(kh.shape, lambda i: (0, 0, 0, 0)),
                  pl.BlockSpec(kl.shape, lambda i: (0, 0, 0, 0))],
        out_specs=[feat_spec, feat_spec, feat_spec, feat_spec,
                   pl.BlockSpec((nh, 8, ts), lambda i: (0, 0, i))],
        out_shape=[feat, feat, feat, feat, jax.ShapeDtypeStruct((nh, 8, n), F32)],
        scratch_shapes=[pltpu.VMEM((2 * nh, nk, ts), F32),
                        pltpu.VMEM((PEER_TOPK, LANES), F32),
                        pltpu.VMEM((PEER_TOPK, LANES), F32),
                        pltpu.VMEM((_peer_cells(), LANES), F32)],
        compiler_params=pltpu.CompilerParams(vmem_limit_bytes=VMEM_LIMIT),
        name="peer_scores",
    )(xin, wh, wl, kh, kl)

    ni = eb // nk
    nblk = ne // eb
    row_spec = pl.BlockSpec((nh, ni, tb), lambda t, e: (0, jnp.minimum(e, nblk - 1), t))
    col_spec = pl.BlockSpec((nh, nk, tb), lambda t, e: (0, 0, t))
    return pl.pallas_call(
        _peer_expert_kernel,
        grid=(n // tb, nblk + 1),
        in_specs=[pl.BlockSpec((d, tb), lambda t, e: (0, t)),
                  pl.BlockSpec((eb, d), lambda t, e: (jnp.minimum(e, nblk - 1), 0)),
                  pl.BlockSpec((d, eb), lambda t, e: (0, jnp.maximum(e - 1, 0))),
                  row_spec, row_spec, col_spec, col_spec,
                  pl.BlockSpec((nh, 8, tb), lambda t, e: (0, 0, t))],
        out_specs=pl.BlockSpec((tb, d), lambda t, e: (t, 0)),
        out_shape=jax.ShapeDtypeStruct((n, d), F32),
        scratch_shapes=[pltpu.VMEM((d, tb), F32),
                        pltpu.VMEM((eb, tb), F32),
                        pltpu.VMEM((2, eb, tb), BF16)],
        compiler_params=pltpu.CompilerParams(vmem_limit_bytes=VMEM_LIMIT,
                                             dimension_semantics=("arbitrary", "arbitrary")),
        name="peer_experts",
    )(xin.astype(BF16).T, u_bf, vt_bf, s1, w1, s2, e2, tau)


MM_TM = 512
MM_TN = 512


def _mm_kernel(x_ref, w_ref, o_ref):
    o_ref[...] = jnp.dot(x_ref[...].astype(BF16), w_ref[...], preferred_element_type=F32)


def _glu_kernel(x_ref, wo_ref, wg_ref, o_ref):
    xb = x_ref[...].astype(BF16)
    a = jnp.dot(xb, wo_ref[...], preferred_element_type=F32)
    g = jnp.dot(xb, wg_ref[...], preferred_element_type=F32)
    o_ref[...] = a * jax.nn.sigmoid(g)


def _mm_call(body, x, ws, name):
    m, k = x.shape
    n = ws[0].shape[1]
    tm = min(MM_TM, m)
    tn = min(MM_TN, n)
    return pl.pallas_call(
        body,
        grid=(m // tm, n // tn),
        in_specs=[pl.BlockSpec((tm, k), lambda i, j: (i, 0))]
        + [pl.BlockSpec((k, tn), lambda i, j: (0, j)) for _ in ws],
        out_specs=pl.BlockSpec((tm, tn), lambda i, j: (i, j)),
        out_shape=jax.ShapeDtypeStruct((m, n), F32),
        compiler_params=pltpu.CompilerParams(vmem_limit_bytes=VMEM_LIMIT),
        name=name,
    )(x, *ws)


def matmul_pallas(x, w):
    n = w.shape[1]
    npad = -n % MM_TN if n > MM_TN else -n % LANES
    wb = jnp.pad(w.astype(BF16), ((0, 0), (0, npad)))
    out = _mm_call(_mm_kernel, x, [wb], "dense_proj")
    return out[:, :n] if npad else out


def glu_pallas(x, w_o, w_g):
    return _mm_call(_glu_kernel, x, [w_o.astype(BF16), w_g.astype(BF16)], "glu_proj")


ATT_TQ = 256
ATT_TK = 768


def _attn_kernel(q_ref, kt_ref, v_ref, o_ref, *, tk):
    q = q_ref[0, 0]
    tq = q.shape[0]
    nk = kt_ref.shape[-1]
    m = jnp.full((tq, 1), NEG, F32)
    l = jnp.zeros((tq, 1), F32)
    acc = jnp.zeros((tq, v_ref.shape[-1]), F32)
    for c in range(nk // tk):
        s = jnp.dot(q, kt_ref[0, 0, :, c * tk:(c + 1) * tk], preferred_element_type=F32)
        m_new = jnp.maximum(m, jnp.max(s, axis=-1, keepdims=True))
        alpha = jnp.exp(m - m_new)
        p = jnp.exp(s - m_new)
        l = alpha * l + jnp.sum(p, axis=-1, keepdims=True)
        acc = alpha * acc + jnp.dot(p.astype(BF16), v_ref[0, 0, c * tk:(c + 1) * tk, :],
                                    preferred_element_type=F32)
        m = m_new
    o_ref[0] = acc / l


def mla_attention_pallas(qn, qr, kn, kr, v):
    bn, lq, nh, _ = qn.shape
    sk = kn.shape[1]
    dv = v.shape[-1]
    q = (jnp.concatenate([qn, qr], axis=-1) * MLA_SCALE).astype(BF16).transpose(0, 2, 1, 3)
    kcat = jnp.concatenate([kn, jnp.broadcast_to(kr[:, :, None, :], (bn, sk, nh, kr.shape[-1]))], axis=-1)
    kt = kcat.astype(BF16).transpose(0, 2, 3, 1)
    vv = v.astype(BF16).transpose(0, 2, 1, 3)
    dq = q.shape[-1]
    tq = min(ATT_TQ, lq)
    tk = ATT_TK if sk % ATT_TK == 0 else sk
    return pl.pallas_call(
        functools.partial(_attn_kernel, tk=tk),
        grid=(bn, nh, lq // tq),
        in_specs=[pl.BlockSpec((1, 1, tq, dq), lambda b, h, i: (b, h, i, 0)),
                  pl.BlockSpec((1, 1, dq, sk), lambda b, h, i: (b, h, 0, 0)),
                  pl.BlockSpec((1, 1, sk, dv), lambda b, h, i: (b, h, 0, 0))],
        out_specs=pl.BlockSpec((1, tq, dv), lambda b, h, i: (b, i, h)),
        out_shape=jax.ShapeDtypeStruct((bn, lq, nh * dv), F32),
        compiler_params=pltpu.CompilerParams(vmem_limit_bytes=VMEM_LIMIT),
        name="mla_attention",
    )(q, kt, vv)


def layer_norm(x, g, b):
    xf = x.astype(F32)
    xc = xf - jnp.mean(xf, axis=-1, keepdims=True)
    var = jnp.mean(xc * xc, axis=-1, keepdims=True)
    return (xc * lax.rsqrt(var + LN_EPS) * g + b).astype(x.dtype)


def rms_norm(x, g):
    xf = x.astype(F32)
    return (xf * lax.rsqrt(jnp.mean(xf * xf, axis=-1, keepdims=True) + RMS_EPS) * g).astype(x.dtype)


def axial_rope(L):
    rows = L // GRID_W
    row = jnp.repeat(jnp.arange(rows), GRID_W).astype(F32)
    col = jnp.tile(jnp.arange(GRID_W), rows).astype(F32)
    half = MLA_ROPE // 2
    inv = ROPE_BASE ** (-jnp.arange(0, half, 2, dtype=F32) / half)
    ar, ac = row[:, None] * inv, col[:, None] * inv
    ang = jnp.concatenate([ar, ar, ac, ac], axis=-1)
    return jnp.cos(ang), jnp.sin(ang)


def apply_axial_rope(x, cos, sin):
    xs = x.reshape(x.shape[:-1] + (2, 2, MLA_ROPE // 4))
    rot = jnp.concatenate([-xs[..., 1:, :], xs[..., :1, :]], axis=-2).reshape(x.shape)
    return (x * cos + rot * sin).astype(x.dtype)


def short_conv3(z, w, b):
    L = z.shape[1]
    zp = jnp.pad(z, ((0, 0), (1, 1), (0, 0)))
    return zp[:, :L] * w[0] + zp[:, 1:L + 1] * w[1] + zp[:, 2:] * w[2] + b


def hyena_pos_features(L):
    t = jnp.linspace(0.0, 1.0, L, dtype=F32)[:, None]
    w = (2.0 * math.pi / L) * jnp.arange(L, dtype=F32)[:, None]
    f = jnp.linspace(1e-4, HY_BANDS - 1, HY_BANDS, dtype=F32)[None, :]
    return jnp.concatenate([t, jnp.cos(f * w), -jnp.sin(f * w)], axis=-1)


def hyena_filter_spectrum(L, w1, b1, w2, b2, w3, b3, w4, freq):
    act = lambda a: jnp.sin(freq * a)
    hdn = act(hyena_pos_features(L) @ w1 + b1)
    hdn = act(hdn @ w2 + b2)
    hdn = act(hdn @ w3 + b3)
    h = (hdn @ w4).astype(F32).reshape(L, 2, HY_ORDER, HY_W)
    deltas = jnp.abs(jnp.linspace(math.log(HY_TARGET) / HY_SLOW_DECAY, math.log(HY_TARGET) / HY_FAST_DECAY,
                                  HY_ORDER * HY_W, dtype=F32)).reshape(HY_ORDER, HY_W)
    t = jnp.linspace(0.0, 1.0, L, dtype=F32)[:, None, None, None]
    h = h * jnp.exp(-t * deltas)
    h = h / jnp.sum(jnp.abs(h), axis=(0, 1), keepdims=True)
    k = jnp.concatenate([h[:, 0], jnp.zeros((1, HY_ORDER, HY_W), F32), h[1:, 1][::-1]], axis=0)
    return jnp.fft.rfft(k, axis=0)


def fft_long_conv(u, k_f):
    L = u.shape[1]
    u_f = jnp.fft.rfft(u.astype(F32), n=2 * L, axis=1)
    return jnp.fft.irfft(u_f * k_f[None], n=2 * L, axis=1)[:, :L]


def hyena_mixer(z, conv_w, conv_b, w1, b1, w2, b2, w3, b3, w4, freq, bias):
    L = z.shape[1]
    zc = short_conv3(z, conv_w, conv_b)
    x1, x2, v = jnp.split(zc, 3, axis=-1)
    k_f = hyena_filter_spectrum(L, w1, b1, w2, b2, w3, b3, w4, freq)
    y = v.astype(F32)
    for o, gate in enumerate((x1, x2)):
        y = gate * (fft_long_conv(y, k_f[:, o]) + y * bias[o])
    return y.astype(z.dtype)


def _mm3(x, w):
    b, l, k = x.shape
    return matmul_pallas(x.reshape(b * l, k), w).reshape(b, l, w.shape[1])


def mla_heads(z, q_norm, w_uq, kv_norm, w_ukv):
    Bn, L, _ = z.shape
    q = _mm3(rms_norm(z[..., O_HY:O_Q], q_norm), w_uq).reshape(Bn, L, MLA_HEADS, MLA_NOPE + MLA_ROPE)
    kv = _mm3(rms_norm(z[..., O_Q:O_KV], kv_norm), w_ukv).reshape(Bn, L, MLA_HEADS, MLA_NOPE + MLA_V)
    return q[..., :MLA_NOPE], q[..., MLA_NOPE:], kv[..., :MLA_NOPE], z[..., O_KV:], kv[..., MLA_NOPE:]


def even_mixer(in_ctx, in_lat, cos, sin, need_ctx, w_in, hy, mla, w_o):
    z_ctx = _mm3(in_ctx, w_in)
    z_lat = _mm3(in_lat, w_in)
    cqn, cqr, ckn, ckr, cv = mla_heads(z_ctx, *mla)
    lqn, lqr, lkn, lkr, lv = mla_heads(z_lat, *mla)
    lqr = apply_axial_rope(lqr, cos[:, None, :], sin[:, None, :])
    lkr = apply_axial_rope(lkr, cos, sin)
    kn = jnp.concatenate([ckn, lkn], axis=1)
    kr = jnp.concatenate([ckr, lkr], axis=1)
    vv = jnp.concatenate([cv, lv], axis=1)
    att = mla_attention_pallas(lqn, lqr, kn, kr, vv)
    y_lat = _mm3(jnp.concatenate([hyena_mixer(z_lat[..., :O_HY], *hy), att], axis=-1), w_o)
    y_ctx = None
    if need_ctx:
        att_c = mla_attention_pallas(cqn, cqr, ckn, ckr, cv)
        y_ctx = _mm3(jnp.concatenate([hyena_mixer(z_ctx[..., :O_HY], *hy), att_c], axis=-1), w_o)
    return y_ctx, y_lat


def _scan_op(e1, e2):
    a1, b1 = e1
    a2, b2 = e2
    return a1 * a2, a2 * b1 + b2


def linear_scan(a_bar, bu, s0):
    bu = bu.at[:, 0].add(a_bar * s0)
    a = jnp.broadcast_to(a_bar, bu.shape)
    _, s = lax.associative_scan(_scan_op, (a, bu), axis=1)
    return s


def _rev(s, d):
    return s[:, ::-1] if d == 1 else s


def s5_mixer(in_ctx, in_lat, need_ctx, lam_re, lam_im, log_dt, b_re, b_im, c_re, c_im, d_skip, w_o, w_g):
    s_ctx, s_lat = s5_scan(in_ctx, in_lat, lam_re, lam_im, log_dt, b_re, b_im, c_re, c_im)

    def finish(ys, h):
        y = ys + d_skip * h.astype(F32)
        y = jax.nn.gelu(y).astype(h.dtype)
        b, l, dm = y.shape
        return glu_pallas(y.reshape(b * l, dm), w_o, w_g).reshape(b, l, dm)

    y_lat = finish(s_lat, in_lat)
    y_ctx = finish(s_ctx, in_ctx) if need_ctx else None
    return y_ctx, y_lat


def peer_ffn(h, w_q, keys, u_tab, v_tab):
    Bn, L, D = h.shape
    hb = h.reshape(Bn * L // PEER_BLOCK, PEER_BLOCK, D)

    def block_fn(xb):
        q = (xb @ w_q).reshape(PEER_BLOCK, PEER_HEADS, 2, PEER_DK // 2)
        s = jnp.einsum('thsk,hsnk->thsn', q, keys).astype(F32)
        top_s, top_i = lax.top_k(s, PEER_TOPK)
        cand = top_s[:, :, 0, :, None] + top_s[:, :, 1, None, :]
        cand_s, cand_j = lax.top_k(cand.reshape(PEER_BLOCK, PEER_HEADS, PEER_TOPK * PEER_TOPK), PEER_TOPK)
        idx = (jnp.take_along_axis(top_i[:, :, 0], cand_j // PEER_TOPK, axis=-1) * PEER_NK
               + jnp.take_along_axis(top_i[:, :, 1], cand_j % PEER_TOPK, axis=-1))
        g = jax.nn.softmax(cand_s, axis=-1)
        act = jax.nn.gelu(jnp.einsum('thkd,td->thk', u_tab[idx], xb).astype(F32))
        return jnp.einsum('thk,thkd->td', (g * act).astype(xb.dtype), v_tab[idx])

    return lax.map(block_fn, hb).reshape(Bn, L, D)


def kernel(x, c, ctx, c_ctx, mod_w, mod_b, ln_mix_g, ln_mix_b, ln_ffn_g, ln_ffn_b,
           ev_w_in, ev_conv_w, ev_conv_b, hy_w1, hy_b1, hy_w2, hy_b2, hy_w3, hy_b3, hy_w4,
           hy_freq, hy_bias, mla_q_norm, mla_w_uq, mla_kv_norm, mla_w_ukv, ev_w_o,
           s5_lam_re, s5_lam_im, s5_log_dt, s5_b_re, s5_b_im, s5_c_re, s5_c_im, s5_d,
           od_w_o, od_w_g, peer_w_q, peer_keys, peer_u, peer_v):
    cos, sin = axial_rope(x.shape[1])
    act_lat = jax.nn.silu(c)
    act_ctx = jax.nn.silu(c_ctx)
    h_lat, h_ctx = x, ctx
    for layer in range(DEPTH):
        need_ctx = layer < DEPTH - 1
        m_lat = jnp.split((act_lat @ mod_w[layer] + mod_b[layer])[:, None, :], 6, axis=-1)
        m_ctx = jnp.split((act_ctx @ mod_w[layer] + mod_b[layer])[None, None, :], 6, axis=-1)
        in_lat = h_lat * (1.0 + m_lat[1]) + m_lat[0]
        in_ctx = h_ctx * (1.0 + m_ctx[1]) + m_ctx[0]
        i = layer // 2
        if layer % 2 == 0:
            hy = (ev_conv_w[i], ev_conv_b[i], hy_w1[i], hy_b1[i], hy_w2[i], hy_b2[i], hy_w3[i], hy_b3[i],
                  hy_w4[i], hy_freq[i], hy_bias[i])
            mla = (mla_q_norm[i], mla_w_uq[i], mla_kv_norm[i], mla_w_ukv[i])
            y_ctx, y_lat = even_mixer(in_ctx, in_lat, cos, sin, need_ctx, ev_w_in[i], hy, mla, ev_w_o[i])
        else:
            y_ctx, y_lat = s5_mixer(in_ctx, in_lat, need_ctx, s5_lam_re[i], s5_lam_im[i], s5_log_dt[i],
                                    s5_b_re[i], s5_b_im[i], s5_c_re[i], s5_c_im[i], s5_d[i],
                                    od_w_o[i], od_w_g[i])
        h_lat = layer_norm(DN_ALPHA * h_lat + m_lat[2] * y_lat, ln_mix_g[layer], ln_mix_b[layer])
        bn, ll, dm = h_lat.shape
        xin = (h_lat * (1.0 + m_lat[4]) + m_lat[3]).reshape(bn * ll, dm)
        if need_ctx:
            h_ctx = layer_norm(DN_ALPHA * h_ctx + m_ctx[2] * y_ctx, ln_mix_g[layer], ln_mix_b[layer])
            xin_ctx = (h_ctx * (1.0 + m_ctx[4]) + m_ctx[3]).reshape(-1, dm)
            xin = jnp.concatenate([xin, xin_ctx], axis=0)
        f_all = peer_ffn_pallas(xin, peer_w_q[layer], peer_keys[layer],
                                peer_u[layer].astype(BF16), peer_v[layer].T.astype(BF16))
        f_lat = f_all[:bn * ll].reshape(bn, ll, dm)
        pre = DN_ALPHA * h_lat + m_lat[5] * f_lat
        if layer == DEPTH - 1:
            h_lat = layer_norm_pallas(pre, ln_ffn_g[layer], ln_ffn_b[layer])
        else:
            h_lat = layer_norm(pre, ln_ffn_g[layer], ln_ffn_b[layer])
        if need_ctx:
            f_ctx = f_all[bn * ll:].reshape(h_ctx.shape)
            h_ctx = layer_norm(DN_ALPHA * h_ctx + m_ctx[5] * f_ctx, ln_ffn_g[layer], ln_ffn_b[layer])
    return h_lat
```

```python
import math
import functools
import numpy as np
import jax
import jax.numpy as jnp
from jax import lax
from jax.experimental import pallas as pl
from jax.experimental.pallas import tpu as pltpu

D_MODEL = 2048
BATCH = 2
SEQ = 8192
DEPTH = 4

GRID_W = 64
CTX_LEN = 256

HY_W = 1024
HY_ORDER = 2
HY_EMB = 33
HY_BANDS = (HY_EMB - 1) // 2
HY_HID = 64
HY_FAST_DECAY = 0.3
HY_SLOW_DECAY = 1.5
HY_TARGET = 1e-2

MLA_HEADS = 8
MLA_NOPE = 128
MLA_ROPE = 64
MLA_V = 128
MLA_Q_RANK = 512
MLA_KV_RANK = 256
MLA_SCALE = (MLA_NOPE + MLA_ROPE) ** -0.5
Q_BLOCK = 128
ROPE_BASE = 10000.0

O_HY = 3 * HY_W
O_Q = O_HY + MLA_Q_RANK
O_KV = O_Q + MLA_KV_RANK
IN_W = O_KV + MLA_ROPE

S5_GC = 16
S5_NG = D_MODEL // S5_GC
S5_P = 64
S5_GB = 16
S5_NBLK = S5_NG // S5_GB

PEER_HEADS = 8
PEER_NK = 128
PEER_N = PEER_NK * PEER_NK
PEER_DK = 256
PEER_TOPK = 16
PEER_BLOCK = 128

N_EVEN = (DEPTH + 1) // 2
N_ODD = DEPTH // 2
DN_ALPHA = (2.0 * DEPTH) ** 0.25
DN_BETA = (8.0 * DEPTH) ** -0.25
LN_EPS = 1e-5
RMS_EPS = 1e-6
F32 = jnp.float32


def _ln_kernel(x_ref, g_ref, b_ref, o_ref):
    xf = x_ref[...]
    xc = xf - jnp.mean(xf, axis=-1, keepdims=True)
    var = jnp.mean(xc * xc, axis=-1, keepdims=True)
    o_ref[...] = xc * lax.rsqrt(var + LN_EPS) * g_ref[...] + b_ref[...]


def layer_norm_pallas(x, g, b):
    shp = x.shape
    x2 = x.reshape(-1, shp[-1])
    n, d = x2.shape
    tb = 256
    out = pl.pallas_call(
        _ln_kernel,
        grid=(n // tb,),
        in_specs=[pl.BlockSpec((tb, d), lambda i: (i, 0)),
                  pl.BlockSpec((1, d), lambda i: (0, 0)),
                  pl.BlockSpec((1, d), lambda i: (0, 0))],
        out_specs=pl.BlockSpec((tb, d), lambda i: (i, 0)),
        out_shape=jax.ShapeDtypeStruct((n, d), F32),
        name="layer_norm",
    )(x2, g.reshape(1, d), b.reshape(1, d))
    return out.reshape(shp)


BF16 = jnp.bfloat16
HI = lax.Precision.HIGHEST
VMEM_LIMIT = 56 * 1024 * 1024

S5_T = 16
S5_CB = 8


def _s5_operators(lam_re, lam_im, log_dt, b_re, b_im, c_re, c_im):
    T = S5_T
    ng, p, gc = b_re.shape[1:]
    dt = jnp.exp(log_dt)[..., None]
    zr, zi = lam_re * dt, lam_im * dt
    k = jnp.arange(T + 1, dtype=F32)[:, None, None, None]
    mag = jnp.exp(zr[None] * k)
    pr, pi = mag * jnp.cos(zi[None] * k), mag * jnp.sin(zi[None] * k)
    ar, ai = pr[1], pi[1]
    den = lam_re * lam_re + lam_im * lam_im
    cr = ((ar - 1.0) * lam_re + ai * lam_im) / den
    ci = (ai * lam_re - (ar - 1.0) * lam_im) / den
    bbr = cr[..., None] * b_re - ci[..., None] * b_im
    bbi = cr[..., None] * b_im + ci[..., None] * b_re
    car = c_re[None] * pr[:, :, :, None, :] - c_im[None] * pi[:, :, :, None, :]
    cai = c_re[None] * pi[:, :, :, None, :] + c_im[None] * pr[:, :, :, None, :]
    kk = (jnp.einsum('tdgcp,dgpe->tdgce', car[:T], bbr, precision=HI)
          - jnp.einsum('tdgcp,dgpe->tdgce', cai[:T], bbi, precision=HI))
    kf, kb = kk[:, 0], kk[:, 1]
    kfull = jnp.concatenate([kb[1:][::-1], (kf[0] + kb[0])[None], kf[1:]], axis=0)
    jj = jnp.arange(T)
    idx = jj[None, :] - jj[:, None] + (T - 1)
    m = kfull[idx]
    m = m.transpose(2, 0, 4, 1, 3).reshape(ng, T * gc, T * gc)
    def e_w(d, pw_idx):
        er = pr[pw_idx, d][..., None] * bbr[d][None] - pi[pw_idx, d][..., None] * bbi[d][None]
        ei = pr[pw_idx, d][..., None] * bbi[d][None] + pi[pw_idx, d][..., None] * bbr[d][None]
        w = jnp.concatenate([er, ei], axis=2)
        return w.transpose(1, 0, 3, 2).reshape(ng, T * gc, 2 * p)
    w_e = jnp.concatenate([e_w(0, T - 1 - jj), e_w(1, jj)], axis=-1)
    def s_w(d, pw_idx):
        w = jnp.concatenate([car[pw_idx, d], -cai[pw_idx, d]], axis=-1)
        return w.transpose(1, 3, 0, 2).reshape(ng, 2 * p, T * gc)
    w_y = jnp.concatenate([m, s_w(0, jj + 1), s_w(1, T - jj)], axis=1)
    a1 = jnp.concatenate([pr[T], pr[T]], axis=-1)
    a2 = jnp.concatenate([-pi[T], pi[T]], axis=-1)
    return w_e, w_y, a1, a2


def _s5_end_state_kernel(x_ref, w_ref, ef_ref, eb_ref):
    e = jnp.dot(x_ref[0], w_ref[0], precision=HI, preferred_element_type=F32)
    half = e.shape[1] // 2
    ef_ref[...] = e[:, :half]
    eb_ref[...] = e[:, half:]


def _s5_carry_kernel(ef_ref, eb_ref, a1_ref, a2_ref, s0f_ref, s0b_ref, sf_scr, sb_scr):
    @pl.when(pl.program_id(1) == 0)
    def _():
        sf_scr[...] = jnp.zeros_like(sf_scr)
        sb_scr[...] = jnp.zeros_like(sb_scr)

    half = sf_scr.shape[-1] // 2

    def step(s, e, d):
        return a1_ref[d] * s + a2_ref[d] * pltpu.roll(s, half, 1) + e

    for r in range(S5_CB):
        s = sf_scr[...]
        s0f_ref[r] = s
        sf_scr[...] = step(s, ef_ref[r], 0)
    for r in reversed(range(S5_CB)):
        s = sb_scr[...]
        s0b_ref[r] = s
        sb_scr[...] = step(s, eb_ref[r], 1)


def _s5_output_kernel(x_ref, s0f_ref, s0b_ref, w_ref, y_ref):
    kx = x_ref.shape[-1]
    ks = s0f_ref.shape[-1]
    w = w_ref[0]
    y = jnp.dot(x_ref[0], w[:kx], precision=HI, preferred_element_type=F32)
    y = y + jnp.dot(s0f_ref[...], w[kx:kx + ks], precision=HI, preferred_element_type=F32)
    y = y + jnp.dot(s0b_ref[...], w[kx + ks:], precision=HI, preferred_element_type=F32)
    y_ref[0] = y


def s5_scan(in_ctx, in_lat, lam_re, lam_im, log_dt, b_re, b_im, c_re, c_im):
    T = S5_T
    bn, lc, d = in_ctx.shape
    ll = in_lat.shape[1]
    ng, p, gc = b_re.shape[1:]
    w_e, w_y, a1, a2 = _s5_operators(lam_re, lam_im, log_dt, b_re, b_im, c_re, c_im)
    u = jnp.concatenate([in_ctx, in_lat], axis=1)
    nch = (lc + ll) // T
    rows = bn * nch
    kx, ks = T * gc, 2 * p
    xg = u.reshape(bn, nch, T, ng, gc).transpose(3, 0, 1, 2, 4).reshape(ng, rows, kx)

    ef, eb = pl.pallas_call(
        _s5_end_state_kernel,
        grid=(ng,),
        in_specs=[pl.BlockSpec((1, rows, kx), lambda g: (g, 0, 0)),
                  pl.BlockSpec((1, kx, 2 * ks), lambda g: (g, 0, 0))],
        out_specs=[pl.BlockSpec((rows, ks), lambda g: (0, g)),
                   pl.BlockSpec((rows, ks), lambda g: (0, g))],
        out_shape=[jax.ShapeDtypeStruct((rows, ng * ks), F32)] * 2,
        name="s5_end_state",
    )(xg, w_e)

    nblk = nch // S5_CB
    cblk = (lc // T) // S5_CB
    fwd_map = lambda b, j: (b * nblk + j, 0, 0)
    bwd_map = lambda b, j: (b * nblk + jnp.where(j < cblk, cblk - 1 - j, nblk + cblk - 1 - j), 0, 0)
    s0f, s0b = pl.pallas_call(
        _s5_carry_kernel,
        grid=(bn, nblk),
        in_specs=[pl.BlockSpec((S5_CB, ng, ks), fwd_map),
                  pl.BlockSpec((S5_CB, ng, ks), bwd_map),
                  pl.BlockSpec((2, ng, ks), lambda b, j: (0, 0, 0)),
                  pl.BlockSpec((2, ng, ks), lambda b, j: (0, 0, 0))],
        out_specs=[pl.BlockSpec((S5_CB, ng, ks), fwd_map),
                   pl.BlockSpec((S5_CB, ng, ks), bwd_map)],
        out_shape=[jax.ShapeDtypeStruct((rows, ng, ks), F32)] * 2,
        scratch_shapes=[pltpu.VMEM((ng, ks), F32), pltpu.VMEM((ng, ks), F32)],
        name="s5_carry",
    )(ef.reshape(rows, ng, ks), eb.reshape(rows, ng, ks), a1, a2)

    yg = pl.pallas_call(
        _s5_output_kernel,
        grid=(ng,),
        in_specs=[pl.BlockSpec((1, rows, kx), lambda g: (g, 0, 0)),
                  pl.BlockSpec((rows, ks), lambda g: (0, g)),
                  pl.BlockSpec((rows, ks), lambda g: (0, g)),
                  pl.BlockSpec((1, kx + 2 * ks, kx), lambda g: (g, 0, 0))],
        out_specs=pl.BlockSpec((1, rows, kx), lambda g: (g, 0, 0)),
        out_shape=jax.ShapeDtypeStruct((ng, rows, kx), F32),
        name="s5_output",
    )(xg, s0f.reshape(rows, ng * ks), s0b.reshape(rows, ng * ks), w_y)
    y = yg.reshape(ng, bn, nch, T, gc).transpose(1, 2, 3, 0, 4).reshape(bn, lc + ll, d)
    return y[:, :lc], y[:, lc:]


PEER_TS = 256
PEER_TB = 512
PEER_EB = 1024
LANES = 128
NEG = -3.0e38
_NN = (((1,), (0,)), ((), ()))
_NT = (((1,), (1,)), ((), ()))


def _split_bf16(a):
    hi = a.astype(BF16)
    return hi, (a - hi.astype(F32)).astype(BF16)


def _dot3(ah, al, bh, bl, dims):
    f = lambda a, b: lax.dot_general(a, b, dims, preferred_element_type=F32)
    return f(ah, bh) + f(al, bh) + f(ah, bl)


def _peer_cells():
    return PEER_TOPK + 8 * (PEER_TOPK // 2 - 1) + 8


def _peer_score_kernel(x_ref, wh_ref, wl_ref, kh_ref, kl_ref,
                       n1_ref, w1_ref, r2_ref, e2_ref,
                       st_scr, a1_scr, a2_scr, c_scr):
    nh, _, nk, dk = kh_ref.shape
    tb = x_ref.shape[0]
    xh, xl = _split_bf16(x_ref[...])
    q = _dot3(xh, xl, wh_ref[...], wl_ref[...], _NN)
    qh, ql = _split_bf16(q)
    for h in range(nh):
        for s in range(2):
            c0 = (2 * h + s) * dk
            st_scr[2 * h + s] = _dot3(kh_ref[h, s], kl_ref[h, s],
                                      qh[:, c0:c0 + dk], ql[:, c0:c0 + dk], _NT)

    def top_values(s, out_scr):
        rank = jnp.full(s.shape, float(PEER_TOPK), F32)
        for r in range(PEER_TOPK):
            m = jnp.max(s, axis=0, keepdims=True)
            out_scr[r:r + 1, :] = m
            hit = s == m
            rank = jnp.where(hit, float(r), rank)
            s = jnp.where(hit, NEG, s)
        return rank

    nsub = tb // LANES
    half = PEER_TOPK // 2

    def body(it, carry):
        h = it // nsub
        lanes = pl.ds(pl.multiple_of((it % nsub) * LANES, LANES), LANES)
        s1 = st_scr[2 * h, :, lanes]
        s2 = st_scr[2 * h + 1, :, lanes]
        top_values(s1, a1_scr)
        r2 = top_values(s2, a2_scr)
        a1 = a1_scr[...]
        a2 = a2_scr[...]
        c_scr[0:PEER_TOPK, :] = a1[0:1] + a2
        for k in range(1, half):
            c_scr[PEER_TOPK + 8 * (k - 1):PEER_TOPK + 8 * k, :] = a1[k:k + 1] + a2[0:8]
        c_scr[PEER_TOPK + 8 * (half - 1):, :] = a1[half:] + a2[0:1]
        c = c_scr[...]
        m0 = a1[0:1] + a2[0:1]
        z = jnp.zeros_like(m0)
        tau = m0
        for r in range(PEER_TOPK):
            tau = jnp.max(c, axis=0, keepdims=True)
            z = z + jnp.exp(tau - m0)
            c = jnp.where(c == tau, NEG, c)
        n1 = jnp.zeros_like(s1)
        for l in range(PEER_TOPK):
            n1 = n1 + jnp.where(s1 + a2[l:l + 1] >= tau, 1.0, 0.0)
        n1_ref[h, :, lanes] = n1
        w1_ref[h, :, lanes] = jnp.exp(s1 - a1[0:1]) / z
        r2_ref[h, :, lanes] = r2.astype(BF16)
        e2_ref[h, :, lanes] = jnp.exp(s2 - a2[0:1]).astype(BF16)
        return carry

    lax.fori_loop(0, nh * nsub, body, 0)


def _peer_expert_kernel(xt_ref, u_ref, vt_ref, n1_ref, w1_ref, r2_ref, e2_ref,
                        o_ref, acc_scr, a_scr, p_scr):
    e = pl.program_id(1)
    last = pl.num_programs(1) - 1
    slot = e % 2
    nh, nk, tb = r2_ref.shape
    ni = a_scr.shape[0] // nk

    @pl.when(e == 0)
    def _():
        acc_scr[...] = jnp.zeros_like(acc_scr)
        p_scr[1] = jnp.zeros(p_scr.shape[1:], BF16)

    @pl.when(e < last)
    def _():
        acc_scr[...] += jnp.dot(vt_ref[...], p_scr[1 - slot], preferred_element_type=F32)
        a_scr[...] = jnp.dot(u_ref[...], xt_ref[...], preferred_element_type=F32)
        for ts in range(tb // LANES):
            lanes = slice(ts * LANES, (ts + 1) * LANES)
            for ii in range(ni):
                rows = slice(ii * nk, (ii + 1) * nk)
                g = jnp.zeros((nk, LANES), BF16)
                for h in range(nh):
                    n1row = jnp.broadcast_to(n1_ref[h, ii:ii + 1, lanes].astype(BF16), (nk, LANES))
                    w1row = jnp.broadcast_to(w1_ref[h, ii:ii + 1, lanes].astype(BF16), (nk, LANES))
                    keep = r2_ref[h, :, lanes] < n1row
                    g = g + jnp.where(keep, e2_ref[h, :, lanes], jnp.zeros((), BF16)) * w1row
                p_scr[slot, rows, lanes] = (g.astype(F32) * jax.nn.gelu(a_scr[rows, lanes])).astype(BF16)

    @pl.when(e == last)
    def _():
        acc = acc_scr[...] + jnp.dot(vt_ref[...], p_scr[1 - slot], preferred_element_type=F32)
        o_ref[...] = acc.T


def peer_ffn_pallas(xin, w_q, keys, u_bf, vt_bf):
    n, d = xin.shape
    nh, _, nk, dk = keys.shape
    ne = u_bf.shape[0]
    ts = min(PEER_TS, n)
    tb = min(PEER_TB, n)
    eb = min(PEER_EB, ne)
    wh, wl = _split_bf16(w_q)
    kh, kl = _split_bf16(keys)
    feat32 = jax.ShapeDtypeStruct((nh, nk, n), F32)
    feat16 = jax.ShapeDtypeStruct((nh, nk, n), BF16)
    feat_spec = pl.BlockSpec((nh, nk, ts), lambda i: (0, 0, i))
    n1, w1, r2, e2 = pl.pallas_call(
        _peer_score_kernel,
        grid=(n // ts,),
        in_specs=[pl.BlockSpec((ts, d), lambda i: (i, 0)),
                  pl.BlockSpec(wh.shape, lambda i: (0, 0)),
                  pl.BlockSpec(wl.shape, lambda i: (0, 0)),
                  pl.BlockSpec(kh.shape, lambda i: (0, 0, 0, 0)),
                  pl.BlockSpec(kl.shape, lambda i: (0, 0, 0, 0))],
        out_specs=[feat_spec, feat_spec, feat_spec, feat_spec],
        out_shape=[feat32, feat32, feat16, feat16],
        scratch_shapes=[pltpu.VMEM((2 * nh, nk, ts), F32),
                        pltpu.VMEM((PEER_TOPK, LANES), F32),
                        pltpu.VMEM((PEER_TOPK, LANES), F32),
                        pltpu.VMEM((_peer_cells(), LANES), F32)],
        compiler_params=pltpu.CompilerParams(vmem_limit_bytes=VMEM_LIMIT),
        name="peer_scores",
    )(xin, wh, wl, kh, kl)

    ni = eb // nk
    nblk = ne // eb
    row_spec = pl.BlockSpec((nh, ni, tb), lambda t, e: (0, jnp.minimum(e, nblk - 1), t))
    col_spec = pl.BlockSpec((nh, nk, tb), lambda t, e: (0, 0, t))
    return pl.pallas_call(
        _peer_expert_kernel,
        grid=(n // tb, nblk + 1),
        in_specs=[pl.BlockSpec((d, tb), lambda t, e: (0, t)),
                  pl.BlockSpec((eb, d), lambda t, e: (jnp.minimum(e, nblk - 1), 0)),
                  pl.BlockSpec((d, eb), lambda t, e: (0, jnp.maximum(e - 1, 0))),
                  row_spec, row_spec, col_spec, col_spec],
        out_specs=pl.BlockSpec((tb, d), lambda t, e: (t, 0)),
        out_shape=jax.ShapeDtypeStruct((n, d), F32),
        scratch_shapes=[pltpu.VMEM((d, tb), F32),
                        pltpu.VMEM((eb, tb), F32),
                        pltpu.VMEM((2, eb, tb), BF16)],
        compiler_params=pltpu.CompilerParams(vmem_limit_bytes=VMEM_LIMIT,
                                             dimension_semantics=("arbitrary", "arbitrary")),
        name="peer_experts",
    )(xin.astype(BF16).T, u_bf, vt_bf, n1, w1, r2, e2)


MM_TM = 512
MM_TN = 512


def _mm_kernel(x_ref, w_ref, o_ref):
    o_ref[...] = jnp.dot(x_ref[...].astype(BF16), w_ref[...], preferred_element_type=F32)


def _glu_kernel(x_ref, wo_ref, wg_ref, o_ref):
    xb = x_ref[...].astype(BF16)
    a = jnp.dot(xb, wo_ref[...], preferred_element_type=F32)
    g = jnp.dot(xb, wg_ref[...], preferred_element_type=F32)
    o_ref[...] = a * jax.nn.sigmoid(g)


def _mm_call(body, x, ws, name):
    m, k = x.shape
    n = ws[0].shape[1]
    tm = min(MM_TM, m)
    tn = min(MM_TN, n)
    return pl.pallas_call(
        body,
        grid=(m // tm, n // tn),
        in_specs=[pl.BlockSpec((tm, k), lambda i, j: (i, 0))]
        + [pl.BlockSpec((k, tn), lambda i, j: (0, j)) for _ in ws],
        out_specs=pl.BlockSpec((tm, tn), lambda i, j: (i, j)),
        out_shape=jax.ShapeDtypeStruct((m, n), F32),
        compiler_params=pltpu.CompilerParams(vmem_limit_bytes=VMEM_LIMIT),
        name=name,
    )(x, *ws)


def matmul_pallas(x, w):
    n = w.shape[1]
    npad = -n % MM_TN if n > MM_TN else -n % LANES
    wb = jnp.pad(w.astype(BF16), ((0, 0), (0, npad)))
    out = _mm_call(_mm_kernel, x, [wb], "dense_proj")
    return out[:, :n] if npad else out


def glu_pallas(x, w_o, w_g):
    return _mm_call(_glu_kernel, x, [w_o.astype(BF16), w_g.astype(BF16)], "glu_proj")


ATT_TQ = 256
ATT_TK = 768


def _attn_kernel(q_ref, kt_ref, v_ref, o_ref, *, tk):
    q = q_ref[0, 0]
    tq = q.shape[0]
    nk = kt_ref.shape[-1]
    m = jnp.full((tq, 1), NEG, F32)
    l = jnp.zeros((tq, 1), F32)
    acc = jnp.zeros((tq, v_ref.shape[-1]), F32)
    for c in range(nk // tk):
        s = jnp.dot(q, kt_ref[0, 0, :, c * tk:(c + 1) * tk], preferred_element_type=F32)
        m_new = jnp.maximum(m, jnp.max(s, axis=-1, keepdims=True))
        alpha = jnp.exp(m - m_new)
        p = jnp.exp(s - m_new)
        l = alpha * l + jnp.sum(p, axis=-1, keepdims=True)
        acc = alpha * acc + jnp.dot(p.astype(BF16), v_ref[0, 0, c * tk:(c + 1) * tk, :],
                                    preferred_element_type=F32)
        m = m_new
    o_ref[0] = acc / l


def mla_attention_pallas(qn, qr, kn, kr, v):
    bn, lq, nh, _ = qn.shape
    sk = kn.shape[1]
    dv = v.shape[-1]
    q = (jnp.concatenate([qn, qr], axis=-1) * MLA_SCALE).astype(BF16).transpose(0, 2, 1, 3)
    kcat = jnp.concatenate([kn, jnp.broadcast_to(kr[:, :, None, :], (bn, sk, nh, kr.shape[-1]))], axis=-1)
    kt = kcat.astype(BF16).transpose(0, 2, 3, 1)
    vv = v.astype(BF16).transpose(0, 2, 1, 3)
    dq = q.shape[-1]
    tq = min(ATT_TQ, lq)
    tk = ATT_TK if sk % ATT_TK == 0 else sk
    return pl.pallas_call(
        functools.partial(_attn_kernel, tk=tk),
        grid=(bn, nh, lq // tq),
        in_specs=[pl.BlockSpec((1, 1, tq, dq), lambda b, h, i: (b, h, i, 0)),
                  pl.BlockSpec((1, 1, dq, sk), lambda b, h, i: (b, h, 0, 0)),
                  pl.BlockSpec((1, 1, sk, dv), lambda b, h, i: (b, h, 0, 0))],
        out_specs=pl.BlockSpec((1, tq, dv), lambda b, h, i: (b, i, h)),
        out_shape=jax.ShapeDtypeStruct((bn, lq, nh * dv), F32),
        compiler_params=pltpu.CompilerParams(vmem_limit_bytes=VMEM_LIMIT),
        name="mla_attention",
    )(q, kt, vv)


def layer_norm(x, g, b):
    xf = x.astype(F32)
    xc = xf - jnp.mean(xf, axis=-1, keepdims=True)
    var = jnp.mean(xc * xc, axis=-1, keepdims=True)
    return (xc * lax.rsqrt(var + LN_EPS) * g + b).astype(x.dtype)


def rms_norm(x, g):
    xf = x.astype(F32)
    return (xf * lax.rsqrt(jnp.mean(xf * xf, axis=-1, keepdims=True) + RMS_EPS) * g).astype(x.dtype)


def axial_rope(L):
    rows = L // GRID_W
    row = jnp.repeat(jnp.arange(rows), GRID_W).astype(F32)
    col = jnp.tile(jnp.arange(GRID_W), rows).astype(F32)
    half = MLA_ROPE // 2
    inv = ROPE_BASE ** (-jnp.arange(0, half, 2, dtype=F32) / half)
    ar, ac = row[:, None] * inv, col[:, None] * inv
    ang = jnp.concatenate([ar, ar, ac, ac], axis=-1)
    return jnp.cos(ang), jnp.sin(ang)


def apply_axial_rope(x, cos, sin):
    xs = x.reshape(x.shape[:-1] + (2, 2, MLA_ROPE // 4))
    rot = jnp.concatenate([-xs[..., 1:, :], xs[..., :1, :]], axis=-2).reshape(x.shape)
    return (x * cos + rot * sin).astype(x.dtype)


def short_conv3(z, w, b):
    L = z.shape[1]
    zp = jnp.pad(z, ((0, 0), (1, 1), (0, 0)))
    return zp[:, :L] * w[0] + zp[:, 1:L + 1] * w[1] + zp[:, 2:] * w[2] + b


def hyena_pos_features(L):
    t = jnp.linspace(0.0, 1.0, L, dtype=F32)[:, None]
    w = (2.0 * math.pi / L) * jnp.arange(L, dtype=F32)[:, None]
    f = jnp.linspace(1e-4, HY_BANDS - 1, HY_BANDS, dtype=F32)[None, :]
    return jnp.concatenate([t, jnp.cos(f * w), -jnp.sin(f * w)], axis=-1)


def hyena_filter_spectrum(L, w1, b1, w2, b2, w3, b3, w4, freq):
    act = lambda a: jnp.sin(freq * a)
    hdn = act(hyena_pos_features(L) @ w1 + b1)
    hdn = act(hdn @ w2 + b2)
    hdn = act(hdn @ w3 + b3)
    h = (hdn @ w4).astype(F32).reshape(L, 2, HY_ORDER, HY_W)
    deltas = jnp.abs(jnp.linspace(math.log(HY_TARGET) / HY_SLOW_DECAY, math.log(HY_TARGET) / HY_FAST_DECAY,
                                  HY_ORDER * HY_W, dtype=F32)).reshape(HY_ORDER, HY_W)
    t = jnp.linspace(0.0, 1.0, L, dtype=F32)[:, None, None, None]
    h = h * jnp.exp(-t * deltas)
    h = h / jnp.sum(jnp.abs(h), axis=(0, 1), keepdims=True)
    return jnp.concatenate([h[:, 0], jnp.zeros((1, HY_ORDER, HY_W), F32), h[1:, 1][::-1]], axis=0)


def fft_long_conv(u, k_f):
    L = u.shape[1]
    u_f = jnp.fft.rfft(u.astype(F32), n=2 * L, axis=1)
    return jnp.fft.irfft(u_f * k_f[None], n=2 * L, axis=1)[:, :L]


FFT_R = 128
FFT_CB = 16


def _dft_constants():
    r = FFT_R
    n = r * r
    idx = np.arange(r)
    f = np.exp(-2j * np.pi * np.outer(idx, idx) / r)
    tw = np.exp(-2j * np.pi * np.outer(idx, idx) / n)
    blk = lambda w: np.block([[w.real, w.imag], [-w.imag, w.real]])
    mats = {
        'w1': blk(f[:r // 2, :]),
        'w1f': np.concatenate([f.real, f.imag], 1),
        'w2': blk(f),
        'w2c': blk(np.conj(f)),
        'w1c': blk(np.conj(f)[:, :r // 2]) / n,
    }
    out = {}
    for name, w in mats.items():
        w32 = jnp.asarray(w, F32)
        out[name] = _split_bf16(w32)
    out['twa'] = jnp.asarray(np.concatenate([tw.real, tw.real], 1), F32)
    out['twb'] = jnp.asarray(np.concatenate([-tw.imag, tw.imag], 1), F32)
    out['twbc'] = jnp.asarray(np.concatenate([tw.imag, -tw.imag], 1), F32)
    return out


def _cmm(x, wh_ref, wl_ref):
    xh, xl = _split_bf16(x)
    return _dot3(xh, xl, wh_ref[...], wl_ref[...], _NN)


def _swap_halves(x):
    h = x.shape[-1] // 2
    return jnp.concatenate([x[..., h:], x[..., :h]], axis=-1)


def _cmul(x, a, b):
    return x * a + _swap_halves(x) * b


def _tile_transpose(x):
    h = x.shape[-1] // 2
    return jnp.concatenate([jnp.swapaxes(x[..., :h], 1, 2), jnp.swapaxes(x[..., h:], 1, 2)], axis=-1)


def _fft_forward(z2d, cb, w1h, w1l, w2h, w2l, twa, twb):
    r = FFT_R
    x = _cmm(z2d, w1h, w1l).reshape(cb, r, 2 * r)
    x = _tile_transpose(_cmul(x, twa[...], twb[...]))
    return _cmm(x.reshape(cb * r, 2 * r), w2h, w2l).reshape(cb, r, 2 * r)


def _hy_filter_kernel(k_ref, w1h, w1l, w2h, w2l, twa, twb, o_ref):
    cb = k_ref.shape[0]
    o_ref[...] = _fft_forward(k_ref[...].reshape(cb * FFT_R, FFT_R), cb, w1h, w1l, w2h, w2l, twa, twb)


def _hy_conv_kernel(z_ref, k_ref, w1h, w1l, w2h, w2l, w2ch, w2cl, w1ch, w1cl, twa, twb, twbc, o_ref):
    cb = z_ref.shape[0]
    r = FFT_R
    x = _fft_forward(z_ref[...].reshape(cb * r, r), cb, w1h, w1l, w2h, w2l, twa, twb)
    k = k_ref[...]
    kr, ki = k[..., :r], k[..., r:]
    x = _cmul(x, jnp.concatenate([kr, kr], axis=-1), jnp.concatenate([-ki, ki], axis=-1))
    x = _cmm(x.reshape(cb * r, 2 * r), w2ch, w2cl).reshape(cb, r, 2 * r)
    x = _tile_transpose(_cmul(x, twa[...], twbc[...]))
    o_ref[...] = _cmm(x.reshape(cb * r, 2 * r), w1ch, w1cl).reshape(cb, r, r)


def _const_spec(a):
    return pl.BlockSpec(a.shape, lambda i: (0,) * a.ndim)


def hyena_filter_fft(k_time, consts):
    r = FFT_R
    c = k_time.shape[1]
    kt = k_time.reshape(r, r, c).transpose(2, 1, 0)
    ws = [*consts['w1f'], *consts['w2'], consts['twa'], consts['twb']]
    return pl.pallas_call(
        _hy_filter_kernel,
        grid=(c // FFT_CB,),
        in_specs=[pl.BlockSpec((FFT_CB, r, r), lambda i: (i, 0, 0))] + [_const_spec(w) for w in ws],
        out_specs=pl.BlockSpec((FFT_CB, r, 2 * r), lambda i: (i, 0, 0)),
        out_shape=jax.ShapeDtypeStruct((c, r, 2 * r), F32),
        compiler_params=pltpu.CompilerParams(vmem_limit_bytes=VMEM_LIMIT),
        name="hyena_filter_fft",
    )(kt, *ws)


def hyena_long_conv(u, k_spec, consts):
    r = FFT_R
    bn, ll, c = u.shape
    assert bn == 2 and 2 * ll == r * r
    z = u.reshape(2, r // 2, r, c).transpose(3, 2, 0, 1).reshape(c, r, r)
    ws = [*consts['w1'], *consts['w2'], *consts['w2c'], *consts['w1c'],
          consts['twa'], consts['twb'], consts['twbc']]
    y = pl.pallas_call(
        _hy_conv_kernel,
        grid=(c // FFT_CB,),
        in_specs=[pl.BlockSpec((FFT_CB, r, r), lambda i: (i, 0, 0)),
                  pl.BlockSpec((FFT_CB, r, 2 * r), lambda i: (i, 0, 0))] + [_const_spec(w) for w in ws],
        out_specs=pl.BlockSpec((FFT_CB, r, r), lambda i: (i, 0, 0)),
        out_shape=jax.ShapeDtypeStruct((c, r, r), F32),
        compiler_params=pltpu.CompilerParams(vmem_limit_bytes=VMEM_LIMIT),
        name="hyena_long_conv",
    )(z, k_spec, *ws)
    return y.reshape(c, r, 2, r // 2).transpose(2, 3, 1, 0).reshape(2, ll, c)


def hyena_mixer(z, conv_w, conv_b, w1, b1, w2, b2, w3, b3, w4, freq, bias):
    L = z.shape[1]
    zc = short_conv3(z, conv_w, conv_b)
    x1, x2, v = jnp.split(zc, 3, axis=-1)
    k_t = hyena_filter_spectrum(L, w1, b1, w2, b2, w3, b3, w4, freq)
    y = v.astype(F32)
    use_mxu_fft = z.shape[0] == 2 and 2 * L == FFT_R * FFT_R
    if use_mxu_fft:
        consts = _dft_constants()
    else:
        k_f = jnp.fft.rfft(k_t, axis=0)
    for o, gate in enumerate((x1, x2)):
        if use_mxu_fft:
            conv = hyena_long_conv(y, hyena_filter_fft(k_t[:, o], consts), consts)
        else:
            conv = fft_long_conv(y, k_f[:, o])
        y = gate * (conv + y * bias[o])
    return y.astype(z.dtype)


def _mm3(x, w):
    b, l, k = x.shape
    return matmul_pallas(x.reshape(b * l, k), w).reshape(b, l, w.shape[1])


def mla_heads(z, q_norm, w_uq, kv_norm, w_ukv):
    Bn, L, _ = z.shape
    q = _mm3(rms_norm(z[..., O_HY:O_Q], q_norm), w_uq).reshape(Bn, L, MLA_HEADS, MLA_NOPE + MLA_ROPE)
    kv = _mm3(rms_norm(z[..., O_Q:O_KV], kv_norm), w_ukv).reshape(Bn, L, MLA_HEADS, MLA_NOPE + MLA_V)
    return q[..., :MLA_NOPE], q[..., MLA_NOPE:], kv[..., :MLA_NOPE], z[..., O_KV:], kv[..., MLA_NOPE:]


def even_mixer(in_ctx, in_lat, cos, sin, need_ctx, w_in, hy, mla, w_o):
    z_ctx = _mm3(in_ctx, w_in)
    z_lat = _mm3(in_lat, w_in)
    cqn, cqr, ckn, ckr, cv = mla_heads(z_ctx, *mla)
    lqn, lqr, lkn, lkr, lv = mla_heads(z_lat, *mla)
    lqr = apply_axial_rope(lqr, cos[:, None, :], sin[:, None, :])
    lkr = apply_axial_rope(lkr, cos, sin)
    kn = jnp.concatenate([ckn, lkn], axis=1)
    kr = jnp.concatenate([ckr, lkr], axis=1)
    vv = jnp.concatenate([cv, lv], axis=1)
    att = mla_attention_pallas(lqn, lqr, kn, kr, vv)
    y_lat = _mm3(jnp.concatenate([hyena_mixer(z_lat[..., :O_HY], *hy), att], axis=-1), w_o)
    y_ctx = None
    if need_ctx:
        att_c = mla_attention_pallas(cqn, cqr, ckn, ckr, cv)
        y_ctx = _mm3(jnp.concatenate([hyena_mixer(z_ctx[..., :O_HY], *hy), att_c], axis=-1), w_o)
    return y_ctx, y_lat


def _scan_op(e1, e2):
    a1, b1 = e1
    a2, b2 = e2
    return a1 * a2, a2 * b1 + b2


def linear_scan(a_bar, bu, s0):
    bu = bu.at[:, 0].add(a_bar * s0)
    a = jnp.broadcast_to(a_bar, bu.shape)
    _, s = lax.associative_scan(_scan_op, (a, bu), axis=1)
    return s


def _rev(s, d):
    return s[:, ::-1] if d == 1 else s


def s5_mixer(in_ctx, in_lat, need_ctx, lam_re, lam_im, log_dt, b_re, b_im, c_re, c_im, d_skip, w_o, w_g):
    s_ctx, s_lat = s5_scan(in_ctx, in_lat, lam_re, lam_im, log_dt, b_re, b_im, c_re, c_im)

    def finish(ys, h):
        y = ys + d_skip * h.astype(F32)
        y = jax.nn.gelu(y).astype(h.dtype)
        b, l, dm = y.shape
        return glu_pallas(y.reshape(b * l, dm), w_o, w_g).reshape(b, l, dm)

    y_lat = finish(s_lat, in_lat)
    y_ctx = finish(s_ctx, in_ctx) if need_ctx else None
    return y_ctx, y_lat


def peer_ffn(h, w_q, keys, u_tab, v_tab):
    Bn, L, D = h.shape
    hb = h.reshape(Bn * L // PEER_BLOCK, PEER_BLOCK, D)

    def block_fn(xb):
        q = (xb @ w_q).reshape(PEER_BLOCK, PEER_HEADS, 2, PEER_DK // 2)
        s = jnp.einsum('thsk,hsnk->thsn', q, keys).astype(F32)
        top_s, top_i = lax.top_k(s, PEER_TOPK)
        cand = top_s[:, :, 0, :, None] + top_s[:, :, 1, None, :]
        cand_s, cand_j = lax.top_k(cand.reshape(PEER_BLOCK, PEER_HEADS, PEER_TOPK * PEER_TOPK), PEER_TOPK)
        idx = (jnp.take_along_axis(top_i[:, :, 0], cand_j // PEER_TOPK, axis=-1) * PEER_NK
               + jnp.take_along_axis(top_i[:, :, 1], cand_j % PEER_TOPK, axis=-1))
        g = jax.nn.softmax(cand_s, axis=-1)
        act = jax.nn.gelu(jnp.einsum('thkd,td->thk', u_tab[idx], xb).astype(F32))
        return jnp.einsum('thk,thkd->td', (g * act).astype(xb.dtype), v_tab[idx])

    return lax.map(block_fn, hb).reshape(Bn, L, D)


def kernel(x, c, ctx, c_ctx, mod_w, mod_b, ln_mix_g, ln_mix_b, ln_ffn_g, ln_ffn_b,
           ev_w_in, ev_conv_w, ev_conv_b, hy_w1, hy_b1, hy_w2, hy_b2, hy_w3, hy_b3, hy_w4,
           hy_freq, hy_bias, mla_q_norm, mla_w_uq, mla_kv_norm, mla_w_ukv, ev_w_o,
           s5_lam_re, s5_lam_im, s5_log_dt, s5_b_re, s5_b_im, s5_c_re, s5_c_im, s5_d,
           od_w_o, od_w_g, peer_w_q, peer_keys, peer_u, peer_v):
    cos, sin = axial_rope(x.shape[1])
    act_lat = jax.nn.silu(c)
    act_ctx = jax.nn.silu(c_ctx)
    h_lat, h_ctx = x, ctx
    for layer in range(DEPTH):
        need_ctx = layer < DEPTH - 1
        m_lat = jnp.split((act_lat @ mod_w[layer] + mod_b[layer])[:, None, :], 6, axis=-1)
        m_ctx = jnp.split((act_ctx @ mod_w[layer] + mod_b[layer])[None, None, :], 6, axis=-1)
        in_lat = h_lat * (1.0 + m_lat[1]) + m_lat[0]
        in_ctx = h_ctx * (1.0 + m_ctx[1]) + m_ctx[0]
        i = layer // 2
        if layer % 2 == 0:
            hy = (ev_conv_w[i], ev_conv_b[i], hy_w1[i], hy_b1[i], hy_w2[i], hy_b2[i], hy_w3[i], hy_b3[i],
                  hy_w4[i], hy_freq[i], hy_bias[i])
            mla = (mla_q_norm[i], mla_w_uq[i], mla_kv_norm[i], mla_w_ukv[i])
            y_ctx, y_lat = even_mixer(in_ctx, in_lat, cos, sin, need_ctx, ev_w_in[i], hy, mla, ev_w_o[i])
        else:
            y_ctx, y_lat = s5_mixer(in_ctx, in_lat, need_ctx, s5_lam_re[i], s5_lam_im[i], s5_log_dt[i],
                                    s5_b_re[i], s5_b_im[i], s5_c_re[i], s5_c_im[i], s5_d[i],
                                    od_w_o[i], od_w_g[i])
        h_lat = layer_norm(DN_ALPHA * h_lat + m_lat[2] * y_lat, ln_mix_g[layer], ln_mix_b[layer])
        bn, ll, dm = h_lat.shape
        xin = (h_lat * (1.0 + m_lat[4]) + m_lat[3]).reshape(bn * ll, dm)
        if need_ctx:
            h_ctx = layer_norm(DN_ALPHA * h_ctx + m_ctx[2] * y_ctx, ln_mix_g[layer], ln_mix_b[layer])
            xin_ctx = (h_ctx * (1.0 + m_ctx[4]) + m_ctx[3]).reshape(-1, dm)
            xin = jnp.concatenate([xin, xin_ctx], axis=0)
        f_all = peer_ffn_pallas(xin, peer_w_q[layer], peer_keys[layer],
                                peer_u[layer].astype(BF16), peer_v[layer].T.astype(BF16))
        f_lat = f_all[:bn * ll].reshape(bn, ll, dm)
        pre = DN_ALPHA * h_lat + m_lat[5] * f_lat
        if layer == DEPTH - 1:
            h_lat = layer_norm_pallas(pre, ln_ffn_g[layer], ln_ffn_b[layer])
        else:
            h_lat = layer_norm(pre, ln_ffn_g[layer], ln_ffn_b[layer])
        if need_ctx:
            f_ctx = f_all[bn * ll:].reshape(h_ctx.shape)
            h_ctx = layer_norm(DN_ALPHA * h_ctx + m_ctx[5] * f_ctx, ln_ffn_g[layer], ln_ffn_b[layer])
    return h_lat
```

```python
import math
import functools
import numpy as np
import jax
import jax.numpy as jnp
from jax import lax
from jax.experimental import pallas as pl
from jax.experimental.pallas import tpu as pltpu

D_MODEL = 2048
BATCH = 2
SEQ = 8192
DEPTH = 4

GRID_W = 64
CTX_LEN = 256

HY_W = 1024
HY_ORDER = 2
HY_EMB = 33
HY_BANDS = (HY_EMB - 1) // 2
HY_HID = 64
HY_FAST_DECAY = 0.3
HY_SLOW_DECAY = 1.5
HY_TARGET = 1e-2

MLA_HEADS = 8
MLA_NOPE = 128
MLA_ROPE = 64
MLA_V = 128
MLA_Q_RANK = 512
MLA_KV_RANK = 256
MLA_SCALE = (MLA_NOPE + MLA_ROPE) ** -0.5
Q_BLOCK = 128
ROPE_BASE = 10000.0

O_HY = 3 * HY_W
O_Q = O_HY + MLA_Q_RANK
O_KV = O_Q + MLA_KV_RANK
IN_W = O_KV + MLA_ROPE

S5_GC = 16
S5_NG = D_MODEL // S5_GC
S5_P = 64
S5_GB = 16
S5_NBLK = S5_NG // S5_GB

PEER_HEADS = 8
PEER_NK = 128
PEER_N = PEER_NK * PEER_NK
PEER_DK = 256
PEER_TOPK = 16
PEER_BLOCK = 128

N_EVEN = (DEPTH + 1) // 2
N_ODD = DEPTH // 2
DN_ALPHA = (2.0 * DEPTH) ** 0.25
DN_BETA = (8.0 * DEPTH) ** -0.25
LN_EPS = 1e-5
RMS_EPS = 1e-6
F32 = jnp.float32


def _ln_kernel(x_ref, g_ref, b_ref, o_ref):
    xf = x_ref[...]
    xc = xf - jnp.mean(xf, axis=-1, keepdims=True)
    var = jnp.mean(xc * xc, axis=-1, keepdims=True)
    o_ref[...] = xc * lax.rsqrt(var + LN_EPS) * g_ref[...] + b_ref[...]


def layer_norm_pallas(x, g, b):
    shp = x.shape
    x2 = x.reshape(-1, shp[-1])
    n, d = x2.shape
    tb = 256
    out = pl.pallas_call(
        _ln_kernel,
        grid=(n // tb,),
        in_specs=[pl.BlockSpec((tb, d), lambda i: (i, 0)),
                  pl.BlockSpec((1, d), lambda i: (0, 0)),
                  pl.BlockSpec((1, d), lambda i: (0, 0))],
        out_specs=pl.BlockSpec((tb, d), lambda i: (i, 0)),
        out_shape=jax.ShapeDtypeStruct((n, d), F32),
        name="layer_norm",
    )(x2, g.reshape(1, d), b.reshape(1, d))
    return out.reshape(shp)


BF16 = jnp.bfloat16
HI = lax.Precision.HIGHEST
VMEM_LIMIT = 56 * 1024 * 1024

S5_T = 16
S5_CB = 8


def _s5_operators(lam_re, lam_im, log_dt, b_re, b_im, c_re, c_im):
    T = S5_T
    ng, p, gc = b_re.shape[1:]
    dt = jnp.exp(log_dt)[..., None]
    zr, zi = lam_re * dt, lam_im * dt
    k = jnp.arange(T + 1, dtype=F32)[:, None, None, None]
    mag = jnp.exp(zr[None] * k)
    pr, pi = mag * jnp.cos(zi[None] * k), mag * jnp.sin(zi[None] * k)
    ar, ai = pr[1], pi[1]
    den = lam_re * lam_re + lam_im * lam_im
    cr = ((ar - 1.0) * lam_re + ai * lam_im) / den
    ci = (ai * lam_re - (ar - 1.0) * lam_im) / den
    bbr = cr[..., None] * b_re - ci[..., None] * b_im
    bbi = cr[..., None] * b_im + ci[..., None] * b_re
    car = c_re[None] * pr[:, :, :, None, :] - c_im[None] * pi[:, :, :, None, :]
    cai = c_re[None] * pi[:, :, :, None, :] + c_im[None] * pr[:, :, :, None, :]
    kk = (jnp.einsum('tdgcp,dgpe->tdgce', car[:T], bbr, precision=HI)
          - jnp.einsum('tdgcp,dgpe->tdgce', cai[:T], bbi, precision=HI))
    kf, kb = kk[:, 0], kk[:, 1]
    kfull = jnp.concatenate([kb[1:][::-1], (kf[0] + kb[0])[None], kf[1:]], axis=0)
    jj = jnp.arange(T)
    idx = jj[None, :] - jj[:, None] + (T - 1)
    m = kfull[idx]
    m = m.transpose(2, 0, 4, 1, 3).reshape(ng, T * gc, T * gc)
    def e_w(d, pw_idx):
        er = pr[pw_idx, d][..., None] * bbr[d][None] - pi[pw_idx, d][..., None] * bbi[d][None]
        ei = pr[pw_idx, d][..., None] * bbi[d][None] + pi[pw_idx, d][..., None] * bbr[d][None]
        w = jnp.concatenate([er, ei], axis=2)
        return w.transpose(1, 0, 3, 2).reshape(ng, T * gc, 2 * p)
    w_e = jnp.concatenate([e_w(0, T - 1 - jj), e_w(1, jj)], axis=-1)
    def s_w(d, pw_idx):
        w = jnp.concatenate([car[pw_idx, d], -cai[pw_idx, d]], axis=-1)
        return w.transpose(1, 3, 0, 2).reshape(ng, 2 * p, T * gc)
    w_y = jnp.concatenate([m, s_w(0, jj + 1), s_w(1, T - jj)], axis=1)
    a1 = jnp.concatenate([pr[T], pr[T]], axis=-1)
    a2 = jnp.concatenate([-pi[T], pi[T]], axis=-1)
    return w_e, w_y, a1, a2


def _s5_lane_group():
    return lax.broadcasted_iota(jnp.int32, (1, LANES), 1) // S5_GC


def _s5_row_block(n):
    return 88 if n % 88 == 0 else 8


def _s5_group_rows(x_ref, xg_scr):
    nrow, T, _ = x_ref.shape
    per = LANES // S5_GC
    grp = _s5_lane_group()
    rb = _s5_row_block(nrow)

    def body(i, carry):
        rows = pl.ds(pl.multiple_of(i * rb, 8), rb)
        for g in range(per):
            for hf in range(T // per):
                acc = None
                for j in range(per):
                    xt = x_ref[rows, hf * per + j, :]
                    sh = ((j - g) * S5_GC) % LANES
                    r = pltpu.roll(xt, sh, 1) if sh else xt
                    acc = r if acc is None else jnp.where(grp == j, r, acc)
                xg_scr[g, rows, hf * LANES:(hf + 1) * LANES] = acc
        return carry

    lax.fori_loop(0, nrow // rb, body, 0)


def _s5_scatter_rows(yg_scr, o_ref):
    nrow, T, _ = o_ref.shape
    per = LANES // S5_GC
    grp = _s5_lane_group()
    rb = _s5_row_block(nrow)

    def body(i, carry):
        rows = pl.ds(pl.multiple_of(i * rb, 8), rb)
        for t in range(T):
            hf, j = divmod(t, per)
            acc = None
            for g in range(per):
                piece = yg_scr[g, rows, hf * LANES:(hf + 1) * LANES]
                sh = ((g - j) * S5_GC) % LANES
                r = pltpu.roll(piece, sh, 1) if sh else piece
                acc = r if acc is None else jnp.where(grp == g, r, acc)
            o_ref[rows, t, :] = acc
        return carry

    lax.fori_loop(0, nrow // rb, body, 0)


def _s5_end_state_kernel(x_ref, w_ref, ef_ref, eb_ref, xg_scr):
    _s5_group_rows(x_ref, xg_scr)
    half = w_ref.shape[-1] // 2

    def body(g, carry):
        e = jnp.dot(xg_scr[g], w_ref[g], precision=HI, preferred_element_type=F32)
        cols = pl.ds(pl.multiple_of(g * half, LANES), half)
        ef_ref[:, cols] = e[:, :half]
        eb_ref[:, cols] = e[:, half:]
        return carry

    lax.fori_loop(0, xg_scr.shape[0], body, 0)


def _s5_carry_kernel(ef_ref, eb_ref, a1_ref, a2_ref, s0f_ref, s0b_ref, sf_scr, sb_scr):
    @pl.when(pl.program_id(1) == 0)
    def _():
        sf_scr[...] = jnp.zeros_like(sf_scr)
        sb_scr[...] = jnp.zeros_like(sb_scr)

    half = sf_scr.shape[-1] // 2

    def step(s, e, d):
        return a1_ref[d] * s + a2_ref[d] * pltpu.roll(s, half, 1) + e

    for r in range(S5_CB):
        s = sf_scr[...]
        s0f_ref[r] = s
        sf_scr[...] = step(s, ef_ref[r], 0)
    for r in reversed(range(S5_CB)):
        s = sb_scr[...]
        s0b_ref[r] = s
        sb_scr[...] = step(s, eb_ref[r], 1)


def _s5_output_kernel(x_ref, s0f_ref, s0b_ref, w_ref, y_ref, xg_scr):
    _s5_group_rows(x_ref, xg_scr)
    kx = xg_scr.shape[-1]
    ks = s0f_ref.shape[-1] // xg_scr.shape[0]

    def body(g, carry):
        cols = pl.ds(pl.multiple_of(g * ks, LANES), ks)
        y = jnp.dot(xg_scr[g], w_ref[g, :kx, :], precision=HI, preferred_element_type=F32)
        y = y + jnp.dot(s0f_ref[:, cols], w_ref[g, kx:kx + ks, :], precision=HI, preferred_element_type=F32)
        y = y + jnp.dot(s0b_ref[:, cols], w_ref[g, kx + ks:, :], precision=HI, preferred_element_type=F32)
        xg_scr[g] = y
        return carry

    lax.fori_loop(0, xg_scr.shape[0], body, 0)
    _s5_scatter_rows(xg_scr, y_ref)


def s5_scan(in_ctx, in_lat, lam_re, lam_im, log_dt, b_re, b_im, c_re, c_im):
    T = S5_T
    bn, lc, d = in_ctx.shape
    ll = in_lat.shape[1]
    ng, p, gc = b_re.shape[1:]
    w_e, w_y, a1, a2 = _s5_operators(lam_re, lam_im, log_dt, b_re, b_im, c_re, c_im)
    u = jnp.concatenate([in_ctx, in_lat], axis=1)
    nch = (lc + ll) // T
    rows = bn * nch
    kx, ks = T * gc, 2 * p
    per = LANES // gc
    ncb = ng // per
    x3 = u.reshape(rows, T, d)

    ef, eb = pl.pallas_call(
        _s5_end_state_kernel,
        grid=(ncb, bn),
        in_specs=[pl.BlockSpec((nch, T, LANES), lambda c, r: (r, 0, c)),
                  pl.BlockSpec((per, kx, 2 * ks), lambda c, r: (c, 0, 0))],
        out_specs=[pl.BlockSpec((nch, per * ks), lambda c, r: (r, c)),
                   pl.BlockSpec((nch, per * ks), lambda c, r: (r, c))],
        out_shape=[jax.ShapeDtypeStruct((rows, ng * ks), F32)] * 2,
        scratch_shapes=[pltpu.VMEM((per, nch, kx), F32)],
        compiler_params=pltpu.CompilerParams(vmem_limit_bytes=VMEM_LIMIT),
        name="s5_end_state",
    )(x3, w_e)

    nblk = nch // S5_CB
    cblk = (lc // T) // S5_CB
    fwd_map = lambda b, j: (b * nblk + j, 0, 0)
    bwd_map = lambda b, j: (b * nblk + jnp.where(j < cblk, cblk - 1 - j, nblk + cblk - 1 - j), 0, 0)
    s0f, s0b = pl.pallas_call(
        _s5_carry_kernel,
        grid=(bn, nblk),
        in_specs=[pl.BlockSpec((S5_CB, ng, ks), fwd_map),
                  pl.BlockSpec((S5_CB, ng, ks), bwd_map),
                  pl.BlockSpec((2, ng, ks), lambda b, j: (0, 0, 0)),
                  pl.BlockSpec((2, ng, ks), lambda b, j: (0, 0, 0))],
        out_specs=[pl.BlockSpec((S5_CB, ng, ks), fwd_map),
                   pl.BlockSpec((S5_CB, ng, ks), bwd_map)],
        out_shape=[jax.ShapeDtypeStruct((rows, ng, ks), F32)] * 2,
        scratch_shapes=[pltpu.VMEM((ng, ks), F32), pltpu.VMEM((ng, ks), F32)],
        name="s5_carry",
    )(ef.reshape(rows, ng, ks), eb.reshape(rows, ng, ks), a1, a2)

    y3 = pl.pallas_call(
        _s5_output_kernel,
        grid=(ncb, bn),
        in_specs=[pl.BlockSpec((nch, T, LANES), lambda c, r: (r, 0, c)),
                  pl.BlockSpec((nch, per * ks), lambda c, r: (r, c)),
                  pl.BlockSpec((nch, per * ks), lambda c, r: (r, c)),
                  pl.BlockSpec((per, kx + 2 * ks, kx), lambda c, r: (c, 0, 0))],
        out_specs=pl.BlockSpec((nch, T, LANES), lambda c, r: (r, 0, c)),
        out_shape=jax.ShapeDtypeStruct((rows, T, d), F32),
        scratch_shapes=[pltpu.VMEM((per, nch, kx), F32)],
        compiler_params=pltpu.CompilerParams(vmem_limit_bytes=VMEM_LIMIT),
        name="s5_output",
    )(x3, s0f.reshape(rows, ng * ks), s0b.reshape(rows, ng * ks), w_y)
    y = y3.reshape(bn, lc + ll, d)
    return y[:, :lc], y[:, lc:]


PEER_TS = 256
PEER_TB = 512
PEER_EB = 1024
LANES = 128
NEG = -3.0e38
_NN = (((1,), (0,)), ((), ()))
_NT = (((1,), (1,)), ((), ()))


def _split_bf16(a):
    hi = a.astype(BF16)
    return hi, (a - hi.astype(F32)).astype(BF16)


def _dot3(ah, al, bh, bl, dims):
    f = lambda a, b: lax.dot_general(a, b, dims, preferred_element_type=F32)
    return f(ah, bh) + f(al, bh) + f(ah, bl)


def _peer_cells():
    return PEER_TOPK + 8 * (PEER_TOPK // 2 - 1) + 8


def _peer_score_kernel(x_ref, wq_ref, kh_ref, kl_ref,
                       n1_ref, w1_ref, r2_ref, e2_ref,
                       st_scr, a1_scr, a2_scr, c_scr):
    nh, _, nk, dk = kh_ref.shape
    tb = x_ref.shape[0]
    q = jnp.dot(x_ref[...].astype(BF16), wq_ref[...], preferred_element_type=F32)
    qh, ql = _split_bf16(q)
    for h in range(nh):
        for s in range(2):
            c0 = (2 * h + s) * dk
            st_scr[2 * h + s] = _dot3(kh_ref[h, s], kl_ref[h, s],
                                      qh[:, c0:c0 + dk], ql[:, c0:c0 + dk], _NT)

    def top_values(s, out_scr):
        rank = jnp.full(s.shape, float(PEER_TOPK), F32)
        for r in range(PEER_TOPK):
            m = jnp.max(s, axis=0, keepdims=True)
            out_scr[r:r + 1, :] = m
            hit = s == m
            rank = jnp.where(hit, float(r), rank)
            s = jnp.where(hit, NEG, s)
        return rank

    nsub = tb // LANES
    half = PEER_TOPK // 2

    def body(it, carry):
        h = it // nsub
        lanes = pl.ds(pl.multiple_of((it % nsub) * LANES, LANES), LANES)
        s1 = st_scr[2 * h, :, lanes]
        s2 = st_scr[2 * h + 1, :, lanes]
        top_values(s1, a1_scr)
        r2 = top_values(s2, a2_scr)
        a1 = a1_scr[...]
        a2 = a2_scr[...]
        c_scr[0:PEER_TOPK, :] = a1[0:1] + a2
        for k in range(1, half):
            c_scr[PEER_TOPK + 8 * (k - 1):PEER_TOPK + 8 * k, :] = a1[k:k + 1] + a2[0:8]
        c_scr[PEER_TOPK + 8 * (half - 1):, :] = a1[half:] + a2[0:1]
        c = c_scr[...]
        m0 = a1[0:1] + a2[0:1]
        z = jnp.zeros_like(m0)
        tau = m0
        for r in range(PEER_TOPK):
            tau = jnp.max(c, axis=0, keepdims=True)
            z = z + jnp.exp(tau - m0)
            c = jnp.where(c == tau, NEG, c)
        n1 = jnp.zeros_like(s1)
        for l in range(PEER_TOPK):
            n1 = n1 + jnp.where(s1 + a2[l:l + 1] >= tau, 1.0, 0.0)
        n1_ref[h, :, lanes] = n1
        w1_ref[h, :, lanes] = jnp.exp(s1 - a1[0:1]) / z
        r2_ref[h, :, lanes] = r2.astype(BF16)
        e2_ref[h, :, lanes] = jnp.exp(s2 - a2[0:1]).astype(BF16)
        return carry

    lax.fori_loop(0, nh * nsub, body, 0)


def _peer_expert_kernel(xt_ref, u_ref, vt_ref, n1_ref, w1_ref, r2_ref, e2_ref,
                        o_ref, acc_scr, a_scr, p_scr, g_scr):
    e = pl.program_id(1)
    last = pl.num_programs(1) - 1
    slot = e % 2
    nh, nk, tb = r2_ref.shape
    ni = a_scr.shape[0] // nk

    @pl.when(e == 0)
    def _():
        acc_scr[...] = jnp.zeros_like(acc_scr)
        p_scr[1] = jnp.zeros(p_scr.shape[1:], BF16)

    def gate_tiles(ts):
        lanes = slice(ts * LANES, (ts + 1) * LANES)
        for ii in range(ni):
            rows = slice(ii * nk, (ii + 1) * nk)
            g = jnp.zeros((nk, LANES), BF16)
            for h in range(nh):
                n1row = jnp.broadcast_to(n1_ref[h, ii:ii + 1, lanes].astype(BF16), (nk, LANES))
                w1row = jnp.broadcast_to(w1_ref[h, ii:ii + 1, lanes].astype(BF16), (nk, LANES))
                keep = r2_ref[h, :, lanes] < n1row
                g = g + jnp.where(keep, e2_ref[h, :, lanes], jnp.zeros((), BF16)) * w1row
            g_scr[rows, lanes] = g

    def act_tiles(ts):
        lanes = slice(ts * LANES, (ts + 1) * LANES)
        for ii in range(ni):
            rows = slice(ii * nk, (ii + 1) * nk)
            p_scr[slot, rows, lanes] = (g_scr[rows, lanes].astype(F32)
                                        * jax.nn.gelu(a_scr[rows, lanes])).astype(BF16)

    @pl.when(e < last)
    def _():
        nsub = tb // LANES
        hw = tb // 2
        halves = (slice(0, hw), slice(hw, tb))
        for k, cols in enumerate(halves):
            acc_scr[:, cols] += jnp.dot(vt_ref[...], p_scr[1 - slot, :, cols], preferred_element_type=F32)
            for ts in range(k * nsub // 2, (k + 1) * nsub // 2):
                gate_tiles(ts)
        for k, cols in enumerate(halves):
            a_scr[:, cols] = jnp.dot(u_ref[...], xt_ref[:, cols], preferred_element_type=F32)
            for ts in range(k * nsub // 2, (k + 1) * nsub // 2):
                act_tiles(ts)

    @pl.when(e == last)
    def _():
        acc = acc_scr[...] + jnp.dot(vt_ref[...], p_scr[1 - slot], preferred_element_type=F32)
        o_ref[...] = acc.T


def peer_ffn_pallas(xin, w_q, keys, u_bf, vt_bf):
    n, d = xin.shape
    nh, _, nk, dk = keys.shape
    ne = u_bf.shape[0]
    ts = min(PEER_TS, n)
    tb = min(PEER_TB, n)
    eb = min(PEER_EB, ne)
    wq = w_q.astype(BF16)
    kh, kl = _split_bf16(keys)
    feat32 = jax.ShapeDtypeStruct((nh, nk, n), F32)
    feat16 = jax.ShapeDtypeStruct((nh, nk, n), BF16)
    feat_spec = pl.BlockSpec((nh, nk, ts), lambda i: (0, 0, i))
    n1, w1, r2, e2 = pl.pallas_call(
        _peer_score_kernel,
        grid=(n // ts,),
        in_specs=[pl.BlockSpec((ts, d), lambda i: (i, 0)),
                  pl.BlockSpec(wq.shape, lambda i: (0, 0)),
                  pl.BlockSpec(kh.shape, lambda i: (0, 0, 0, 0)),
                  pl.BlockSpec(kl.shape, lambda i: (0, 0, 0, 0))],
        out_specs=[feat_spec, feat_spec, feat_spec, feat_spec],
        out_shape=[feat32, feat32, feat16, feat16],
        scratch_shapes=[pltpu.VMEM((2 * nh, nk, ts), F32),
                        pltpu.VMEM((PEER_TOPK, LANES), F32),
                        pltpu.VMEM((PEER_TOPK, LANES), F32),
                        pltpu.VMEM((_peer_cells(), LANES), F32)],
        compiler_params=pltpu.CompilerParams(vmem_limit_bytes=VMEM_LIMIT),
        name="peer_scores",
    )(xin, wq, kh, kl)

    ni = eb // nk
    nblk = ne // eb
    row_spec = pl.BlockSpec((nh, ni, tb), lambda t, e: (0, jnp.minimum(e, nblk - 1), t))
    col_spec = pl.BlockSpec((nh, nk, tb), lambda t, e: (0, 0, t))
    return pl.pallas_call(
        _peer_expert_kernel,
        grid=(n // tb, nblk + 1),
        in_specs=[pl.BlockSpec((d, tb), lambda t, e: (0, t)),
                  pl.BlockSpec((eb, d), lambda t, e: (jnp.minimum(e, nblk - 1), 0)),
                  pl.BlockSpec((d, eb), lambda t, e: (0, jnp.maximum(e - 1, 0))),
                  row_spec, row_spec, col_spec, col_spec],
        out_specs=pl.BlockSpec((tb, d), lambda t, e: (t, 0)),
        out_shape=jax.ShapeDtypeStruct((n, d), F32),
        scratch_shapes=[pltpu.VMEM((d, tb), F32),
                        pltpu.VMEM((eb, tb), F32),
                        pltpu.VMEM((2, eb, tb), BF16),
                        pltpu.VMEM((eb, tb), BF16)],
        compiler_params=pltpu.CompilerParams(vmem_limit_bytes=VMEM_LIMIT,
                                             dimension_semantics=("arbitrary", "arbitrary")),
        name="peer_experts",
    )(xin.astype(BF16).T, u_bf, vt_bf, n1, w1, r2, e2)


MM_TM = 512
MM_TN = 512


def _mm_kernel(x_ref, w_ref, o_ref):
    o_ref[...] = jnp.dot(x_ref[...].astype(BF16), w_ref[...], preferred_element_type=F32)


def _glu_kernel(x_ref, wo_ref, wg_ref, o_ref):
    xb = x_ref[...].astype(BF16)
    a = jnp.dot(xb, wo_ref[...], preferred_element_type=F32)
    g = jnp.dot(xb, wg_ref[...], preferred_element_type=F32)
    o_ref[...] = a * jax.nn.sigmoid(g)


def _mm_call(body, x, ws, name):
    m, k = x.shape
    n = ws[0].shape[1]
    tm = min(MM_TM, m)
    tn = min(MM_TN, n)
    return pl.pallas_call(
        body,
        grid=(m // tm, n // tn),
        in_specs=[pl.BlockSpec((tm, k), lambda i, j: (i, 0))]
        + [pl.BlockSpec((k, tn), lambda i, j: (0, j)) for _ in ws],
        out_specs=pl.BlockSpec((tm, tn), lambda i, j: (i, j)),
        out_shape=jax.ShapeDtypeStruct((m, n), F32),
        compiler_params=pltpu.CompilerParams(vmem_limit_bytes=VMEM_LIMIT),
        name=name,
    )(x, *ws)


def matmul_pallas(x, w):
    n = w.shape[1]
    npad = -n % MM_TN if n > MM_TN else -n % LANES
    wb = jnp.pad(w.astype(BF16), ((0, 0), (0, npad)))
    out = _mm_call(_mm_kernel, x, [wb], "dense_proj")
    return out[:, :n] if npad else out


def glu_pallas(x, w_o, w_g):
    return _mm_call(_glu_kernel, x, [w_o.astype(BF16), w_g.astype(BF16)], "glu_proj")


ATT_TQ = 512
ATT_TK = 2816


def _attn_kernel(q_ref, kt_ref, v_ref, o_ref, *, tk):
    q = q_ref[0, 0]
    tq = q.shape[0]
    nk = kt_ref.shape[-1]
    m = jnp.full((tq, 1), NEG, F32)
    l = jnp.zeros((tq, 1), F32)
    acc = jnp.zeros((tq, v_ref.shape[-1]), F32)
    for c in range(nk // tk):
        s = jnp.dot(q, kt_ref[0, 0, :, c * tk:(c + 1) * tk], preferred_element_type=F32)
        m_new = jnp.maximum(m, jnp.max(s, axis=-1, keepdims=True))
        alpha = jnp.exp(m - m_new)
        p = jnp.exp(s - m_new)
        l = alpha * l + jnp.sum(p, axis=-1, keepdims=True)
        acc = alpha * acc + jnp.dot(p.astype(BF16), v_ref[0, 0, c * tk:(c + 1) * tk, :],
                                    preferred_element_type=F32)
        m = m_new
    o_ref[0] = acc / l


def mla_attention_pallas(qn, qr, kn, kr, v):
    bn, lq, nh, _ = qn.shape
    sk = kn.shape[1]
    dv = v.shape[-1]
    q = (jnp.concatenate([qn, qr], axis=-1) * MLA_SCALE).astype(BF16).transpose(0, 2, 1, 3)
    kcat = jnp.concatenate([kn, jnp.broadcast_to(kr[:, :, None, :], (bn, sk, nh, kr.shape[-1]))], axis=-1)
    kt = kcat.astype(BF16).transpose(0, 2, 3, 1)
    vv = v.astype(BF16).transpose(0, 2, 1, 3)
    dq = q.shape[-1]
    tq = min(ATT_TQ, lq)
    tk = ATT_TK if sk % ATT_TK == 0 else sk
    return pl.pallas_call(
        functools.partial(_attn_kernel, tk=tk),
        grid=(bn, nh, lq // tq),
        in_specs=[pl.BlockSpec((1, 1, tq, dq), lambda b, h, i: (b, h, i, 0)),
                  pl.BlockSpec((1, 1, dq, sk), lambda b, h, i: (b, h, 0, 0)),
                  pl.BlockSpec((1, 1, sk, dv), lambda b, h, i: (b, h, 0, 0))],
        out_specs=pl.BlockSpec((1, tq, dv), lambda b, h, i: (b, i, h)),
        out_shape=jax.ShapeDtypeStruct((bn, lq, nh * dv), F32),
        compiler_params=pltpu.CompilerParams(vmem_limit_bytes=VMEM_LIMIT),
        name="mla_attention",
    )(q, kt, vv)


def layer_norm(x, g, b):
    xf = x.astype(F32)
    xc = xf - jnp.mean(xf, axis=-1, keepdims=True)
    var = jnp.mean(xc * xc, axis=-1, keepdims=True)
    return (xc * lax.rsqrt(var + LN_EPS) * g + b).astype(x.dtype)


def rms_norm(x, g):
    xf = x.astype(F32)
    return (xf * lax.rsqrt(jnp.mean(xf * xf, axis=-1, keepdims=True) + RMS_EPS) * g).astype(x.dtype)


def axial_rope(L):
    rows = L // GRID_W
    row = jnp.repeat(jnp.arange(rows), GRID_W).astype(F32)
    col = jnp.tile(jnp.arange(GRID_W), rows).astype(F32)
    half = MLA_ROPE // 2
    inv = ROPE_BASE ** (-jnp.arange(0, half, 2, dtype=F32) / half)
    ar, ac = row[:, None] * inv, col[:, None] * inv
    ang = jnp.concatenate([ar, ar, ac, ac], axis=-1)
    return jnp.cos(ang), jnp.sin(ang)


def apply_axial_rope(x, cos, sin):
    xs = x.reshape(x.shape[:-1] + (2, 2, MLA_ROPE // 4))
    rot = jnp.concatenate([-xs[..., 1:, :], xs[..., :1, :]], axis=-2).reshape(x.shape)
    return (x * cos + rot * sin).astype(x.dtype)


def short_conv3(z, w, b):
    L = z.shape[1]
    zp = jnp.pad(z, ((0, 0), (1, 1), (0, 0)))
    return zp[:, :L] * w[0] + zp[:, 1:L + 1] * w[1] + zp[:, 2:] * w[2] + b


def hyena_pos_features(L):
    t = jnp.linspace(0.0, 1.0, L, dtype=F32)[:, None]
    w = (2.0 * math.pi / L) * jnp.arange(L, dtype=F32)[:, None]
    f = jnp.linspace(1e-4, HY_BANDS - 1, HY_BANDS, dtype=F32)[None, :]
    return jnp.concatenate([t, jnp.cos(f * w), -jnp.sin(f * w)], axis=-1)


def hyena_filter_spectrum(L, w1, b1, w2, b2, w3, b3, w4, freq):
    act = lambda a: jnp.sin(freq * a)
    hdn = act(hyena_pos_features(L) @ w1 + b1)
    hdn = act(hdn @ w2 + b2)
    hdn = act(hdn @ w3 + b3)
    deltas = jnp.abs(jnp.linspace(math.log(HY_TARGET) / HY_SLOW_DECAY, math.log(HY_TARGET) / HY_FAST_DECAY,
                                  HY_ORDER * HY_W, dtype=F32)).reshape(HY_ORDER, HY_W)
    t = jnp.linspace(0.0, 1.0, L, dtype=F32)[:, None, None]
    w4d = w4.reshape(w4.shape[0], 2, HY_ORDER * HY_W)
    hf = (hdn @ w4d[:, 0]).astype(F32).reshape(L, HY_ORDER, HY_W) * jnp.exp(-t * deltas)
    hb = (hdn[::-1] @ w4d[:, 1]).astype(F32).reshape(L, HY_ORDER, HY_W) * jnp.exp(-t[::-1] * deltas)
    norm = jnp.sum(jnp.abs(hf), axis=0) + jnp.sum(jnp.abs(hb), axis=0)
    return jnp.concatenate([hf / norm, jnp.zeros((1, HY_ORDER, HY_W), F32), hb[:L - 1] / norm], axis=0)


def fft_long_conv(u, k_f):
    L = u.shape[1]
    u_f = jnp.fft.rfft(u.astype(F32), n=2 * L, axis=1)
    return jnp.fft.irfft(u_f * k_f[None], n=2 * L, axis=1)[:, :L]


FFT_R = 128
FFT_CB = 16


def _dft_constants():
    r = FFT_R
    n = r * r
    idx = np.arange(r)
    f = np.exp(-2j * np.pi * np.outer(idx, idx) / r)
    tw = np.exp(-2j * np.pi * np.outer(idx, idx) / n)
    blk = lambda w: np.block([[w.real, w.imag], [-w.imag, w.real]])
    mats = {
        'w1': blk(f[:r // 2, :]),
        'w1f': np.concatenate([f.real, f.imag], 1),
        'w2': blk(f),
        'w2c': blk(np.conj(f)),
        'w1c': blk(np.conj(f)[:, :r // 2]) / n,
    }
    out = {}
    for name, w in mats.items():
        w32 = jnp.asarray(w, F32)
        out[name] = _split_bf16(w32)
    out['twa'] = jnp.asarray(np.concatenate([tw.real, tw.real], 1), F32)
    out['twb'] = jnp.asarray(np.concatenate([-tw.imag, tw.imag], 1), F32)
    out['twbc'] = jnp.asarray(np.concatenate([tw.imag, -tw.imag], 1), F32)
    return out


def _cmm(x, wh_ref, wl_ref):
    xh, xl = _split_bf16(x)
    return _dot3(xh, xl, wh_ref[...], wl_ref[...], _NN)


def _swap_halves(x):
    h = x.shape[-1] // 2
    return jnp.concatenate([x[..., h:], x[..., :h]], axis=-1)


def _cmul(x, a, b):
    return x * a + _swap_halves(x) * b


def _tile_transpose(x):
    h = x.shape[-1] // 2
    return jnp.concatenate([jnp.swapaxes(x[..., :h], 1, 2), jnp.swapaxes(x[..., h:], 1, 2)], axis=-1)


def _fft_forward(z2d, cb, w1h, w1l, w2h, w2l, twa, twb):
    r = FFT_R
    x = _cmm(z2d, w1h, w1l).reshape(cb, r, 2 * r)
    x = _tile_transpose(_cmul(x, twa[...], twb[...]))
    return _cmm(x.reshape(cb * r, 2 * r), w2h, w2l).reshape(cb, r, 2 * r)


def _hy_filter_kernel(k_ref, w1h, w1l, w2h, w2l, twa, twb, o_ref):
    cb = k_ref.shape[0]
    o_ref[...] = _fft_forward(k_ref[...].reshape(cb * FFT_R, FFT_R), cb, w1h, w1l, w2h, w2l, twa, twb)


def _hy_conv_kernel(z_ref, k_ref, w1h, w1l, w2h, w2l, w2ch, w2cl, w1ch, w1cl, twa, twb, twbc, o_ref):
    cb = z_ref.shape[0]
    r = FFT_R
    x = _fft_forward(z_ref[...].reshape(cb * r, r), cb, w1h, w1l, w2h, w2l, twa, twb)
    k = k_ref[...]
    kr, ki = k[..., :r], k[..., r:]
    x = _cmul(x, jnp.concatenate([kr, kr], axis=-1), jnp.concatenate([-ki, ki], axis=-1))
    x = _cmm(x.reshape(cb * r, 2 * r), w2ch, w2cl).reshape(cb, r, 2 * r)
    x = _tile_transpose(_cmul(x, twa[...], twbc[...]))
    o_ref[...] = _cmm(x.reshape(cb * r, 2 * r), w1ch, w1cl).reshape(cb, r, r)


def _const_spec(a):
    return pl.BlockSpec(a.shape, lambda i: (0,) * a.ndim)


def hyena_filter_fft(k_time, consts):
    r = FFT_R
    c = k_time.shape[1]
    kt = k_time.reshape(r, r, c).transpose(2, 1, 0)
    ws = [*consts['w1f'], *consts['w2'], consts['twa'], consts['twb']]
    return pl.pallas_call(
        _hy_filter_kernel,
        grid=(c // FFT_CB,),
        in_specs=[pl.BlockSpec((FFT_CB, r, r), lambda i: (i, 0, 0))] + [_const_spec(w) for w in ws],
        out_specs=pl.BlockSpec((FFT_CB, r, 2 * r), lambda i: (i, 0, 0)),
        out_shape=jax.ShapeDtypeStruct((c, r, 2 * r), F32),
        compiler_params=pltpu.CompilerParams(vmem_limit_bytes=VMEM_LIMIT),
        name="hyena_filter_fft",
    )(kt, *ws)


def hyena_long_conv(u, k_spec, consts):
    r = FFT_R
    bn, ll, c = u.shape
    assert bn == 2 and 2 * ll == r * r
    z = u.reshape(2, r // 2, r, c).transpose(3, 2, 0, 1).reshape(c, r, r)
    ws = [*consts['w1'], *consts['w2'], *consts['w2c'], *consts['w1c'],
          consts['twa'], consts['twb'], consts['twbc']]
    y = pl.pallas_call(
        _hy_conv_kernel,
        grid=(c // FFT_CB,),
        in_specs=[pl.BlockSpec((FFT_CB, r, r), lambda i: (i, 0, 0)),
                  pl.BlockSpec((FFT_CB, r, 2 * r), lambda i: (i, 0, 0))] + [_const_spec(w) for w in ws],
        out_specs=pl.BlockSpec((FFT_CB, r, r), lambda i: (i, 0, 0)),
        out_shape=jax.ShapeDtypeStruct((c, r, r), F32),
        compiler_params=pltpu.CompilerParams(vmem_limit_bytes=VMEM_LIMIT),
        name="hyena_long_conv",
    )(z, k_spec, *ws)
    return y.reshape(c, r, 2, r // 2).transpose(2, 3, 1, 0).reshape(2, ll, c)


def hyena_mixer(z, conv_w, conv_b, w1, b1, w2, b2, w3, b3, w4, freq, bias):
    L = z.shape[1]
    zc = short_conv3(z, conv_w, conv_b)
    x1, x2, v = jnp.split(zc, 3, axis=-1)
    k_t = hyena_filter_spectrum(L, w1, b1, w2, b2, w3, b3, w4, freq)
    y = v.astype(F32)
    use_mxu_fft = z.shape[0] == 2 and 2 * L == FFT_R * FFT_R
    if use_mxu_fft:
        consts = _dft_constants()
    else:
        k_f = jnp.fft.rfft(k_t, axis=0)
    for o, gate in enumerate((x1, x2)):
        if use_mxu_fft:
            conv = hyena_long_conv(y, hyena_filter_fft(k_t[:, o], consts), consts)
        else:
            conv = fft_long_conv(y, k_f[:, o])
        y = gate * (conv + y * bias[o])
    return y.astype(z.dtype)


def _mm3(x, w):
    b, l, k = x.shape
    return matmul_pallas(x.reshape(b * l, k), w).reshape(b, l, w.shape[1])


def mla_heads(z, q_norm, w_uq, kv_norm, w_ukv):
    Bn, L, _ = z.shape
    q = _mm3(rms_norm(z[..., O_HY:O_Q], q_norm), w_uq).reshape(Bn, L, MLA_HEADS, MLA_NOPE + MLA_ROPE)
    kv = _mm3(rms_norm(z[..., O_Q:O_KV], kv_norm), w_ukv).reshape(Bn, L, MLA_HEADS, MLA_NOPE + MLA_V)
    return q[..., :MLA_NOPE], q[..., MLA_NOPE:], kv[..., :MLA_NOPE], z[..., O_KV:], kv[..., MLA_NOPE:]


def even_mixer(in_ctx, in_lat, cos, sin, need_ctx, w_in, hy, mla, w_o):
    z_ctx = _mm3(in_ctx, w_in)
    z_lat = _mm3(in_lat, w_in)
    cqn, cqr, ckn, ckr, cv = mla_heads(z_ctx, *mla)
    lqn, lqr, lkn, lkr, lv = mla_heads(z_lat, *mla)
    lqr = apply_axial_rope(lqr, cos[:, None, :], sin[:, None, :])
    lkr = apply_axial_rope(lkr, cos, sin)
    kn = jnp.concatenate([ckn, lkn], axis=1)
    kr = jnp.concatenate([ckr, lkr], axis=1)
    vv = jnp.concatenate([cv, lv], axis=1)
    att = mla_attention_pallas(lqn, lqr, kn, kr, vv)
    y_lat = _mm3(jnp.concatenate([hyena_mixer(z_lat[..., :O_HY], *hy), att], axis=-1), w_o)
    y_ctx = None
    if need_ctx:
        att_c = mla_attention_pallas(cqn, cqr, ckn, ckr, cv)
        y_ctx = _mm3(jnp.concatenate([hyena_mixer(z_ctx[..., :O_HY], *hy), att_c], axis=-1), w_o)
    return y_ctx, y_lat


def _scan_op(e1, e2):
    a1, b1 = e1
    a2, b2 = e2
    return a1 * a2, a2 * b1 + b2


def linear_scan(a_bar, bu, s0):
    bu = bu.at[:, 0].add(a_bar * s0)
    a = jnp.broadcast_to(a_bar, bu.shape)
    _, s = lax.associative_scan(_scan_op, (a, bu), axis=1)
    return s


def _rev(s, d):
    return s[:, ::-1] if d == 1 else s


def s5_mixer(in_ctx, in_lat, need_ctx, lam_re, lam_im, log_dt, b_re, b_im, c_re, c_im, d_skip, w_o, w_g):
    s_ctx, s_lat = s5_scan(in_ctx, in_lat, lam_re, lam_im, log_dt, b_re, b_im, c_re, c_im)

    def finish(ys, h):
        y = ys + d_skip * h.astype(F32)
        y = jax.nn.gelu(y).astype(h.dtype)
        b, l, dm = y.shape
        return glu_pallas(y.reshape(b * l, dm), w_o, w_g).reshape(b, l, dm)

    y_lat = finish(s_lat, in_lat)
    y_ctx = finish(s_ctx, in_ctx) if need_ctx else None
    return y_ctx, y_lat


def peer_ffn(h, w_q, keys, u_tab, v_tab):
    Bn, L, D = h.shape
    hb = h.reshape(Bn * L // PEER_BLOCK, PEER_BLOCK, D)

    def block_fn(xb):
        q = (xb @ w_q).reshape(PEER_BLOCK, PEER_HEADS, 2, PEER_DK // 2)
        s = jnp.einsum('thsk,hsnk->thsn', q, keys).astype(F32)
        top_s, top_i = lax.top_k(s, PEER_TOPK)
        cand = top_s[:, :, 0, :, None] + top_s[:, :, 1, None, :]
        cand_s, cand_j = lax.top_k(cand.reshape(PEER_BLOCK, PEER_HEADS, PEER_TOPK * PEER_TOPK), PEER_TOPK)
        idx = (jnp.take_along_axis(top_i[:, :, 0], cand_j // PEER_TOPK, axis=-1) * PEER_NK
               + jnp.take_along_axis(top_i[:, :, 1], cand_j % PEER_TOPK, axis=-1))
        g = jax.nn.softmax(cand_s, axis=-1)
        act = jax.nn.gelu(jnp.einsum('thkd,td->thk', u_tab[idx], xb).astype(F32))
        return jnp.einsum('thk,thkd->td', (g * act).astype(xb.dtype), v_tab[idx])

    return lax.map(block_fn, hb).reshape(Bn, L, D)


def kernel(x, c, ctx, c_ctx, mod_w, mod_b, ln_mix_g, ln_mix_b, ln_ffn_g, ln_ffn_b,
           ev_w_in, ev_conv_w, ev_conv_b, hy_w1, hy_b1, hy_w2, hy_b2, hy_w3, hy_b3, hy_w4,
           hy_freq, hy_bias, mla_q_norm, mla_w_uq, mla_kv_norm, mla_w_ukv, ev_w_o,
           s5_lam_re, s5_lam_im, s5_log_dt, s5_b_re, s5_b_im, s5_c_re, s5_c_im, s5_d,
           od_w_o, od_w_g, peer_w_q, peer_keys, peer_u, peer_v):
    cos, sin = axial_rope(x.shape[1])
    act_lat = jax.nn.silu(c)
    act_ctx = jax.nn.silu(c_ctx)
    h_lat, h_ctx = x, ctx
    for layer in range(DEPTH):
        need_ctx = layer < DEPTH - 1
        m_lat = jnp.split((act_lat @ mod_w[layer] + mod_b[layer])[:, None, :], 6, axis=-1)
        m_ctx = jnp.split((act_ctx @ mod_w[layer] + mod_b[layer])[None, None, :], 6, axis=-1)
        in_lat = h_lat * (1.0 + m_lat[1]) + m_lat[0]
        in_ctx = h_ctx * (1.0 + m_ctx[1]) + m_ctx[0]
        i = layer // 2
        if layer % 2 == 0:
            hy = (ev_conv_w[i], ev_conv_b[i], hy_w1[i], hy_b1[i], hy_w2[i], hy_b2[i], hy_w3[i], hy_b3[i],
                  hy_w4[i], hy_freq[i], hy_bias[i])
            mla = (mla_q_norm[i], mla_w_uq[i], mla_kv_norm[i], mla_w_ukv[i])
            y_ctx, y_lat = even_mixer(in_ctx, in_lat, cos, sin, need_ctx, ev_w_in[i], hy, mla, ev_w_o[i])
        else:
            y_ctx, y_lat = s5_mixer(in_ctx, in_lat, need_ctx, s5_lam_re[i], s5_lam_im[i], s5_log_dt[i],
                                    s5_b_re[i], s5_b_im[i], s5_c_re[i], s5_c_im[i], s5_d[i],
                                    od_w_o[i], od_w_g[i])
        h_lat = layer_norm(DN_ALPHA * h_lat + m_lat[2] * y_lat, ln_mix_g[layer], ln_mix_b[layer])
        bn, ll, dm = h_lat.shape
        xin = (h_lat * (1.0 + m_lat[4]) + m_lat[3]).reshape(bn * ll, dm)
        if need_ctx:
            h_ctx = layer_norm(DN_ALPHA * h_ctx + m_ctx[2] * y_ctx, ln_mix_g[layer], ln_mix_b[layer])
            xin_ctx = (h_ctx * (1.0 + m_ctx[4]) + m_ctx[3]).reshape(-1, dm)
            xin = jnp.concatenate([xin, xin_ctx], axis=0)
        f_all = peer_ffn_pallas(xin, peer_w_q[layer], peer_keys[layer],
                                peer_u[layer].astype(BF16), peer_v[layer].T.astype(BF16))
        f_lat = f_all[:bn * ll].reshape(bn, ll, dm)
        pre = DN_ALPHA * h_lat + m_lat[5] * f_lat
        if layer == DEPTH - 1:
            h_lat = layer_norm_pallas(pre, ln_ffn_g[layer], ln_ffn_b[layer])
        else:
            h_lat = layer_norm(pre, ln_ffn_g[layer], ln_ffn_b[layer])
        if need_ctx:
            f_ctx = f_all[bn * ll:].reshape(h_ctx.shape)
            h_ctx = layer_norm(DN_ALPHA * h_ctx + m_ctx[5] * f_ctx, ln_ffn_g[layer], ln_ffn_b[layer])
    return h_lat
```

```python
import math
import functools
import numpy as np
import jax
import jax.numpy as jnp
from jax import lax
from jax.experimental import pallas as pl
from jax.experimental.pallas import tpu as pltpu

D_MODEL = 2048
BATCH = 2
SEQ = 8192
DEPTH = 4

GRID_W = 64
CTX_LEN = 256

HY_W = 1024
HY_ORDER = 2
HY_EMB = 33
HY_BANDS = (HY_EMB - 1) // 2
HY_HID = 64
HY_FAST_DECAY = 0.3
HY_SLOW_DECAY = 1.5
HY_TARGET = 1e-2

MLA_HEADS = 8
MLA_NOPE = 128
MLA_ROPE = 64
MLA_V = 128
MLA_Q_RANK = 512
MLA_KV_RANK = 256
MLA_SCALE = (MLA_NOPE + MLA_ROPE) ** -0.5
Q_BLOCK = 128
ROPE_BASE = 10000.0

O_HY = 3 * HY_W
O_Q = O_HY + MLA_Q_RANK
O_KV = O_Q + MLA_KV_RANK
IN_W = O_KV + MLA_ROPE

S5_GC = 16
S5_NG = D_MODEL // S5_GC
S5_P = 64
S5_GB = 16
S5_NBLK = S5_NG // S5_GB

PEER_HEADS = 8
PEER_NK = 128
PEER_N = PEER_NK * PEER_NK
PEER_DK = 256
PEER_TOPK = 16
PEER_BLOCK = 128

N_EVEN = (DEPTH + 1) // 2
N_ODD = DEPTH // 2
DN_ALPHA = (2.0 * DEPTH) ** 0.25
DN_BETA = (8.0 * DEPTH) ** -0.25
LN_EPS = 1e-5
RMS_EPS = 1e-6
F32 = jnp.float32


def _ln_kernel(x_ref, g_ref, b_ref, o_ref):
    xf = x_ref[...]
    xc = xf - jnp.mean(xf, axis=-1, keepdims=True)
    var = jnp.mean(xc * xc, axis=-1, keepdims=True)
    o_ref[...] = xc * lax.rsqrt(var + LN_EPS) * g_ref[...] + b_ref[...]


def layer_norm_pallas(x, g, b):
    shp = x.shape
    x2 = x.reshape(-1, shp[-1])
    n, d = x2.shape
    tb = 256
    out = pl.pallas_call(
        _ln_kernel,
        grid=(n // tb,),
        in_specs=[pl.BlockSpec((tb, d), lambda i: (i, 0)),
                  pl.BlockSpec((1, d), lambda i: (0, 0)),
                  pl.BlockSpec((1, d), lambda i: (0, 0))],
        out_specs=pl.BlockSpec((tb, d), lambda i: (i, 0)),
        out_shape=jax.ShapeDtypeStruct((n, d), F32),
        name="layer_norm",
    )(x2, g.reshape(1, d), b.reshape(1, d))
    return out.reshape(shp)


BF16 = jnp.bfloat16
HI = lax.Precision.HIGHEST
VMEM_LIMIT = 56 * 1024 * 1024

S5_T = 16
S5_CB = 8


def _s5_operators(lam_re, lam_im, log_dt, b_re, b_im, c_re, c_im):
    T = S5_T
    ng, p, gc = b_re.shape[1:]
    dt = jnp.exp(log_dt)[..., None]
    zr, zi = lam_re * dt, lam_im * dt
    k = jnp.arange(T + 1, dtype=F32)[:, None, None, None]
    mag = jnp.exp(zr[None] * k)
    pr, pi = mag * jnp.cos(zi[None] * k), mag * jnp.sin(zi[None] * k)
    ar, ai = pr[1], pi[1]
    den = lam_re * lam_re + lam_im * lam_im
    cr = ((ar - 1.0) * lam_re + ai * lam_im) / den
    ci = (ai * lam_re - (ar - 1.0) * lam_im) / den
    bbr = cr[..., None] * b_re - ci[..., None] * b_im
    bbi = cr[..., None] * b_im + ci[..., None] * b_re
    car = c_re[None] * pr[:, :, :, None, :] - c_im[None] * pi[:, :, :, None, :]
    cai = c_re[None] * pi[:, :, :, None, :] + c_im[None] * pr[:, :, :, None, :]
    kk = (jnp.einsum('tdgcp,dgpe->tdgce', car[:T], bbr, precision=HI)
          - jnp.einsum('tdgcp,dgpe->tdgce', cai[:T], bbi, precision=HI))
    kf, kb = kk[:, 0], kk[:, 1]
    kfull = jnp.concatenate([kb[1:][::-1], (kf[0] + kb[0])[None], kf[1:]], axis=0)
    jj = jnp.arange(T)
    idx = jj[None, :] - jj[:, None] + (T - 1)
    m = kfull[idx]
    m = m.transpose(2, 0, 4, 1, 3).reshape(ng, T * gc, T * gc)
    def e_w(d, pw_idx):
        er = pr[pw_idx, d][..., None] * bbr[d][None] - pi[pw_idx, d][..., None] * bbi[d][None]
        ei = pr[pw_idx, d][..., None] * bbi[d][None] + pi[pw_idx, d][..., None] * bbr[d][None]
        w = jnp.concatenate([er, ei], axis=2)
        return w.transpose(1, 0, 3, 2).reshape(ng, T * gc, 2 * p)
    w_e = jnp.concatenate([e_w(0, T - 1 - jj), e_w(1, jj)], axis=-1)
    def s_w(d, pw_idx):
        w = jnp.concatenate([car[pw_idx, d], -cai[pw_idx, d]], axis=-1)
        return w.transpose(1, 3, 0, 2).reshape(ng, 2 * p, T * gc)
    w_y = jnp.concatenate([m, s_w(0, jj + 1), s_w(1, T - jj)], axis=1)
    a1 = jnp.concatenate([pr[T], pr[T]], axis=-1)
    a2 = jnp.concatenate([-pi[T], pi[T]], axis=-1)
    return w_e, w_y, a1, a2


def _s5_lane_group():
    return lax.broadcasted_iota(jnp.int32, (1, LANES), 1) // S5_GC


def _s5_row_block(n):
    return 24 if n % 24 == 0 else 8


def _s5_group_rows(x_ref, xg_ref):
    nrow, T, _ = x_ref.shape
    per = LANES // S5_GC
    grp = _s5_lane_group()
    rb = _s5_row_block(nrow)

    def body(i, carry):
        rows = pl.ds(pl.multiple_of(i * rb, 8), rb)
        for hf in range(T // per):
            accs = [None] * per
            for j in range(per):
                xt = x_ref[rows, hf * per + j, :]
                for g in range(per):
                    sh = ((j - g) * S5_GC) % LANES
                    r = pltpu.roll(xt, sh, 1) if sh else xt
                    accs[g] = r if accs[g] is None else jnp.where(grp == j, r, accs[g])
            for g in range(per):
                xg_ref[g, rows, hf * LANES:(hf + 1) * LANES] = accs[g]
        return carry

    lax.fori_loop(0, nrow // rb, body, 0)


def _s5_scatter_rows(yg_scr, o_ref):
    nrow, T, _ = o_ref.shape
    per = LANES // S5_GC
    grp = _s5_lane_group()
    rb = _s5_row_block(nrow)

    def body(i, carry):
        rows = pl.ds(pl.multiple_of(i * rb, 8), rb)
        for t in range(T):
            hf, j = divmod(t, per)
            acc = None
            for g in range(per):
                piece = yg_scr[g, rows, hf * LANES:(hf + 1) * LANES]
                sh = ((g - j) * S5_GC) % LANES
                r = pltpu.roll(piece, sh, 1) if sh else piece
                acc = r if acc is None else jnp.where(grp == g, r, acc)
            o_ref[rows, t, :] = acc
        return carry

    lax.fori_loop(0, nrow // rb, body, 0)


def _s5_end_state_kernel(x_ref, w_ref, ef_ref, eb_ref, xg_ref):
    _s5_group_rows(x_ref, xg_ref)
    half = w_ref.shape[-1] // 2

    def body(g, carry):
        e = jnp.dot(xg_ref[g], w_ref[g], precision=HI, preferred_element_type=F32)
        cols = pl.ds(pl.multiple_of(g * half, LANES), half)
        ef_ref[:, cols] = e[:, :half]
        eb_ref[:, cols] = e[:, half:]
        return carry

    lax.fori_loop(0, xg_ref.shape[0], body, 0)


def _s5_carry_kernel(ef_ref, eb_ref, a1_ref, a2_ref, s0f_ref, s0b_ref, sf_scr, sb_scr):
    @pl.when(pl.program_id(1) == 0)
    def _():
        sf_scr[...] = jnp.zeros_like(sf_scr)
        sb_scr[...] = jnp.zeros_like(sb_scr)

    half = sf_scr.shape[-1] // 2

    def step(s, e, d):
        return a1_ref[d] * s + a2_ref[d] * pltpu.roll(s, half, 1) + e

    for r in range(S5_CB):
        s = sf_scr[...]
        s0f_ref[r] = s
        sf_scr[...] = step(s, ef_ref[r], 0)
    for r in reversed(range(S5_CB)):
        s = sb_scr[...]
        s0b_ref[r] = s
        sb_scr[...] = step(s, eb_ref[r], 1)


def _s5_output_kernel(xg_ref, s0f_ref, s0b_ref, w_ref, y_ref, xg_scr):
    kx = xg_scr.shape[-1]
    ks = s0f_ref.shape[-1] // xg_scr.shape[0]

    def body(g, carry):
        cols = pl.ds(pl.multiple_of(g * ks, LANES), ks)
        y = jnp.dot(xg_ref[g], w_ref[g, :kx, :], precision=HI, preferred_element_type=F32)
        y = y + jnp.dot(s0f_ref[:, cols], w_ref[g, kx:kx + ks, :], precision=HI, preferred_element_type=F32)
        y = y + jnp.dot(s0b_ref[:, cols], w_ref[g, kx + ks:, :], precision=HI, preferred_element_type=F32)
        xg_scr[g] = y
        return carry

    lax.fori_loop(0, xg_scr.shape[0], body, 0)
    _s5_scatter_rows(xg_scr, y_ref)


def s5_scan(in_ctx, in_lat, lam_re, lam_im, log_dt, b_re, b_im, c_re, c_im):
    T = S5_T
    bn, lc, d = in_ctx.shape
    ll = in_lat.shape[1]
    ng, p, gc = b_re.shape[1:]
    w_e, w_y, a1, a2 = _s5_operators(lam_re, lam_im, log_dt, b_re, b_im, c_re, c_im)
    u = jnp.concatenate([in_ctx, in_lat], axis=1)
    nch = (lc + ll) // T
    rows = bn * nch
    kx, ks = T * gc, 2 * p
    per = LANES // gc
    ncb = ng // per
    x3 = u.reshape(rows, T, d)

    grouped = pl.BlockSpec((per, nch, kx), lambda c, r: (c, r, 0))
    ef, eb, xg = pl.pallas_call(
        _s5_end_state_kernel,
        grid=(ncb, bn),
        in_specs=[pl.BlockSpec((nch, T, LANES), lambda c, r: (r, 0, c)),
                  pl.BlockSpec((per, kx, 2 * ks), lambda c, r: (c, 0, 0))],
        out_specs=[pl.BlockSpec((nch, per * ks), lambda c, r: (r, c)),
                   pl.BlockSpec((nch, per * ks), lambda c, r: (r, c)),
                   grouped],
        out_shape=[jax.ShapeDtypeStruct((rows, ng * ks), F32)] * 2
        + [jax.ShapeDtypeStruct((ng, rows, kx), F32)],
        compiler_params=pltpu.CompilerParams(vmem_limit_bytes=VMEM_LIMIT),
        name="s5_end_state",
    )(x3, w_e)

    nblk = nch // S5_CB
    cblk = (lc // T) // S5_CB
    fwd_map = lambda b, j: (b * nblk + j, 0, 0)
    bwd_map = lambda b, j: (b * nblk + jnp.where(j < cblk, cblk - 1 - j, nblk + cblk - 1 - j), 0, 0)
    s0f, s0b = pl.pallas_call(
        _s5_carry_kernel,
        grid=(bn, nblk),
        in_specs=[pl.BlockSpec((S5_CB, ng, ks), fwd_map),
                  pl.BlockSpec((S5_CB, ng, ks), bwd_map),
                  pl.BlockSpec((2, ng, ks), lambda b, j: (0, 0, 0)),
                  pl.BlockSpec((2, ng, ks), lambda b, j: (0, 0, 0))],
        out_specs=[pl.BlockSpec((S5_CB, ng, ks), fwd_map),
                   pl.BlockSpec((S5_CB, ng, ks), bwd_map)],
        out_shape=[jax.ShapeDtypeStruct((rows, ng, ks), F32)] * 2,
        scratch_shapes=[pltpu.VMEM((ng, ks), F32), pltpu.VMEM((ng, ks), F32)],
        name="s5_carry",
    )(ef.reshape(rows, ng, ks), eb.reshape(rows, ng, ks), a1, a2)

    y3 = pl.pallas_call(
        _s5_output_kernel,
        grid=(ncb, bn),
        in_specs=[grouped,
                  pl.BlockSpec((nch, per * ks), lambda c, r: (r, c)),
                  pl.BlockSpec((nch, per * ks), lambda c, r: (r, c)),
                  pl.BlockSpec((per, kx + 2 * ks, kx), lambda c, r: (c, 0, 0))],
        out_specs=pl.BlockSpec((nch, T, LANES), lambda c, r: (r, 0, c)),
        out_shape=jax.ShapeDtypeStruct((rows, T, d), F32),
        scratch_shapes=[pltpu.VMEM((per, nch, kx), F32)],
        compiler_params=pltpu.CompilerParams(vmem_limit_bytes=VMEM_LIMIT),
        name="s5_output",
    )(xg, s0f.reshape(rows, ng * ks), s0b.reshape(rows, ng * ks), w_y)
    y = y3.reshape(bn, lc + ll, d)
    return y[:, :lc], y[:, lc:]


PEER_TS = 256
PEER_TB = 512
PEER_EB = 1024
LANES = 128
NEG = -3.0e38
_NN = (((1,), (0,)), ((), ()))
_NT = (((1,), (1,)), ((), ()))


def _split_bf16(a):
    hi = a.astype(BF16)
    return hi, (a - hi.astype(F32)).astype(BF16)


def _dot3(ah, al, bh, bl, dims):
    f = lambda a, b: lax.dot_general(a, b, dims, preferred_element_type=F32)
    return f(ah, bh) + f(al, bh) + f(ah, bl)


def _peer_cells():
    return PEER_TOPK + 8 * (PEER_TOPK // 2 - 1) + 8


def _peer_score_kernel(x_ref, wq_ref, kh_ref, kl_ref,
                       n1_ref, w1_ref, r2_ref, e2_ref,
                       st_scr, a1_scr, a2_scr, c_scr):
    nh, _, nk, dk = kh_ref.shape
    tb = x_ref.shape[0]
    q = jnp.dot(x_ref[...].astype(BF16), wq_ref[...], preferred_element_type=F32)
    qh, ql = _split_bf16(q)
    for h in range(nh):
        for s in range(2):
            c0 = (2 * h + s) * dk
            st_scr[2 * h + s] = _dot3(kh_ref[h, s], kl_ref[h, s],
                                      qh[:, c0:c0 + dk], ql[:, c0:c0 + dk], _NT)

    def top_values(s, out_scr):
        rank = jnp.full(s.shape, float(PEER_TOPK), F32)
        for r in range(PEER_TOPK):
            m = jnp.max(s, axis=0, keepdims=True)
            out_scr[r:r + 1, :] = m
            hit = s == m
            rank = jnp.where(hit, float(r), rank)
            s = jnp.where(hit, NEG, s)
        return rank

    nsub = tb // LANES
    half = PEER_TOPK // 2

    def body(it, carry):
        h = it // nsub
        lanes = pl.ds(pl.multiple_of((it % nsub) * LANES, LANES), LANES)
        s1 = st_scr[2 * h, :, lanes]
        s2 = st_scr[2 * h + 1, :, lanes]
        top_values(s1, a1_scr)
        r2 = top_values(s2, a2_scr)
        a1 = a1_scr[...]
        a2 = a2_scr[...]
        c_scr[0:PEER_TOPK, :] = a1[0:1] + a2
        for k in range(1, half):
            c_scr[PEER_TOPK + 8 * (k - 1):PEER_TOPK + 8 * k, :] = a1[k:k + 1] + a2[0:8]
        c_scr[PEER_TOPK + 8 * (half - 1):, :] = a1[half:] + a2[0:1]
        c = c_scr[...]
        m0 = a1[0:1] + a2[0:1]
        z = jnp.zeros_like(m0)
        tau = m0
        for r in range(PEER_TOPK):
            tau = jnp.max(c, axis=0, keepdims=True)
            z = z + jnp.exp(tau - m0)
            c = jnp.where(c == tau, NEG, c)
        n1 = jnp.zeros_like(s1)
        for l in range(PEER_TOPK):
            n1 = n1 + jnp.where(s1 + a2[l:l + 1] >= tau, 1.0, 0.0)
        n1_ref[h, :, lanes] = n1
        w1_ref[h, :, lanes] = jnp.exp(s1 - a1[0:1]) / z
        r2_ref[h, :, lanes] = pltpu.bitcast(r2.astype(BF16), jnp.uint32)
        e2_ref[h, :, lanes] = pltpu.bitcast(jnp.exp(s2 - a2[0:1]).astype(BF16), jnp.uint32)
        return carry

    lax.fori_loop(0, nh * nsub, body, 0)


def _peer_expert_kernel(xt_ref, u_ref, vt_ref, n1_ref, w1_ref, r2_ref, e2_ref,
                        o_ref, acc_scr, a_scr, p_scr):
    e = pl.program_id(1)
    last = pl.num_programs(1) - 1
    slot = e % 2
    nh, nk, tb = r2_ref.shape[0], 2 * r2_ref.shape[1], r2_ref.shape[2]
    ni = a_scr.shape[0] // nk

    @pl.when(e == 0)
    def _():
        acc_scr[...] = jnp.zeros_like(acc_scr)
        p_scr[1] = jnp.zeros(p_scr.shape[1:], BF16)

    @pl.when(e < last)
    def _():
        acc_scr[...] += jnp.dot(vt_ref[...], p_scr[1 - slot], preferred_element_type=F32)
        a_scr[...] = jnp.dot(u_ref[...], xt_ref[...], preferred_element_type=F32)
        for ts in range(tb // LANES):
            lanes = slice(ts * LANES, (ts + 1) * LANES)
            for ii in range(ni):
                rows = slice(ii * nk, (ii + 1) * nk)
                g = jnp.zeros((nk, LANES), BF16)
                for h in range(nh):
                    n1row = jnp.broadcast_to(n1_ref[h, ii:ii + 1, lanes].astype(BF16), (nk, LANES))
                    w1row = jnp.broadcast_to(w1_ref[h, ii:ii + 1, lanes].astype(BF16), (nk, LANES))
                    r2 = pltpu.bitcast(r2_ref[h, :, lanes], BF16)
                    e2 = pltpu.bitcast(e2_ref[h, :, lanes], BF16)
                    g = g + jnp.where(r2 < n1row, e2, jnp.zeros((), BF16)) * w1row
                p_scr[slot, rows, lanes] = (g.astype(F32) * jax.nn.gelu(a_scr[rows, lanes])).astype(BF16)

    @pl.when(e == last)
    def _():
        acc = acc_scr[...] + jnp.dot(vt_ref[...], p_scr[1 - slot], preferred_element_type=F32)
        o_ref[...] = acc.T


def peer_ffn_pallas(xin, w_q, keys, u_bf, vt_bf):
    n, d = xin.shape
    nh, _, nk, dk = keys.shape
    ne = u_bf.shape[0]
    ts = min(PEER_TS, n)
    tb = min(PEER_TB, n)
    eb = min(PEER_EB, ne)
    wq = w_q.astype(BF16)
    kh, kl = _split_bf16(keys)
    feat32 = jax.ShapeDtypeStruct((nh, nk, n), F32)
    feat16 = jax.ShapeDtypeStruct((nh, nk // 2, n), jnp.uint32)
    pack_spec = pl.BlockSpec((nh, nk // 2, ts), lambda i: (0, 0, i))
    feat_spec = pl.BlockSpec((nh, nk, ts), lambda i: (0, 0, i))
    n1, w1, r2, e2 = pl.pallas_call(
        _peer_score_kernel,
        grid=(n // ts,),
        in_specs=[pl.BlockSpec((ts, d), lambda i: (i, 0)),
                  pl.BlockSpec(wq.shape, lambda i: (0, 0)),
                  pl.BlockSpec(kh.shape, lambda i: (0, 0, 0, 0)),
                  pl.BlockSpec(kl.shape, lambda i: (0, 0, 0, 0))],
        out_specs=[feat_spec, feat_spec, pack_spec, pack_spec],
        out_shape=[feat32, feat32, feat16, feat16],
        scratch_shapes=[pltpu.VMEM((2 * nh, nk, ts), F32),
                        pltpu.VMEM((PEER_TOPK, LANES), F32),
                        pltpu.VMEM((PEER_TOPK, LANES), F32),
                        pltpu.VMEM((_peer_cells(), LANES), F32)],
        compiler_params=pltpu.CompilerParams(vmem_limit_bytes=VMEM_LIMIT),
        name="peer_scores",
    )(xin, wq, kh, kl)

    ni = eb // nk
    nblk = ne // eb
    row_spec = pl.BlockSpec((nh, ni, tb), lambda t, e: (0, jnp.minimum(e, nblk - 1), t))
    col_spec = pl.BlockSpec((nh, nk // 2, tb), lambda t, e: (0, 0, t))
    return pl.pallas_call(
        _peer_expert_kernel,
        grid=(n // tb, nblk + 1),
        in_specs=[pl.BlockSpec((d, tb), lambda t, e: (0, t)),
                  pl.BlockSpec((eb, d), lambda t, e: (jnp.minimum(e, nblk - 1), 0)),
                  pl.BlockSpec((d, eb), lambda t, e: (0, jnp.maximum(e - 1, 0))),
                  row_spec, row_spec, col_spec, col_spec],
        out_specs=pl.BlockSpec((tb, d), lambda t, e: (t, 0)),
        out_shape=jax.ShapeDtypeStruct((n, d), F32),
        scratch_shapes=[pltpu.VMEM((d, tb), F32),
                        pltpu.VMEM((eb, tb), F32),
                        pltpu.VMEM((2, eb, tb), BF16)],
        compiler_params=pltpu.CompilerParams(vmem_limit_bytes=VMEM_LIMIT,
                                             dimension_semantics=("arbitrary", "arbitrary")),
        name="peer_experts",
    )(xin.astype(BF16).T, u_bf, vt_bf, n1, w1, r2, e2)


MM_TM = 512
MM_TN = 512


def _mm_kernel(x_ref, w_ref, o_ref):
    o_ref[...] = jnp.dot(x_ref[...].astype(BF16), w_ref[...], preferred_element_type=F32)


def _glu_kernel(x_ref, wo_ref, wg_ref, o_ref):
    xb = x_ref[...].astype(BF16)
    a = jnp.dot(xb, wo_ref[...], preferred_element_type=F32)
    g = jnp.dot(xb, wg_ref[...], preferred_element_type=F32)
    o_ref[...] = a * jax.nn.sigmoid(g)


def _mm_call(body, x, ws, name):
    m, k = x.shape
    n = ws[0].shape[1]
    tm = min(MM_TM, m)
    tn = min(MM_TN, n)
    return pl.pallas_call(
        body,
        grid=(m // tm, n // tn),
        in_specs=[pl.BlockSpec((tm, k), lambda i, j: (i, 0))]
        + [pl.BlockSpec((k, tn), lambda i, j: (0, j)) for _ in ws],
        out_specs=pl.BlockSpec((tm, tn), lambda i, j: (i, j)),
        out_shape=jax.ShapeDtypeStruct((m, n), F32),
        compiler_params=pltpu.CompilerParams(vmem_limit_bytes=VMEM_LIMIT),
        name=name,
    )(x, *ws)


def matmul_pallas(x, w):
    n = w.shape[1]
    npad = -n % MM_TN if n > MM_TN else -n % LANES
    wb = jnp.pad(w.astype(BF16), ((0, 0), (0, npad)))
    out = _mm_call(_mm_kernel, x, [wb], "dense_proj")
    return out[:, :n] if npad else out


def glu_pallas(x, w_o, w_g):
    return _mm_call(_glu_kernel, x, [w_o.astype(BF16), w_g.astype(BF16)], "glu_proj")


ATT_TQ = 512
ATT_TK = 2816


def _attn_kernel(q_ref, kt_ref, v_ref, o_ref, *, tk):
    q = q_ref[0, 0]
    tq = q.shape[0]
    nk = kt_ref.shape[-1]
    m = jnp.full((tq, 1), NEG, F32)
    l = jnp.zeros((tq, 1), F32)
    acc = jnp.zeros((tq, v_ref.shape[-1]), F32)
    for c in range(nk // tk):
        s = jnp.dot(q, kt_ref[0, 0, :, c * tk:(c + 1) * tk], preferred_element_type=F32)
        m_new = jnp.maximum(m, jnp.max(s, axis=-1, keepdims=True))
        alpha = jnp.exp(m - m_new)
        p = jnp.exp(s - m_new)
        l = alpha * l + jnp.sum(p, axis=-1, keepdims=True)
        acc = alpha * acc + jnp.dot(p.astype(BF16), v_ref[0, 0, c * tk:(c + 1) * tk, :],
                                    preferred_element_type=F32)
        m = m_new
    o_ref[0] = acc / l


def mla_attention_pallas(qn, qr, kn, kr, v):
    bn, lq, nh, _ = qn.shape
    sk = kn.shape[1]
    dv = v.shape[-1]
    q = (jnp.concatenate([qn, qr], axis=-1) * MLA_SCALE).astype(BF16).transpose(0, 2, 1, 3)
    kcat = jnp.concatenate([kn, jnp.broadcast_to(kr[:, :, None, :], (bn, sk, nh, kr.shape[-1]))], axis=-1)
    kt = kcat.astype(BF16).transpose(0, 2, 3, 1)
    vv = v.astype(BF16).transpose(0, 2, 1, 3)
    dq = q.shape[-1]
    tq = min(ATT_TQ, lq)
    tk = ATT_TK if sk % ATT_TK == 0 else sk
    return pl.pallas_call(
        functools.partial(_attn_kernel, tk=tk),
        grid=(bn, nh, lq // tq),
        in_specs=[pl.BlockSpec((1, 1, tq, dq), lambda b, h, i: (b, h, i, 0)),
                  pl.BlockSpec((1, 1, dq, sk), lambda b, h, i: (b, h, 0, 0)),
                  pl.BlockSpec((1, 1, sk, dv), lambda b, h, i: (b, h, 0, 0))],
        out_specs=pl.BlockSpec((1, tq, dv), lambda b, h, i: (b, i, h)),
        out_shape=jax.ShapeDtypeStruct((bn, lq, nh * dv), F32),
        compiler_params=pltpu.CompilerParams(vmem_limit_bytes=VMEM_LIMIT),
        name="mla_attention",
    )(q, kt, vv)


def layer_norm(x, g, b):
    xf = x.astype(F32)
    xc = xf - jnp.mean(xf, axis=-1, keepdims=True)
    var = jnp.mean(xc * xc, axis=-1, keepdims=True)
    return (xc * lax.rsqrt(var + LN_EPS) * g + b).astype(x.dtype)


def rms_norm(x, g):
    xf = x.astype(F32)
    return (xf * lax.rsqrt(jnp.mean(xf * xf, axis=-1, keepdims=True) + RMS_EPS) * g).astype(x.dtype)


def axial_rope(L):
    rows = L // GRID_W
    row = jnp.repeat(jnp.arange(rows), GRID_W).astype(F32)
    col = jnp.tile(jnp.arange(GRID_W), rows).astype(F32)
    half = MLA_ROPE // 2
    inv = ROPE_BASE ** (-jnp.arange(0, half, 2, dtype=F32) / half)
    ar, ac = row[:, None] * inv, col[:, None] * inv
    ang = jnp.concatenate([ar, ar, ac, ac], axis=-1)
    return jnp.cos(ang), jnp.sin(ang)


def apply_axial_rope(x, cos, sin):
    xs = x.reshape(x.shape[:-1] + (2, 2, MLA_ROPE // 4))
    rot = jnp.concatenate([-xs[..., 1:, :], xs[..., :1, :]], axis=-2).reshape(x.shape)
    return (x * cos + rot * sin).astype(x.dtype)


def short_conv3(z, w, b):
    L = z.shape[1]
    zp = jnp.pad(z, ((0, 0), (1, 1), (0, 0)))
    return zp[:, :L] * w[0] + zp[:, 1:L + 1] * w[1] + zp[:, 2:] * w[2] + b


def hyena_pos_features(L):
    t = jnp.linspace(0.0, 1.0, L, dtype=F32)[:, None]
    w = (2.0 * math.pi / L) * jnp.arange(L, dtype=F32)[:, None]
    f = jnp.linspace(1e-4, HY_BANDS - 1, HY_BANDS, dtype=F32)[None, :]
    return jnp.concatenate([t, jnp.cos(f * w), -jnp.sin(f * w)], axis=-1)


def hyena_filter_spectrum(L, w1, b1, w2, b2, w3, b3, w4, freq):
    act = lambda a: jnp.sin(freq * a)
    hdn = act(hyena_pos_features(L) @ w1 + b1)
    hdn = act(hdn @ w2 + b2)
    hdn = act(hdn @ w3 + b3)
    deltas = jnp.abs(jnp.linspace(math.log(HY_TARGET) / HY_SLOW_DECAY, math.log(HY_TARGET) / HY_FAST_DECAY,
                                  HY_ORDER * HY_W, dtype=F32)).reshape(HY_ORDER, HY_W)
    t = jnp.linspace(0.0, 1.0, L, dtype=F32)[:, None, None]
    w4d = w4.reshape(w4.shape[0], 2, HY_ORDER * HY_W)
    hf = (hdn @ w4d[:, 0]).astype(F32).reshape(L, HY_ORDER, HY_W) * jnp.exp(-t * deltas)
    hb = (hdn[::-1] @ w4d[:, 1]).astype(F32).reshape(L, HY_ORDER, HY_W) * jnp.exp(-t[::-1] * deltas)
    norm = jnp.sum(jnp.abs(hf), axis=0) + jnp.sum(jnp.abs(hb), axis=0)
    return jnp.concatenate([hf / norm, jnp.zeros((1, HY_ORDER, HY_W), F32), hb[:L - 1] / norm], axis=0)


def fft_long_conv(u, k_f):
    L = u.shape[1]
    u_f = jnp.fft.rfft(u.astype(F32), n=2 * L, axis=1)
    return jnp.fft.irfft(u_f * k_f[None], n=2 * L, axis=1)[:, :L]


FFT_R = 128
FFT_CB = 16


def _dft_constants():
    r = FFT_R
    n = r * r
    idx = np.arange(r)
    f = np.exp(-2j * np.pi * np.outer(idx, idx) / r)
    tw = np.exp(-2j * np.pi * np.outer(idx, idx) / n)
    blk = lambda w: np.block([[w.real, w.imag], [-w.imag, w.real]])
    mats = {
        'w1': blk(f[:r // 2, :]),
        'w1f': np.concatenate([f.real, f.imag], 1),
        'w2': blk(f),
        'w2c': blk(np.conj(f)),
        'w1c': blk(np.conj(f)[:, :r // 2]) / n,
    }
    out = {}
    for name, w in mats.items():
        w32 = jnp.asarray(w, F32)
        out[name] = _split_bf16(w32)
    out['twa'] = jnp.asarray(np.concatenate([tw.real, tw.real], 1), F32)
    out['twb'] = jnp.asarray(np.concatenate([-tw.imag, tw.imag], 1), F32)
    out['twbc'] = jnp.asarray(np.concatenate([tw.imag, -tw.imag], 1), F32)
    return out


def _cmm(x, wh_ref, wl_ref):
    xh, xl = _split_bf16(x)
    return _dot3(xh, xl, wh_ref[...], wl_ref[...], _NN)


def _swap_halves(x):
    h = x.shape[-1] // 2
    return jnp.concatenate([x[..., h:], x[..., :h]], axis=-1)


def _cmul(x, a, b):
    return x * a + _swap_halves(x) * b


def _tile_transpose(x):
    h = x.shape[-1] // 2
    return jnp.concatenate([jnp.swapaxes(x[..., :h], 1, 2), jnp.swapaxes(x[..., h:], 1, 2)], axis=-1)


def _fft_forward(z2d, cb, w1h, w1l, w2h, w2l, twa, twb):
    r = FFT_R
    x = _cmm(z2d, w1h, w1l).reshape(cb, r, 2 * r)
    x = _tile_transpose(_cmul(x, twa[...], twb[...]))
    return _cmm(x.reshape(cb * r, 2 * r), w2h, w2l).reshape(cb, r, 2 * r)


def _hy_filter_kernel(k_ref, w1h, w1l, w2h, w2l, twa, twb, o_ref):
    cb = k_ref.shape[0]
    o_ref[...] = _fft_forward(k_ref[...].reshape(cb * FFT_R, FFT_R), cb, w1h, w1l, w2h, w2l, twa, twb)


def _hy_conv_kernel(z_ref, k_ref, w1h, w1l, w2h, w2l, w2ch, w2cl, w1ch, w1cl, twa, twb, twbc, o_ref):
    cb = z_ref.shape[0]
    r = FFT_R
    x = _fft_forward(z_ref[...].reshape(cb * r, r), cb, w1h, w1l, w2h, w2l, twa, twb)
    k = k_ref[...]
    kr, ki = k[..., :r], k[..., r:]
    x = _cmul(x, jnp.concatenate([kr, kr], axis=-1), jnp.concatenate([-ki, ki], axis=-1))
    x = _cmm(x.reshape(cb * r, 2 * r), w2ch, w2cl).reshape(cb, r, 2 * r)
    x = _tile_transpose(_cmul(x, twa[...], twbc[...]))
    o_ref[...] = _cmm(x.reshape(cb * r, 2 * r), w1ch, w1cl).reshape(cb, r, r)


def _const_spec(a):
    return pl.BlockSpec(a.shape, lambda i: (0,) * a.ndim)


def hyena_filter_fft(k_time, consts):
    r = FFT_R
    c = k_time.shape[1]
    kt = k_time.reshape(r, r, c).transpose(2, 1, 0)
    ws = [*consts['w1f'], *consts['w2'], consts['twa'], consts['twb']]
    return pl.pallas_call(
        _hy_filter_kernel,
        grid=(c // FFT_CB,),
        in_specs=[pl.BlockSpec((FFT_CB, r, r), lambda i: (i, 0, 0))] + [_const_spec(w) for w in ws],
        out_specs=pl.BlockSpec((FFT_CB, r, 2 * r), lambda i: (i, 0, 0)),
        out_shape=jax.ShapeDtypeStruct((c, r, 2 * r), F32),
        compiler_params=pltpu.CompilerParams(vmem_limit_bytes=VMEM_LIMIT),
        name="hyena_filter_fft",
    )(kt, *ws)


def hyena_long_conv(u, k_spec, consts):
    r = FFT_R
    bn, ll, c = u.shape
    assert bn == 2 and 2 * ll == r * r
    z = u.reshape(2, r // 2, r, c).transpose(3, 2, 0, 1).reshape(c, r, r)
    ws = [*consts['w1'], *consts['w2'], *consts['w2c'], *consts['w1c'],
          consts['twa'], consts['twb'], consts['twbc']]
    y = pl.pallas_call(
        _hy_conv_kernel,
        grid=(c // FFT_CB,),
        in_specs=[pl.BlockSpec((FFT_CB, r, r), lambda i: (i, 0, 0)),
                  pl.BlockSpec((FFT_CB, r, 2 * r), lambda i: (i, 0, 0))] + [_const_spec(w) for w in ws],
        out_specs=pl.BlockSpec((FFT_CB, r, r), lambda i: (i, 0, 0)),
        out_shape=jax.ShapeDtypeStruct((c, r, r), F32),
        compiler_params=pltpu.CompilerParams(vmem_limit_bytes=VMEM_LIMIT),
        name="hyena_long_conv",
    )(z, k_spec, *ws)
    return y.reshape(c, r, 2, r // 2).transpose(2, 3, 1, 0).reshape(2, ll, c)


def hyena_mixer(z, conv_w, conv_b, w1, b1, w2, b2, w3, b3, w4, freq, bias):
    L = z.shape[1]
    zc = short_conv3(z, conv_w, conv_b)
    x1, x2, v = jnp.split(zc, 3, axis=-1)
    k_t = hyena_filter_spectrum(L, w1, b1, w2, b2, w3, b3, w4, freq)
    y = v.astype(F32)
    use_mxu_fft = z.shape[0] == 2 and 2 * L == FFT_R * FFT_R
    if use_mxu_fft:
        consts = _dft_constants()
    else:
        k_f = jnp.fft.rfft(k_t, axis=0)
    for o, gate in enumerate((x1, x2)):
        if use_mxu_fft:
            conv = hyena_long_conv(y, hyena_filter_fft(k_t[:, o], consts), consts)
        else:
            conv = fft_long_conv(y, k_f[:, o])
        y = gate * (conv + y * bias[o])
    return y.astype(z.dtype)


def _mm3(x, w):
    b, l, k = x.shape
    return matmul_pallas(x.reshape(b * l, k), w).reshape(b, l, w.shape[1])


def mla_heads(z, q_norm, w_uq, kv_norm, w_ukv):
    Bn, L, _ = z.shape
    q = _mm3(rms_norm(z[..., O_HY:O_Q], q_norm), w_uq).reshape(Bn, L, MLA_HEADS, MLA_NOPE + MLA_ROPE)
    kv = _mm3(rms_norm(z[..., O_Q:O_KV], kv_norm), w_ukv).reshape(Bn, L, MLA_HEADS, MLA_NOPE + MLA_V)
    return q[..., :MLA_NOPE], q[..., MLA_NOPE:], kv[..., :MLA_NOPE], z[..., O_KV:], kv[..., MLA_NOPE:]


def even_mixer(in_ctx, in_lat, cos, sin, need_ctx, w_in, hy, mla, w_o):
    z_ctx = _mm3(in_ctx, w_in)
    z_lat = _mm3(in_lat, w_in)
    cqn, cqr, ckn, ckr, cv = mla_heads(z_ctx, *mla)
    lqn, lqr, lkn, lkr, lv = mla_heads(z_lat, *mla)
    lqr = apply_axial_rope(lqr, cos[:, None, :], sin[:, None, :])
    lkr = apply_axial_rope(lkr, cos, sin)
    kn = jnp.concatenate([ckn, lkn], axis=1)
    kr = jnp.concatenate([ckr, lkr], axis=1)
    vv = jnp.concatenate([cv, lv], axis=1)
    att = mla_attention_pallas(lqn, lqr, kn, kr, vv)
    y_lat = _mm3(jnp.concatenate([hyena_mixer(z_lat[..., :O_HY], *hy), att], axis=-1), w_o)
    y_ctx = None
    if need_ctx:
        att_c = mla_attention_pallas(cqn, cqr, ckn, ckr, cv)
        y_ctx = _mm3(jnp.concatenate([hyena_mixer(z_ctx[..., :O_HY], *hy), att_c], axis=-1), w_o)
    return y_ctx, y_lat


def _scan_op(e1, e2):
    a1, b1 = e1
    a2, b2 = e2
    return a1 * a2, a2 * b1 + b2


def linear_scan(a_bar, bu, s0):
    bu = bu.at[:, 0].add(a_bar * s0)
    a = jnp.broadcast_to(a_bar, bu.shape)
    _, s = lax.associative_scan(_scan_op, (a, bu), axis=1)
    return s


def _rev(s, d):
    return s[:, ::-1] if d == 1 else s


def s5_mixer(in_ctx, in_lat, need_ctx, lam_re, lam_im, log_dt, b_re, b_im, c_re, c_im, d_skip, w_o, w_g):
    s_ctx, s_lat = s5_scan(in_ctx, in_lat, lam_re, lam_im, log_dt, b_re, b_im, c_re, c_im)

    def finish(ys, h):
        y = ys + d_skip * h.astype(F32)
        y = jax.nn.gelu(y).astype(h.dtype)
        b, l, dm = y.shape
        return glu_pallas(y.reshape(b * l, dm), w_o, w_g).reshape(b, l, dm)

    y_lat = finish(s_lat, in_lat)
    y_ctx = finish(s_ctx, in_ctx) if need_ctx else None
    return y_ctx, y_lat


def peer_ffn(h, w_q, keys, u_tab, v_tab):
    Bn, L, D = h.shape
    hb = h.reshape(Bn * L // PEER_BLOCK, PEER_BLOCK, D)

    def block_fn(xb):
        q = (xb @ w_q).reshape(PEER_BLOCK, PEER_HEADS, 2, PEER_DK // 2)
        s = jnp.einsum('thsk,hsnk->thsn', q, keys).astype(F32)
        top_s, top_i = lax.top_k(s, PEER_TOPK)
        cand = top_s[:, :, 0, :, None] + top_s[:, :, 1, None, :]
        cand_s, cand_j = lax.top_k(cand.reshape(PEER_BLOCK, PEER_HEADS, PEER_TOPK * PEER_TOPK), PEER_TOPK)
        idx = (jnp.take_along_axis(top_i[:, :, 0], cand_j // PEER_TOPK, axis=-1) * PEER_NK
               + jnp.take_along_axis(top_i[:, :, 1], cand_j % PEER_TOPK, axis=-1))
        g = jax.nn.softmax(cand_s, axis=-1)
        act = jax.nn.gelu(jnp.einsum('thkd,td->thk', u_tab[idx], xb).astype(F32))
        return jnp.einsum('thk,thkd->td', (g * act).astype(xb.dtype), v_tab[idx])

    return lax.map(block_fn, hb).reshape(Bn, L, D)


def kernel(x, c, ctx, c_ctx, mod_w, mod_b, ln_mix_g, ln_mix_b, ln_ffn_g, ln_ffn_b,
           ev_w_in, ev_conv_w, ev_conv_b, hy_w1, hy_b1, hy_w2, hy_b2, hy_w3, hy_b3, hy_w4,
           hy_freq, hy_bias, mla_q_norm, mla_w_uq, mla_kv_norm, mla_w_ukv, ev_w_o,
           s5_lam_re, s5_lam_im, s5_log_dt, s5_b_re, s5_b_im, s5_c_re, s5_c_im, s5_d,
           od_w_o, od_w_g, peer_w_q, peer_keys, peer_u, peer_v):
    cos, sin = axial_rope(x.shape[1])
    act_lat = jax.nn.silu(c)
    act_ctx = jax.nn.silu(c_ctx)
    h_lat, h_ctx = x, ctx
    for layer in range(DEPTH):
        need_ctx = layer < DEPTH - 1
        m_lat = jnp.split((act_lat @ mod_w[layer] + mod_b[layer])[:, None, :], 6, axis=-1)
        m_ctx = jnp.split((act_ctx @ mod_w[layer] + mod_b[layer])[None, None, :], 6, axis=-1)
        in_lat = h_lat * (1.0 + m_lat[1]) + m_lat[0]
        in_ctx = h_ctx * (1.0 + m_ctx[1]) + m_ctx[0]
        i = layer // 2
        if layer % 2 == 0:
            hy = (ev_conv_w[i], ev_conv_b[i], hy_w1[i], hy_b1[i], hy_w2[i], hy_b2[i], hy_w3[i], hy_b3[i],
                  hy_w4[i], hy_freq[i], hy_bias[i])
            mla = (mla_q_norm[i], mla_w_uq[i], mla_kv_norm[i], mla_w_ukv[i])
            y_ctx, y_lat = even_mixer(in_ctx, in_lat, cos, sin, need_ctx, ev_w_in[i], hy, mla, ev_w_o[i])
        else:
            y_ctx, y_lat = s5_mixer(in_ctx, in_lat, need_ctx, s5_lam_re[i], s5_lam_im[i], s5_log_dt[i],
                                    s5_b_re[i], s5_b_im[i], s5_c_re[i], s5_c_im[i], s5_d[i],
                                    od_w_o[i], od_w_g[i])
        h_lat = layer_norm(DN_ALPHA * h_lat + m_lat[2] * y_lat, ln_mix_g[layer], ln_mix_b[layer])
        bn, ll, dm = h_lat.shape
        xin = (h_lat * (1.0 + m_lat[4]) + m_lat[3]).reshape(bn * ll, dm)
        if need_ctx:
            h_ctx = layer_norm(DN_ALPHA * h_ctx + m_ctx[2] * y_ctx, ln_mix_g[layer], ln_mix_b[layer])
            xin_ctx = (h_ctx * (1.0 + m_ctx[4]) + m_ctx[3]).reshape(-1, dm)
            xin = jnp.concatenate([xin, xin_ctx], axis=0)
        f_all = peer_ffn_pallas(xin, peer_w_q[layer], peer_keys[layer],
                                peer_u[layer].astype(BF16), peer_v[layer].T.astype(BF16))
        f_lat = f_all[:bn * ll].reshape(bn, ll, dm)
        pre = DN_ALPHA * h_lat + m_lat[5] * f_lat
        if layer == DEPTH - 1:
            h_lat = layer_norm_pallas(pre, ln_ffn_g[layer], ln_ffn_b[layer])
        else:
            h_lat = layer_norm(pre, ln_ffn_g[layer], ln_ffn_b[layer])
        if need_ctx:
            f_ctx = f_all[bn * ll:].reshape(h_ctx.shape)
            h_ctx = layer_norm(DN_ALPHA * h_ctx + m_ctx[5] * f_ctx, ln_ffn_g[layer], ln_ffn_b[layer])
    return h_lat
```

```python
import math
import functools
import numpy as np
import jax
import jax.numpy as jnp
from jax import lax
from jax.experimental import pallas as pl
from jax.experimental.pallas import tpu as pltpu

D_MODEL = 2048
BATCH = 2
SEQ = 8192
DEPTH = 4

GRID_W = 64
CTX_LEN = 256

HY_W = 1024
HY_ORDER = 2
HY_EMB = 33
HY_BANDS = (HY_EMB - 1) // 2
HY_HID = 64
HY_FAST_DECAY = 0.3
HY_SLOW_DECAY = 1.5
HY_TARGET = 1e-2

MLA_HEADS = 8
MLA_NOPE = 128
MLA_ROPE = 64
MLA_V = 128
MLA_Q_RANK = 512
MLA_KV_RANK = 256
MLA_SCALE = (MLA_NOPE + MLA_ROPE) ** -0.5
Q_BLOCK = 128
ROPE_BASE = 10000.0

O_HY = 3 * HY_W
O_Q = O_HY + MLA_Q_RANK
O_KV = O_Q + MLA_KV_RANK
IN_W = O_KV + MLA_ROPE

S5_GC = 16
S5_NG = D_MODEL // S5_GC
S5_P = 64
S5_GB = 16
S5_NBLK = S5_NG // S5_GB

PEER_HEADS = 8
PEER_NK = 128
PEER_N = PEER_NK * PEER_NK
PEER_DK = 256
PEER_TOPK = 16
PEER_BLOCK = 128

N_EVEN = (DEPTH + 1) // 2
N_ODD = DEPTH // 2
DN_ALPHA = (2.0 * DEPTH) ** 0.25
DN_BETA = (8.0 * DEPTH) ** -0.25
LN_EPS = 1e-5
RMS_EPS = 1e-6
F32 = jnp.float32


def _ln_kernel(x_ref, g_ref, b_ref, o_ref):
    xf = x_ref[...]
    xc = xf - jnp.mean(xf, axis=-1, keepdims=True)
    var = jnp.mean(xc * xc, axis=-1, keepdims=True)
    o_ref[...] = xc * lax.rsqrt(var + LN_EPS) * g_ref[...] + b_ref[...]


LN_TL = 256


def _ln_mod_kernel(a_ref, y_ref, gate_ref, g_ref, b_ref, shift_ref, scale_ref, h_ref, x_ref, *xt_ref):
    v = DN_ALPHA * a_ref[0] + gate_ref[0] * y_ref[0]
    vc = v - jnp.mean(v, axis=-1, keepdims=True)
    var = jnp.mean(vc * vc, axis=-1, keepdims=True)
    h = vc * lax.rsqrt(var + LN_EPS) * g_ref[...] + b_ref[...]
    h_ref[0] = h
    xm = h * (1.0 + scale_ref[0]) + shift_ref[0]
    x_ref[0] = xm
    if xt_ref:
        xt_ref[0][...] = xm.T.astype(BF16)


def ln_mod_pallas(a, y, gate, g, b, shift, scale, with_transpose):
    gn, t, d = a.shape
    tl = min(LN_TL, t)
    nt = t // tl
    tok = pl.BlockSpec((1, tl, d), lambda gi, i: (gi, i, 0))
    per_group = pl.BlockSpec((1, 1, d), lambda gi, i: (gi, 0, 0))
    vec = pl.BlockSpec((1, d), lambda gi, i: (0, 0))
    out_specs = [tok, tok]
    out_shape = [jax.ShapeDtypeStruct((gn, t, d), F32)] * 2
    if with_transpose:
        out_specs.append(pl.BlockSpec((d, tl), lambda gi, i: (0, gi * nt + i)))
        out_shape.append(jax.ShapeDtypeStruct((d, gn * t), BF16))
    return pl.pallas_call(
        _ln_mod_kernel,
        grid=(gn, nt),
        in_specs=[tok, tok, per_group, vec, vec, per_group, per_group],
        out_specs=out_specs,
        out_shape=out_shape,
        compiler_params=pltpu.CompilerParams(vmem_limit_bytes=VMEM_LIMIT),
        name="ln_modulate",
    )(a, y, gate, g.reshape(1, d), b.reshape(1, d), shift, scale)


def layer_norm_pallas(x, g, b):
    shp = x.shape
    x2 = x.reshape(-1, shp[-1])
    n, d = x2.shape
    tb = 256
    out = pl.pallas_call(
        _ln_kernel,
        grid=(n // tb,),
        in_specs=[pl.BlockSpec((tb, d), lambda i: (i, 0)),
                  pl.BlockSpec((1, d), lambda i: (0, 0)),
                  pl.BlockSpec((1, d), lambda i: (0, 0))],
        out_specs=pl.BlockSpec((tb, d), lambda i: (i, 0)),
        out_shape=jax.ShapeDtypeStruct((n, d), F32),
        name="layer_norm",
    )(x2, g.reshape(1, d), b.reshape(1, d))
    return out.reshape(shp)


BF16 = jnp.bfloat16
HI = lax.Precision.HIGHEST
VMEM_LIMIT = 56 * 1024 * 1024

S5_T = 16
S5_CB = 8


def _s5_operators(lam_re, lam_im, log_dt, b_re, b_im, c_re, c_im):
    T = S5_T
    ng, p, gc = b_re.shape[1:]
    dt = jnp.exp(log_dt)[..., None]
    zr, zi = lam_re * dt, lam_im * dt
    k = jnp.arange(T + 1, dtype=F32)[:, None, None, None]
    mag = jnp.exp(zr[None] * k)
    pr, pi = mag * jnp.cos(zi[None] * k), mag * jnp.sin(zi[None] * k)
    ar, ai = pr[1], pi[1]
    den = lam_re * lam_re + lam_im * lam_im
    cr = ((ar - 1.0) * lam_re + ai * lam_im) / den
    ci = (ai * lam_re - (ar - 1.0) * lam_im) / den
    bbr = cr[..., None] * b_re - ci[..., None] * b_im
    bbi = cr[..., None] * b_im + ci[..., None] * b_re
    car = c_re[None] * pr[:, :, :, None, :] - c_im[None] * pi[:, :, :, None, :]
    cai = c_re[None] * pi[:, :, :, None, :] + c_im[None] * pr[:, :, :, None, :]
    kk = (jnp.einsum('tdgcp,dgpe->tdgce', car[:T], bbr, precision=HI)
          - jnp.einsum('tdgcp,dgpe->tdgce', cai[:T], bbi, precision=HI))
    kf, kb = kk[:, 0], kk[:, 1]
    kfull = jnp.concatenate([kb[1:][::-1], (kf[0] + kb[0])[None], kf[1:]], axis=0)
    jj = jnp.arange(T)
    idx = jj[None, :] - jj[:, None] + (T - 1)
    m = kfull[idx]
    m = m.transpose(2, 0, 4, 1, 3).reshape(ng, T * gc, T * gc)
    def e_w(d, pw_idx):
        er = pr[pw_idx, d][..., None] * bbr[d][None] - pi[pw_idx, d][..., None] * bbi[d][None]
        ei = pr[pw_idx, d][..., None] * bbi[d][None] + pi[pw_idx, d][..., None] * bbr[d][None]
        w = jnp.concatenate([er, ei], axis=2)
        return w.transpose(1, 0, 3, 2).reshape(ng, T * gc, 2 * p)
    w_e = jnp.concatenate([e_w(0, T - 1 - jj), e_w(1, jj)], axis=-1)
    def s_w(d, pw_idx):
        w = jnp.concatenate([car[pw_idx, d], -cai[pw_idx, d]], axis=-1)
        return w.transpose(1, 3, 0, 2).reshape(ng, 2 * p, T * gc)
    w_y = jnp.concatenate([m, s_w(0, jj + 1), s_w(1, T - jj)], axis=1)
    a1 = jnp.concatenate([pr[T], pr[T]], axis=-1)
    a2 = jnp.concatenate([-pi[T], pi[T]], axis=-1)
    return w_e, w_y, a1, a2


def _s5_lane_group():
    return lax.broadcasted_iota(jnp.int32, (1, LANES), 1) // S5_GC


def _s5_row_block(n):
    return 24 if n % 24 == 0 else 8


def _s5_group_rows(x_ref, xg_ref):
    nrow, T, _ = x_ref.shape
    per = LANES // S5_GC
    grp = _s5_lane_group()
    rb = _s5_row_block(nrow)

    def body(i, carry):
        rows = pl.ds(pl.multiple_of(i * rb, 8), rb)
        for hf in range(T // per):
            accs = [None] * per
            for j in range(per):
                xt = x_ref[rows, hf * per + j, :]
                for g in range(per):
                    sh = ((j - g) * S5_GC) % LANES
                    r = pltpu.roll(xt, sh, 1) if sh else xt
                    accs[g] = r if accs[g] is None else jnp.where(grp == j, r, accs[g])
            for g in range(per):
                xg_ref[g, rows, hf * LANES:(hf + 1) * LANES] = accs[g]
        return carry

    lax.fori_loop(0, nrow // rb, body, 0)


def _s5_scatter_rows(yg_scr, o_ref):
    nrow, T, _ = o_ref.shape
    per = LANES // S5_GC
    grp = _s5_lane_group()
    rb = _s5_row_block(nrow)

    def body(i, carry):
        rows = pl.ds(pl.multiple_of(i * rb, 8), rb)
        for t in range(T):
            hf, j = divmod(t, per)
            acc = None
            for g in range(per):
                piece = yg_scr[g, rows, hf * LANES:(hf + 1) * LANES]
                sh = ((g - j) * S5_GC) % LANES
                r = pltpu.roll(piece, sh, 1) if sh else piece
                acc = r if acc is None else jnp.where(grp == g, r, acc)
            o_ref[rows, t, :] = acc
        return carry

    lax.fori_loop(0, nrow // rb, body, 0)


def _s5_end_state_kernel(x_ref, w_ref, ef_ref, eb_ref, xg_ref):
    _s5_group_rows(x_ref, xg_ref)
    half = w_ref.shape[-1] // 2

    def body(g, carry):
        e = jnp.dot(xg_ref[g], w_ref[g], precision=HI, preferred_element_type=F32)
        cols = pl.ds(pl.multiple_of(g * half, LANES), half)
        ef_ref[:, cols] = e[:, :half]
        eb_ref[:, cols] = e[:, half:]
        return carry

    lax.fori_loop(0, xg_ref.shape[0], body, 0)


def _s5_carry_kernel(ef_ref, eb_ref, a1_ref, a2_ref, s0f_ref, s0b_ref, sf_scr, sb_scr):
    @pl.when(pl.program_id(1) == 0)
    def _():
        sf_scr[...] = jnp.zeros_like(sf_scr)
        sb_scr[...] = jnp.zeros_like(sb_scr)

    half = sf_scr.shape[-1] // 2

    def step(s, e, d):
        return a1_ref[d] * s + a2_ref[d] * pltpu.roll(s, half, 1) + e

    for r in range(S5_CB):
        s = sf_scr[...]
        s0f_ref[r] = s
        sf_scr[...] = step(s, ef_ref[r], 0)
    for r in reversed(range(S5_CB)):
        s = sb_scr[...]
        s0b_ref[r] = s
        sb_scr[...] = step(s, eb_ref[r], 1)


def _s5_output_kernel(xg_ref, s0f_ref, s0b_ref, w_ref, y_ref, xg_scr):
    kx = xg_scr.shape[-1]
    ks = s0f_ref.shape[-1] // xg_scr.shape[0]

    def body(g, carry):
        cols = pl.ds(pl.multiple_of(g * ks, LANES), ks)
        y = jnp.dot(xg_ref[g], w_ref[g, :kx, :], precision=HI, preferred_element_type=F32)
        y = y + jnp.dot(s0f_ref[:, cols], w_ref[g, kx:kx + ks, :], precision=HI, preferred_element_type=F32)
        y = y + jnp.dot(s0b_ref[:, cols], w_ref[g, kx + ks:, :], precision=HI, preferred_element_type=F32)
        xg_scr[g] = y
        return carry

    lax.fori_loop(0, xg_scr.shape[0], body, 0)
    _s5_scatter_rows(xg_scr, y_ref)


def s5_scan(in_ctx, in_lat, lam_re, lam_im, log_dt, b_re, b_im, c_re, c_im):
    T = S5_T
    bn, lc, d = in_ctx.shape
    ll = in_lat.shape[1]
    ng, p, gc = b_re.shape[1:]
    w_e, w_y, a1, a2 = _s5_operators(lam_re, lam_im, log_dt, b_re, b_im, c_re, c_im)
    u = jnp.concatenate([in_ctx, in_lat], axis=1)
    nch = (lc + ll) // T
    rows = bn * nch
    kx, ks = T * gc, 2 * p
    per = LANES // gc
    ncb = ng // per
    x3 = u.reshape(rows, T, d)

    grouped = pl.BlockSpec((per, nch, kx), lambda c, r: (c, r, 0))
    ef, eb, xg = pl.pallas_call(
        _s5_end_state_kernel,
        grid=(ncb, bn),
        in_specs=[pl.BlockSpec((nch, T, LANES), lambda c, r: (r, 0, c)),
                  pl.BlockSpec((per, kx, 2 * ks), lambda c, r: (c, 0, 0))],
        out_specs=[pl.BlockSpec((nch, per * ks), lambda c, r: (r, c)),
                   pl.BlockSpec((nch, per * ks), lambda c, r: (r, c)),
                   grouped],
        out_shape=[jax.ShapeDtypeStruct((rows, ng * ks), F32)] * 2
        + [jax.ShapeDtypeStruct((ng, rows, kx), F32)],
        compiler_params=pltpu.CompilerParams(vmem_limit_bytes=VMEM_LIMIT),
        name="s5_end_state",
    )(x3, w_e)

    nblk = nch // S5_CB
    cblk = (lc // T) // S5_CB
    fwd_map = lambda b, j: (b * nblk + j, 0, 0)
    bwd_map = lambda b, j: (b * nblk + jnp.where(j < cblk, cblk - 1 - j, nblk + cblk - 1 - j), 0, 0)
    s0f, s0b = pl.pallas_call(
        _s5_carry_kernel,
        grid=(bn, nblk),
        in_specs=[pl.BlockSpec((S5_CB, ng, ks), fwd_map),
                  pl.BlockSpec((S5_CB, ng, ks), bwd_map),
                  pl.BlockSpec((2, ng, ks), lambda b, j: (0, 0, 0)),
                  pl.BlockSpec((2, ng, ks), lambda b, j: (0, 0, 0))],
        out_specs=[pl.BlockSpec((S5_CB, ng, ks), fwd_map),
                   pl.BlockSpec((S5_CB, ng, ks), bwd_map)],
        out_shape=[jax.ShapeDtypeStruct((rows, ng, ks), F32)] * 2,
        scratch_shapes=[pltpu.VMEM((ng, ks), F32), pltpu.VMEM((ng, ks), F32)],
        name="s5_carry",
    )(ef.reshape(rows, ng, ks), eb.reshape(rows, ng, ks), a1, a2)

    y3 = pl.pallas_call(
        _s5_output_kernel,
        grid=(ncb, bn),
        in_specs=[grouped,
                  pl.BlockSpec((nch, per * ks), lambda c, r: (r, c)),
                  pl.BlockSpec((nch, per * ks), lambda c, r: (r, c)),
                  pl.BlockSpec((per, kx + 2 * ks, kx), lambda c, r: (c, 0, 0))],
        out_specs=pl.BlockSpec((nch, T, LANES), lambda c, r: (r, 0, c)),
        out_shape=jax.ShapeDtypeStruct((rows, T, d), F32),
        scratch_shapes=[pltpu.VMEM((per, nch, kx), F32)],
        compiler_params=pltpu.CompilerParams(vmem_limit_bytes=VMEM_LIMIT),
        name="s5_output",
    )(xg, s0f.reshape(rows, ng * ks), s0b.reshape(rows, ng * ks), w_y)
    y = y3.reshape(bn, lc + ll, d)
    return y[:, :lc], y[:, lc:]


PEER_TS = 256
PEER_TB = 512
PEER_EB = 1024
LANES = 128
NEG = -3.0e38
_NN = (((1,), (0,)), ((), ()))
_NT = (((1,), (1,)), ((), ()))


def _split_bf16(a):
    hi = a.astype(BF16)
    return hi, (a - hi.astype(F32)).astype(BF16)


def _dot3(ah, al, bh, bl, dims):
    f = lambda a, b: lax.dot_general(a, b, dims, preferred_element_type=F32)
    return f(ah, bh) + f(al, bh) + f(ah, bl)


def _peer_cells():
    return PEER_TOPK + 8 * (PEER_TOPK // 2 - 1) + 8


def _peer_score_kernel(x_ref, wq_ref, kh_ref, kl_ref,
                       n1_ref, w1_ref, r2_ref, e2_ref,
                       st_scr, a1_scr, a2_scr, c_scr):
    nh, _, nk, dk = kh_ref.shape
    tb = x_ref.shape[0]
    q = jnp.dot(x_ref[...].astype(BF16), wq_ref[...], preferred_element_type=F32)
    qh, ql = _split_bf16(q)
    for h in range(nh):
        for s in range(2):
            c0 = (2 * h + s) * dk
            st_scr[2 * h + s] = _dot3(kh_ref[h, s], kl_ref[h, s],
                                      qh[:, c0:c0 + dk], ql[:, c0:c0 + dk], _NT)

    def top_values(s, out_scr):
        rank = jnp.full(s.shape, float(PEER_TOPK), F32)
        for r in range(PEER_TOPK):
            m = jnp.max(s, axis=0, keepdims=True)
            out_scr[r:r + 1, :] = m
            hit = s == m
            rank = jnp.where(hit, float(r), rank)
            s = jnp.where(hit, NEG, s)
        return rank

    nsub = tb // LANES
    half = PEER_TOPK // 2

    def body(it, carry):
        h = it // nsub
        lanes = pl.ds(pl.multiple_of((it % nsub) * LANES, LANES), LANES)
        s1 = st_scr[2 * h, :, lanes]
        s2 = st_scr[2 * h + 1, :, lanes]
        top_values(s1, a1_scr)
        r2 = top_values(s2, a2_scr)
        a1 = a1_scr[...]
        a2 = a2_scr[...]
        c_scr[0:PEER_TOPK, :] = a1[0:1] + a2
        for k in range(1, half):
            c_scr[PEER_TOPK + 8 * (k - 1):PEER_TOPK + 8 * k, :] = a1[k:k + 1] + a2[0:8]
        c_scr[PEER_TOPK + 8 * (half - 1):, :] = a1[half:] + a2[0:1]
        c = c_scr[...]
        m0 = a1[0:1] + a2[0:1]
        z = jnp.zeros_like(m0)
        tau = m0
        for r in range(PEER_TOPK):
            tau = jnp.max(c, axis=0, keepdims=True)
            z = z + jnp.exp(tau - m0)
            c = jnp.where(c == tau, NEG, c)
        n1 = jnp.zeros_like(s1)
        for l in range(PEER_TOPK):
            n1 = n1 + jnp.where(s1 + a2[l:l + 1] >= tau, 1.0, 0.0)
        n1_ref[h, :, lanes] = n1
        w1_ref[h, :, lanes] = jnp.exp(s1 - a1[0:1]) / z
        r2_ref[h, :, lanes] = pltpu.bitcast(r2.astype(BF16), jnp.uint32)
        e2_ref[h, :, lanes] = pltpu.bitcast(jnp.exp(s2 - a2[0:1]).astype(BF16), jnp.uint32)
        return carry

    lax.fori_loop(0, nh * nsub, body, 0)


def _peer_expert_kernel(xt_ref, u_ref, vt_ref, n1_ref, w1_ref, r2_ref, e2_ref,
                        o_ref, acc_scr, a_scr, p_scr):
    e = pl.program_id(1)
    last = pl.num_programs(1) - 1
    slot = e % 2
    nh, nk, tb = r2_ref.shape[0], 2 * r2_ref.shape[1], r2_ref.shape[2]
    ni = a_scr.shape[0] // nk

    @pl.when(e == 0)
    def _():
        acc_scr[...] = jnp.zeros_like(acc_scr)
        p_scr[1] = jnp.zeros(p_scr.shape[1:], BF16)

    @pl.when(e < last)
    def _():
        acc_scr[...] += jnp.dot(vt_ref[...], p_scr[1 - slot], preferred_element_type=F32)
        a_scr[...] = jnp.dot(u_ref[...], xt_ref[...], preferred_element_type=F32)
        for ts in range(tb // LANES):
            lanes = slice(ts * LANES, (ts + 1) * LANES)
            for ii in range(ni):
                rows = slice(ii * nk, (ii + 1) * nk)
                g = jnp.zeros((nk, LANES), BF16)
                for h in range(nh):
                    n1row = jnp.broadcast_to(n1_ref[h, ii:ii + 1, lanes].astype(BF16), (nk, LANES))
                    w1row = jnp.broadcast_to(w1_ref[h, ii:ii + 1, lanes].astype(BF16), (nk, LANES))
                    r2 = pltpu.bitcast(r2_ref[h, :, lanes], BF16)
                    e2 = pltpu.bitcast(e2_ref[h, :, lanes], BF16)
                    g = g + jnp.where(r2 < n1row, e2, jnp.zeros((), BF16)) * w1row
                p_scr[slot, rows, lanes] = g * jax.nn.gelu(a_scr[rows, lanes].astype(BF16))

    @pl.when(e == last)
    def _():
        acc = acc_scr[...] + jnp.dot(vt_ref[...], p_scr[1 - slot], preferred_element_type=F32)
        o_ref[...] = acc.T


def peer_ffn_pallas(xin, xin_t, w_q, keys, u_bf, vt_bf):
    n, d = xin.shape
    nh, _, nk, dk = keys.shape
    ne = u_bf.shape[0]
    ts = min(PEER_TS, n)
    tb = min(PEER_TB, n)
    eb = min(PEER_EB, ne)
    wq = w_q.astype(BF16)
    kh, kl = _split_bf16(keys)
    feat32 = jax.ShapeDtypeStruct((nh, nk, n), F32)
    feat16 = jax.ShapeDtypeStruct((nh, nk // 2, n), jnp.uint32)
    pack_spec = pl.BlockSpec((nh, nk // 2, ts), lambda i: (0, 0, i))
    feat_spec = pl.BlockSpec((nh, nk, ts), lambda i: (0, 0, i))
    n1, w1, r2, e2 = pl.pallas_call(
        _peer_score_kernel,
        grid=(n // ts,),
        in_specs=[pl.BlockSpec((ts, d), lambda i: (i, 0)),
                  pl.BlockSpec(wq.shape, lambda i: (0, 0)),
                  pl.BlockSpec(kh.shape, lambda i: (0, 0, 0, 0)),
                  pl.BlockSpec(kl.shape, lambda i: (0, 0, 0, 0))],
        out_specs=[feat_spec, feat_spec, pack_spec, pack_spec],
        out_shape=[feat32, feat32, feat16, feat16],
        scratch_shapes=[pltpu.VMEM((2 * nh, nk, ts), F32),
                        pltpu.VMEM((PEER_TOPK, LANES), F32),
                        pltpu.VMEM((PEER_TOPK, LANES), F32),
                        pltpu.VMEM((_peer_cells(), LANES), F32)],
        compiler_params=pltpu.CompilerParams(vmem_limit_bytes=VMEM_LIMIT),
        name="peer_scores",
    )(xin, wq, kh, kl)

    ni = eb // nk
    nblk = ne // eb
    row_spec = pl.BlockSpec((nh, ni, tb), lambda t, e: (0, jnp.minimum(e, nblk - 1), t))
    col_spec = pl.BlockSpec((nh, nk // 2, tb), lambda t, e: (0, 0, t))
    return pl.pallas_call(
        _peer_expert_kernel,
        grid=(n // tb, nblk + 1),
        in_specs=[pl.BlockSpec((d, tb), lambda t, e: (0, t)),
                  pl.BlockSpec((eb, d), lambda t, e: (jnp.minimum(e, nblk - 1), 0)),
                  pl.BlockSpec((d, eb), lambda t, e: (0, jnp.maximum(e - 1, 0))),
                  row_spec, row_spec, col_spec, col_spec],
        out_specs=pl.BlockSpec((tb, d), lambda t, e: (t, 0)),
        out_shape=jax.ShapeDtypeStruct((n, d), F32),
        scratch_shapes=[pltpu.VMEM((d, tb), F32),
                        pltpu.VMEM((eb, tb), F32),
                        pltpu.VMEM((2, eb, tb), BF16)],
        compiler_params=pltpu.CompilerParams(vmem_limit_bytes=VMEM_LIMIT,
                                             dimension_semantics=("arbitrary", "arbitrary")),
        name="peer_experts",
    )(xin_t, u_bf, vt_bf, n1, w1, r2, e2)


MM_TM = 512
MM_TN = 512


def _mm_kernel(x_ref, w_ref, o_ref):
    o_ref[...] = jnp.dot(x_ref[...].astype(BF16), w_ref[...], preferred_element_type=F32)


def _glu_kernel(x_ref, wo_ref, wg_ref, o_ref):
    xb = x_ref[...].astype(BF16)
    a = jnp.dot(xb, wo_ref[...], preferred_element_type=F32)
    g = jnp.dot(xb, wg_ref[...], preferred_element_type=F32)
    o_ref[...] = a * jax.nn.sigmoid(g)


def _mm_call(body, x, ws, name):
    m, k = x.shape
    n = ws[0].shape[1]
    tm = min(MM_TM, m)
    tn = min(MM_TN, n)
    return pl.pallas_call(
        body,
        grid=(m // tm, n // tn),
        in_specs=[pl.BlockSpec((tm, k), lambda i, j: (i, 0))]
        + [pl.BlockSpec((k, tn), lambda i, j: (0, j)) for _ in ws],
        out_specs=pl.BlockSpec((tm, tn), lambda i, j: (i, j)),
        out_shape=jax.ShapeDtypeStruct((m, n), F32),
        compiler_params=pltpu.CompilerParams(vmem_limit_bytes=VMEM_LIMIT),
        name=name,
    )(x, *ws)


def matmul_pallas(x, w):
    n = w.shape[1]
    npad = -n % MM_TN if n > MM_TN else -n % LANES
    wb = jnp.pad(w.astype(BF16), ((0, 0), (0, npad)))
    out = _mm_call(_mm_kernel, x, [wb], "dense_proj")
    return out[:, :n] if npad else out


def _mm_pair_kernel(x1_ref, x2_ref, w1_ref, w2_ref, o_ref):
    o_ref[...] = (jnp.dot(x1_ref[...].astype(BF16), w1_ref[...], preferred_element_type=F32)
                  + jnp.dot(x2_ref[...].astype(BF16), w2_ref[...], preferred_element_type=F32))


def matmul_pair_pallas(x1, x2, w):
    m, k1 = x1.shape
    k2 = x2.shape[1]
    n = w.shape[1]
    tm, tn = min(MM_TM, m), min(MM_TN, n)
    wb = w.astype(BF16)
    return pl.pallas_call(
        _mm_pair_kernel,
        grid=(m // tm, n // tn),
        in_specs=[pl.BlockSpec((tm, k1), lambda i, j: (i, 0)),
                  pl.BlockSpec((tm, k2), lambda i, j: (i, 0)),
                  pl.BlockSpec((k1, tn), lambda i, j: (0, j)),
                  pl.BlockSpec((k2, tn), lambda i, j: (0, j))],
        out_specs=pl.BlockSpec((tm, tn), lambda i, j: (i, j)),
        out_shape=jax.ShapeDtypeStruct((m, n), F32),
        compiler_params=pltpu.CompilerParams(vmem_limit_bytes=VMEM_LIMIT),
        name="dense_proj_pair",
    )(x1, x2, wb[:k1], wb[k1:])


def glu_pallas(x, w_o, w_g):
    return _mm_call(_glu_kernel, x, [w_o.astype(BF16), w_g.astype(BF16)], "glu_proj")


ATT_TQ = 512
ATT_TK = 2816


def _attn_kernel(q_ref, kt_ref, v_ref, o_ref, *, tk):
    q = q_ref[0, 0]
    tq = q.shape[0]
    nk = kt_ref.shape[-1]
    m = jnp.full((tq, 1), NEG, F32)
    l = jnp.zeros((tq, 1), F32)
    acc = jnp.zeros((tq, v_ref.shape[-1]), F32)
    for c in range(nk // tk):
        s = jnp.dot(q, kt_ref[0, 0, :, c * tk:(c + 1) * tk], preferred_element_type=F32)
        m_new = jnp.maximum(m, jnp.max(s, axis=-1, keepdims=True))
        alpha = jnp.exp(m - m_new)
        p = jnp.exp(s - m_new)
        l = alpha * l + jnp.sum(p, axis=-1, keepdims=True)
        acc = alpha * acc + jnp.dot(p.astype(BF16), v_ref[0, 0, c * tk:(c + 1) * tk, :],
                                    preferred_element_type=F32)
        m = m_new
    o_ref[0] = acc / l


def mla_attention_pallas(qn, qr, kn, kr, v):
    bn, lq, nh, _ = qn.shape
    sk = kn.shape[1]
    dv = v.shape[-1]
    q = (jnp.concatenate([qn, qr], axis=-1) * MLA_SCALE).astype(BF16).transpose(0, 2, 1, 3)
    kcat = jnp.concatenate([kn, jnp.broadcast_to(kr[:, :, None, :], (bn, sk, nh, kr.shape[-1]))], axis=-1)
    kt = kcat.astype(BF16).transpose(0, 2, 3, 1)
    vv = v.astype(BF16).transpose(0, 2, 1, 3)
    dq = q.shape[-1]
    tq = min(ATT_TQ, lq)
    tk = ATT_TK if sk % ATT_TK == 0 else sk
    return pl.pallas_call(
        functools.partial(_attn_kernel, tk=tk),
        grid=(bn, nh, lq // tq),
        in_specs=[pl.BlockSpec((1, 1, tq, dq), lambda b, h, i: (b, h, i, 0)),
                  pl.BlockSpec((1, 1, dq, sk), lambda b, h, i: (b, h, 0, 0)),
                  pl.BlockSpec((1, 1, sk, dv), lambda b, h, i: (b, h, 0, 0))],
        out_specs=pl.BlockSpec((1, tq, dv), lambda b, h, i: (b, i, h)),
        out_shape=jax.ShapeDtypeStruct((bn, lq, nh * dv), F32),
        compiler_params=pltpu.CompilerParams(vmem_limit_bytes=VMEM_LIMIT),
        name="mla_attention",
    )(q, kt, vv)


def layer_norm(x, g, b):
    xf = x.astype(F32)
    xc = xf - jnp.mean(xf, axis=-1, keepdims=True)
    var = jnp.mean(xc * xc, axis=-1, keepdims=True)
    return (xc * lax.rsqrt(var + LN_EPS) * g + b).astype(x.dtype)


def rms_norm(x, g):
    xf = x.astype(F32)
    return (xf * lax.rsqrt(jnp.mean(xf * xf, axis=-1, keepdims=True) + RMS_EPS) * g).astype(x.dtype)


def axial_rope(L):
    rows = L // GRID_W
    row = jnp.repeat(jnp.arange(rows), GRID_W).astype(F32)
    col = jnp.tile(jnp.arange(GRID_W), rows).astype(F32)
    half = MLA_ROPE // 2
    inv = ROPE_BASE ** (-jnp.arange(0, half, 2, dtype=F32) / half)
    ar, ac = row[:, None] * inv, col[:, None] * inv
    ang = jnp.concatenate([ar, ar, ac, ac], axis=-1)
    return jnp.cos(ang), jnp.sin(ang)


def apply_axial_rope(x, cos, sin):
    xs = x.reshape(x.shape[:-1] + (2, 2, MLA_ROPE // 4))
    rot = jnp.concatenate([-xs[..., 1:, :], xs[..., :1, :]], axis=-2).reshape(x.shape)
    return (x * cos + rot * sin).astype(x.dtype)


def short_conv3(z, w, b):
    L = z.shape[1]
    zp = jnp.pad(z, ((0, 0), (1, 1), (0, 0)))
    return zp[:, :L] * w[0] + zp[:, 1:L + 1] * w[1] + zp[:, 2:] * w[2] + b


def hyena_pos_features(L):
    t = jnp.linspace(0.0, 1.0, L, dtype=F32)[:, None]
    w = (2.0 * math.pi / L) * jnp.arange(L, dtype=F32)[:, None]
    f = jnp.linspace(1e-4, HY_BANDS - 1, HY_BANDS, dtype=F32)[None, :]
    return jnp.concatenate([t, jnp.cos(f * w), -jnp.sin(f * w)], axis=-1)


def hyena_filter_spectrum(L, w1, b1, w2, b2, w3, b3, w4, freq):
    act = lambda a: jnp.sin(freq * a)
    hdn = act(hyena_pos_features(L) @ w1 + b1)
    hdn = act(hdn @ w2 + b2)
    hdn = act(hdn @ w3 + b3)
    deltas = jnp.abs(jnp.linspace(math.log(HY_TARGET) / HY_SLOW_DECAY, math.log(HY_TARGET) / HY_FAST_DECAY,
                                  HY_ORDER * HY_W, dtype=F32)).reshape(HY_ORDER, HY_W)
    t = jnp.linspace(0.0, 1.0, L, dtype=F32)[:, None, None]
    w4d = w4.reshape(w4.shape[0], 2, HY_ORDER * HY_W)
    hf = (hdn @ w4d[:, 0]).astype(F32).reshape(L, HY_ORDER, HY_W) * jnp.exp(-t * deltas)
    hb = (hdn[::-1] @ w4d[:, 1]).astype(F32).reshape(L, HY_ORDER, HY_W) * jnp.exp(-t[::-1] * deltas)
    norm = jnp.sum(jnp.abs(hf), axis=0) + jnp.sum(jnp.abs(hb), axis=0)
    return jnp.concatenate([hf / norm, jnp.zeros((1, HY_ORDER, HY_W), F32), hb[:L - 1] / norm], axis=0)


def fft_long_conv(u, k_f):
    L = u.shape[1]
    u_f = jnp.fft.rfft(u.astype(F32), n=2 * L, axis=1)
    return jnp.fft.irfft(u_f * k_f[None], n=2 * L, axis=1)[:, :L]


FFT_R = 128
FFT_CB = 16


def _dft_constants():
    r = FFT_R
    n = r * r
    idx = np.arange(r)
    f = np.exp(-2j * np.pi * np.outer(idx, idx) / r)
    tw = np.exp(-2j * np.pi * np.outer(idx, idx) / n)
    blk = lambda w: np.block([[w.real, w.imag], [-w.imag, w.real]])
    mats = {
        'w1': blk(f[:r // 2, :]),
        'w1f': np.concatenate([f.real, f.imag], 1),
        'w2': blk(f),
        'w2c': blk(np.conj(f)),
        'w1c': blk(np.conj(f)[:, :r // 2]) / n,
    }
    out = {}
    for name, w in mats.items():
        w32 = jnp.asarray(w, F32)
        out[name] = _split_bf16(w32)
    out['twa'] = jnp.asarray(np.concatenate([tw.real, tw.real], 1), F32)
    out['twb'] = jnp.asarray(np.concatenate([-tw.imag, tw.imag], 1), F32)
    out['twbc'] = jnp.asarray(np.concatenate([tw.imag, -tw.imag], 1), F32)
    return out


def _cmm(x, wh_ref, wl_ref):
    xh, xl = _split_bf16(x)
    return _dot3(xh, xl, wh_ref[...], wl_ref[...], _NN)


def _swap_halves(x):
    h = x.shape[-1] // 2
    return jnp.concatenate([x[..., h:], x[..., :h]], axis=-1)


def _cmul(x, a, b):
    return x * a + _swap_halves(x) * b


def _tile_transpose(x):
    h = x.shape[-1] // 2
    return jnp.concatenate([jnp.swapaxes(x[..., :h], 1, 2), jnp.swapaxes(x[..., h:], 1, 2)], axis=-1)


def _fft_forward(z2d, cb, w1h, w1l, w2h, w2l, twa, twb):
    r = FFT_R
    x = _cmm(z2d, w1h, w1l).reshape(cb, r, 2 * r)
    x = _tile_transpose(_cmul(x, twa[...], twb[...]))
    return _cmm(x.reshape(cb * r, 2 * r), w2h, w2l).reshape(cb, r, 2 * r)


def _hy_filter_kernel(k_ref, w1h, w1l, w2h, w2l, twa, twb, o_ref):
    cb = k_ref.shape[0]
    o_ref[...] = _fft_forward(k_ref[...].reshape(cb * FFT_R, FFT_R), cb, w1h, w1l, w2h, w2l, twa, twb)


def _hy_conv_kernel(z_ref, k_ref, w1h, w1l, w2h, w2l, w2ch, w2cl, w1ch, w1cl, twa, twb, twbc, o_ref):
    cb = z_ref.shape[0]
    r = FFT_R
    x = _fft_forward(z_ref[...].reshape(cb * r, r), cb, w1h, w1l, w2h, w2l, twa, twb)
    k = k_ref[...]
    kr, ki = k[..., :r], k[..., r:]
    x = _cmul(x, jnp.concatenate([kr, kr], axis=-1), jnp.concatenate([-ki, ki], axis=-1))
    x = _cmm(x.reshape(cb * r, 2 * r), w2ch, w2cl).reshape(cb, r, 2 * r)
    x = _tile_transpose(_cmul(x, twa[...], twbc[...]))
    o_ref[...] = _cmm(x.reshape(cb * r, 2 * r), w1ch, w1cl).reshape(cb, r, r)


def _const_spec(a):
    return pl.BlockSpec(a.shape, lambda i: (0,) * a.ndim)


def hyena_filter_fft(k_time, consts):
    r = FFT_R
    c = k_time.shape[1]
    kt = k_time.reshape(r, r, c).transpose(2, 1, 0)
    ws = [*consts['w1f'], *consts['w2'], consts['twa'], consts['twb']]
    return pl.pallas_call(
        _hy_filter_kernel,
        grid=(c // FFT_CB,),
        in_specs=[pl.BlockSpec((FFT_CB, r, r), lambda i: (i, 0, 0))] + [_const_spec(w) for w in ws],
        out_specs=pl.BlockSpec((FFT_CB, r, 2 * r), lambda i: (i, 0, 0)),
        out_shape=jax.ShapeDtypeStruct((c, r, 2 * r), F32),
        compiler_params=pltpu.CompilerParams(vmem_limit_bytes=VMEM_LIMIT),
        name="hyena_filter_fft",
    )(kt, *ws)


def hyena_long_conv(u, k_spec, consts):
    r = FFT_R
    bn, ll, c = u.shape
    assert bn == 2 and 2 * ll == r * r
    z = u.reshape(2, r // 2, r, c).transpose(3, 2, 0, 1).reshape(c, r, r)
    ws = [*consts['w1'], *consts['w2'], *consts['w2c'], *consts['w1c'],
          consts['twa'], consts['twb'], consts['twbc']]
    y = pl.pallas_call(
        _hy_conv_kernel,
        grid=(c // FFT_CB,),
        in_specs=[pl.BlockSpec((FFT_CB, r, r), lambda i: (i, 0, 0)),
                  pl.BlockSpec((FFT_CB, r, 2 * r), lambda i: (i, 0, 0))] + [_const_spec(w) for w in ws],
        out_specs=pl.BlockSpec((FFT_CB, r, r), lambda i: (i, 0, 0)),
        out_shape=jax.ShapeDtypeStruct((c, r, r), F32),
        compiler_params=pltpu.CompilerParams(vmem_limit_bytes=VMEM_LIMIT),
        name="hyena_long_conv",
    )(z, k_spec, *ws)
    return y.reshape(c, r, 2, r // 2).transpose(2, 3, 1, 0).reshape(2, ll, c)


def hyena_mixer(z, conv_w, conv_b, w1, b1, w2, b2, w3, b3, w4, freq, bias):
    L = z.shape[1]
    zc = short_conv3(z, conv_w, conv_b)
    x1, x2, v = jnp.split(zc, 3, axis=-1)
    k_t = hyena_filter_spectrum(L, w1, b1, w2, b2, w3, b3, w4, freq)
    y = v.astype(F32)
    use_mxu_fft = z.shape[0] == 2 and 2 * L == FFT_R * FFT_R
    if use_mxu_fft:
        consts = _dft_constants()
    else:
        k_f = jnp.fft.rfft(k_t, axis=0)
    for o, gate in enumerate((x1, x2)):
        if use_mxu_fft:
            conv = hyena_long_conv(y, hyena_filter_fft(k_t[:, o], consts), consts)
        else:
            conv = fft_long_conv(y, k_f[:, o])
        y = gate * (conv + y * bias[o])
    return y.astype(z.dtype)


def _mm3(x, w):
    b, l, k = x.shape
    return matmul_pallas(x.reshape(b * l, k), w).reshape(b, l, w.shape[1])


def mla_heads(zs, q_norm, w_uq, kv_norm, w_ukv):
    Bn, L, _ = zs.shape
    q = _mm3(rms_norm(zs[..., :O_Q - O_HY], q_norm), w_uq).reshape(Bn, L, MLA_HEADS, MLA_NOPE + MLA_ROPE)
    kv = _mm3(rms_norm(zs[..., O_Q - O_HY:O_KV - O_HY], kv_norm), w_ukv).reshape(Bn, L, MLA_HEADS, MLA_NOPE + MLA_V)
    return q[..., :MLA_NOPE], q[..., MLA_NOPE:], kv[..., :MLA_NOPE], zs[..., O_KV - O_HY:], kv[..., MLA_NOPE:]


def even_mixer(in_ctx, in_lat, cos, sin, need_ctx, w_in, hy, mla, w_o):
    w_hy, w_att = w_in[:, :O_HY], w_in[:, O_HY:]
    cqn, cqr, ckn, ckr, cv = mla_heads(_mm3(in_ctx, w_att), *mla)
    lqn, lqr, lkn, lkr, lv = mla_heads(_mm3(in_lat, w_att), *mla)
    lqr = apply_axial_rope(lqr, cos[:, None, :], sin[:, None, :])
    lkr = apply_axial_rope(lkr, cos, sin)
    kn = jnp.concatenate([ckn, lkn], axis=1)
    kr = jnp.concatenate([ckr, lkr], axis=1)
    vv = jnp.concatenate([cv, lv], axis=1)

    def out_proj(hyena, att):
        b, l, k = hyena.shape
        return matmul_pair_pallas(hyena.reshape(b * l, k), att.reshape(b * l, -1), w_o).reshape(b, l, -1)

    y_lat = out_proj(hyena_mixer(_mm3(in_lat, w_hy), *hy), mla_attention_pallas(lqn, lqr, kn, kr, vv))
    y_ctx = None
    if need_ctx:
        y_ctx = out_proj(hyena_mixer(_mm3(in_ctx, w_hy), *hy), mla_attention_pallas(cqn, cqr, ckn, ckr, cv))
    return y_ctx, y_lat


def _scan_op(e1, e2):
    a1, b1 = e1
    a2, b2 = e2
    return a1 * a2, a2 * b1 + b2


def linear_scan(a_bar, bu, s0):
    bu = bu.at[:, 0].add(a_bar * s0)
    a = jnp.broadcast_to(a_bar, bu.shape)
    _, s = lax.associative_scan(_scan_op, (a, bu), axis=1)
    return s


def _rev(s, d):
    return s[:, ::-1] if d == 1 else s


def s5_mixer(in_ctx, in_lat, need_ctx, lam_re, lam_im, log_dt, b_re, b_im, c_re, c_im, d_skip, w_o, w_g):
    s_ctx, s_lat = s5_scan(in_ctx, in_lat, lam_re, lam_im, log_dt, b_re, b_im, c_re, c_im)

    def finish(ys, h):
        y = ys + d_skip * h.astype(F32)
        y = jax.nn.gelu(y).astype(h.dtype)
        b, l, dm = y.shape
        return glu_pallas(y.reshape(b * l, dm), w_o, w_g).reshape(b, l, dm)

    y_lat = finish(s_lat, in_lat)
    y_ctx = finish(s_ctx, in_ctx) if need_ctx else None
    return y_ctx, y_lat


def peer_ffn(h, w_q, keys, u_tab, v_tab):
    Bn, L, D = h.shape
    hb = h.reshape(Bn * L // PEER_BLOCK, PEER_BLOCK, D)

    def block_fn(xb):
        q = (xb @ w_q).reshape(PEER_BLOCK, PEER_HEADS, 2, PEER_DK // 2)
        s = jnp.einsum('thsk,hsnk->thsn', q, keys).astype(F32)
        top_s, top_i = lax.top_k(s, PEER_TOPK)
        cand = top_s[:, :, 0, :, None] + top_s[:, :, 1, None, :]
        cand_s, cand_j = lax.top_k(cand.reshape(PEER_BLOCK, PEER_HEADS, PEER_TOPK * PEER_TOPK), PEER_TOPK)
        idx = (jnp.take_along_axis(top_i[:, :, 0], cand_j // PEER_TOPK, axis=-1) * PEER_NK
               + jnp.take_along_axis(top_i[:, :, 1], cand_j % PEER_TOPK, axis=-1))
        g = jax.nn.softmax(cand_s, axis=-1)
        act = jax.nn.gelu(jnp.einsum('thkd,td->thk', u_tab[idx], xb).astype(F32))
        return jnp.einsum('thk,thkd->td', (g * act).astype(xb.dtype), v_tab[idx])

    return lax.map(block_fn, hb).reshape(Bn, L, D)


def kernel(x, c, ctx, c_ctx, mod_w, mod_b, ln_mix_g, ln_mix_b, ln_ffn_g, ln_ffn_b,
           ev_w_in, ev_conv_w, ev_conv_b, hy_w1, hy_b1, hy_w2, hy_b2, hy_w3, hy_b3, hy_w4,
           hy_freq, hy_bias, mla_q_norm, mla_w_uq, mla_kv_norm, mla_w_ukv, ev_w_o,
           s5_lam_re, s5_lam_im, s5_log_dt, s5_b_re, s5_b_im, s5_c_re, s5_c_im, s5_d,
           od_w_o, od_w_g, peer_w_q, peer_keys, peer_u, peer_v):
    cos, sin = axial_rope(x.shape[1])
    act_lat = jax.nn.silu(c)
    act_ctx = jax.nn.silu(c_ctx)
    bn, ll, dm = x.shape
    lc = ctx.shape[1]
    mods_lat = [jnp.split((act_lat @ mod_w[l] + mod_b[l])[:, None, :], 6, axis=-1) for l in range(DEPTH)]
    mods_ctx = [jnp.split((act_ctx @ mod_w[l] + mod_b[l])[None, None, :], 6, axis=-1) for l in range(DEPTH)]
    h_lat = x
    h_ctx = ctx.reshape(1, bn * lc, dm)
    in_lat = h_lat * (1.0 + mods_lat[0][1]) + mods_lat[0][0]
    in_ctx = (h_ctx * (1.0 + mods_ctx[0][1]) + mods_ctx[0][0]).reshape(bn, lc, dm)
    zero_lat = jnp.zeros_like(mods_lat[0][0])
    zero_ctx = jnp.zeros_like(mods_ctx[0][0])
    for layer in range(DEPTH):
        need_ctx = layer < DEPTH - 1
        m_lat, m_ctx = mods_lat[layer], mods_ctx[layer]
        i = layer // 2
        if layer % 2 == 0:
            hy = (ev_conv_w[i], ev_conv_b[i], hy_w1[i], hy_b1[i], hy_w2[i], hy_b2[i], hy_w3[i], hy_b3[i],
                  hy_w4[i], hy_freq[i], hy_bias[i])
            mla = (mla_q_norm[i], mla_w_uq[i], mla_kv_norm[i], mla_w_ukv[i])
            y_ctx, y_lat = even_mixer(in_ctx, in_lat, cos, sin, need_ctx, ev_w_in[i], hy, mla, ev_w_o[i])
        else:
            y_ctx, y_lat = s5_mixer(in_ctx, in_lat, need_ctx, s5_lam_re[i], s5_lam_im[i], s5_log_dt[i],
                                    s5_b_re[i], s5_b_im[i], s5_c_re[i], s5_c_im[i], s5_d[i],
                                    od_w_o[i], od_w_g[i])
        u_bf = peer_u[layer].astype(BF16)
        vt_bf = peer_v[layer].T.astype(BF16)
        peer = lambda xm, xt: peer_ffn_pallas(xm.reshape(-1, dm), xt, peer_w_q[layer], peer_keys[layer],
                                              u_bf, vt_bf).reshape(xm.shape)
        nxt_lat = (mods_lat[layer + 1][0], mods_lat[layer + 1][1]) if need_ctx else (zero_lat, zero_lat)
        h_lat, xin, xin_t = ln_mod_pallas(h_lat, y_lat, m_lat[2], ln_mix_g[layer], ln_mix_b[layer],
                                          m_lat[3], m_lat[4], True)
        h_lat, in_lat = ln_mod_pallas(h_lat, peer(xin, xin_t), m_lat[5], ln_ffn_g[layer], ln_ffn_b[layer],
                                      nxt_lat[0], nxt_lat[1], False)
        if need_ctx:
            h_ctx, xin, xin_t = ln_mod_pallas(h_ctx, y_ctx.reshape(h_ctx.shape), m_ctx[2],
                                              ln_mix_g[layer], ln_mix_b[layer], m_ctx[3], m_ctx[4], True)
            h_ctx, in_ctx = ln_mod_pallas(h_ctx, peer(xin, xin_t), m_ctx[5], ln_ffn_g[layer], ln_ffn_b[layer],
                                          mods_ctx[layer + 1][0], mods_ctx[layer + 1][1], False)
            in_ctx = in_ctx.reshape(bn, lc, dm)
    return h_lat
```

```python
import math
import functools
import numpy as np
import jax
import jax.numpy as jnp
from jax import lax
from jax.experimental import pallas as pl
from jax.experimental.pallas import tpu as pltpu

D_MODEL = 2048
BATCH = 2
SEQ = 8192
DEPTH = 4

GRID_W = 64
CTX_LEN = 256

HY_W = 1024
HY_ORDER = 2
HY_EMB = 33
HY_BANDS = (HY_EMB - 1) // 2
HY_HID = 64
HY_FAST_DECAY = 0.3
HY_SLOW_DECAY = 1.5
HY_TARGET = 1e-2

MLA_HEADS = 8
MLA_NOPE = 128
MLA_ROPE = 64
MLA_V = 128
MLA_Q_RANK = 512
MLA_KV_RANK = 256
MLA_SCALE = (MLA_NOPE + MLA_ROPE) ** -0.5
Q_BLOCK = 128
ROPE_BASE = 10000.0

O_HY = 3 * HY_W
O_Q = O_HY + MLA_Q_RANK
O_KV = O_Q + MLA_KV_RANK
IN_W = O_KV + MLA_ROPE

S5_GC = 16
S5_NG = D_MODEL // S5_GC
S5_P = 64
S5_GB = 16
S5_NBLK = S5_NG // S5_GB

PEER_HEADS = 8
PEER_NK = 128
PEER_N = PEER_NK * PEER_NK
PEER_DK = 256
PEER_TOPK = 16
PEER_BLOCK = 128

N_EVEN = (DEPTH + 1) // 2
N_ODD = DEPTH // 2
DN_ALPHA = (2.0 * DEPTH) ** 0.25
DN_BETA = (8.0 * DEPTH) ** -0.25
LN_EPS = 1e-5
RMS_EPS = 1e-6
F32 = jnp.float32


def _ln_kernel(x_ref, g_ref, b_ref, o_ref):
    xf = x_ref[...]
    xc = xf - jnp.mean(xf, axis=-1, keepdims=True)
    var = jnp.mean(xc * xc, axis=-1, keepdims=True)
    o_ref[...] = xc * lax.rsqrt(var + LN_EPS) * g_ref[...] + b_ref[...]


LN_TL = 256


def _ln_mod_kernel(a_ref, y_ref, gate_ref, g_ref, b_ref, shift_ref, scale_ref, h_ref, x_ref, *xt_ref):
    v = DN_ALPHA * a_ref[0] + gate_ref[0] * y_ref[0]
    vc = v - jnp.mean(v, axis=-1, keepdims=True)
    var = jnp.mean(vc * vc, axis=-1, keepdims=True)
    h = vc * lax.rsqrt(var + LN_EPS) * g_ref[...] + b_ref[...]
    h_ref[0] = h
    xm = h * (1.0 + scale_ref[0]) + shift_ref[0]
    x_ref[0] = xm
    if xt_ref:
        xt_ref[0][...] = xm.T.astype(BF16)


def ln_mod_pallas(a, y, gate, g, b, shift, scale, with_transpose):
    gn, t, d = a.shape
    tl = min(LN_TL, t)
    nt = t // tl
    tok = pl.BlockSpec((1, tl, d), lambda gi, i: (gi, i, 0))
    per_group = pl.BlockSpec((1, 1, d), lambda gi, i: (gi, 0, 0))
    vec = pl.BlockSpec((1, d), lambda gi, i: (0, 0))
    out_specs = [tok, tok]
    out_shape = [jax.ShapeDtypeStruct((gn, t, d), F32)] * 2
    if with_transpose:
        out_specs.append(pl.BlockSpec((d, tl), lambda gi, i: (0, gi * nt + i)))
        out_shape.append(jax.ShapeDtypeStruct((d, gn * t), BF16))
    return pl.pallas_call(
        _ln_mod_kernel,
        grid=(gn, nt),
        in_specs=[tok, tok, per_group, vec, vec, per_group, per_group],
        out_specs=out_specs,
        out_shape=out_shape,
        compiler_params=pltpu.CompilerParams(vmem_limit_bytes=VMEM_LIMIT),
        name="ln_modulate",
    )(a, y, gate, g.reshape(1, d), b.reshape(1, d), shift, scale)


def layer_norm_pallas(x, g, b):
    shp = x.shape
    x2 = x.reshape(-1, shp[-1])
    n, d = x2.shape
    tb = 256
    out = pl.pallas_call(
        _ln_kernel,
        grid=(n // tb,),
        in_specs=[pl.BlockSpec((tb, d), lambda i: (i, 0)),
                  pl.BlockSpec((1, d), lambda i: (0, 0)),
                  pl.BlockSpec((1, d), lambda i: (0, 0))],
        out_specs=pl.BlockSpec((tb, d), lambda i: (i, 0)),
        out_shape=jax.ShapeDtypeStruct((n, d), F32),
        name="layer_norm",
    )(x2, g.reshape(1, d), b.reshape(1, d))
    return out.reshape(shp)


BF16 = jnp.bfloat16
HI = lax.Precision.HIGHEST
VMEM_LIMIT = 56 * 1024 * 1024

S5_T = 16
S5_CB = 8


def _s5_operators(lam_re, lam_im, log_dt, b_re, b_im, c_re, c_im):
    T = S5_T
    ng, p, gc = b_re.shape[1:]
    dt = jnp.exp(log_dt)[..., None]
    zr, zi = lam_re * dt, lam_im * dt
    k = jnp.arange(T + 1, dtype=F32)[:, None, None, None]
    mag = jnp.exp(zr[None] * k)
    pr, pi = mag * jnp.cos(zi[None] * k), mag * jnp.sin(zi[None] * k)
    ar, ai = pr[1], pi[1]
    den = lam_re * lam_re + lam_im * lam_im
    cr = ((ar - 1.0) * lam_re + ai * lam_im) / den
    ci = (ai * lam_re - (ar - 1.0) * lam_im) / den
    bbr = cr[..., None] * b_re - ci[..., None] * b_im
    bbi = cr[..., None] * b_im + ci[..., None] * b_re
    car = c_re[None] * pr[:, :, :, None, :] - c_im[None] * pi[:, :, :, None, :]
    cai = c_re[None] * pi[:, :, :, None, :] + c_im[None] * pr[:, :, :, None, :]
    kk = (jnp.einsum('tdgcp,dgpe->tdgce', car[:T], bbr, precision=HI)
          - jnp.einsum('tdgcp,dgpe->tdgce', cai[:T], bbi, precision=HI))
    kf, kb = kk[:, 0], kk[:, 1]
    kfull = jnp.concatenate([kb[1:][::-1], (kf[0] + kb[0])[None], kf[1:]], axis=0)
    jj = jnp.arange(T)
    idx = jj[None, :] - jj[:, None] + (T - 1)
    m = kfull[idx]
    m = m.transpose(2, 0, 4, 1, 3).reshape(ng, T * gc, T * gc)
    def e_w(d, pw_idx):
        er = pr[pw_idx, d][..., None] * bbr[d][None] - pi[pw_idx, d][..., None] * bbi[d][None]
        ei = pr[pw_idx, d][..., None] * bbi[d][None] + pi[pw_idx, d][..., None] * bbr[d][None]
        w = jnp.concatenate([er, ei], axis=2)
        return w.transpose(1, 0, 3, 2).reshape(ng, T * gc, 2 * p)
    w_e = jnp.concatenate([e_w(0, T - 1 - jj), e_w(1, jj)], axis=-1)
    def s_w(d, pw_idx):
        w = jnp.concatenate([car[pw_idx, d], -cai[pw_idx, d]], axis=-1)
        return w.transpose(1, 3, 0, 2).reshape(ng, 2 * p, T * gc)
    w_y = jnp.concatenate([m, s_w(0, jj + 1), s_w(1, T - jj)], axis=1)
    a1 = jnp.concatenate([pr[T], pr[T]], axis=-1)
    a2 = jnp.concatenate([-pi[T], pi[T]], axis=-1)
    return w_e, w_y, a1, a2


def _s5_lane_group():
    return lax.broadcasted_iota(jnp.int32, (1, LANES), 1) // S5_GC


def _s5_row_block(n):
    return next(c for c in (32, 24, 16, 8) if n % c == 0)


def _s5_group_rows(x_ref, xg_ref, row0):
    nrow, T, _ = x_ref.shape
    per = LANES // S5_GC
    grp = _s5_lane_group()
    rb = _s5_row_block(nrow)

    def body(i, carry):
        rows = pl.ds(pl.multiple_of(i * rb, 8), rb)
        dst = pl.ds(pl.multiple_of(row0 + i * rb, 8), rb)
        for hf in range(T // per):
            accs = [None] * per
            for j in range(per):
                xt = x_ref[rows, hf * per + j, :]
                for g in range(per):
                    sh = ((j - g) * S5_GC) % LANES
                    r = pltpu.roll(xt, sh, 1) if sh else xt
                    accs[g] = r if accs[g] is None else jnp.where(grp == j, r, accs[g])
            for g in range(per):
                xg_ref[g, dst, hf * LANES:(hf + 1) * LANES] = accs[g]
        return carry

    lax.fori_loop(0, nrow // rb, body, 0)


def _s5_scatter_rows(yg_scr, o_ref, row0):
    nrow, T, _ = o_ref.shape
    per = LANES // S5_GC
    grp = _s5_lane_group()
    rb = _s5_row_block(nrow)

    def body(i, carry):
        rows = pl.ds(pl.multiple_of(i * rb, 8), rb)
        src = pl.ds(pl.multiple_of(row0 + i * rb, 8), rb)
        for t in range(T):
            hf, j = divmod(t, per)
            acc = None
            for g in range(per):
                piece = yg_scr[g, src, hf * LANES:(hf + 1) * LANES]
                sh = ((g - j) * S5_GC) % LANES
                r = pltpu.roll(piece, sh, 1) if sh else piece
                acc = r if acc is None else jnp.where(grp == g, r, acc)
            o_ref[rows, t, :] = acc
        return carry

    lax.fori_loop(0, nrow // rb, body, 0)


def _s5_end_state_kernel(xc_ref, xl_ref, w_ref, ef_ref, eb_ref, xg_ref):
    _s5_group_rows(xc_ref, xg_ref, 0)
    _s5_group_rows(xl_ref, xg_ref, xc_ref.shape[0])
    half = w_ref.shape[-1] // 2

    def body(g, carry):
        e = jnp.dot(xg_ref[g], w_ref[g], precision=HI, preferred_element_type=F32)
        cols = pl.ds(pl.multiple_of(g * half, LANES), half)
        ef_ref[:, cols] = e[:, :half]
        eb_ref[:, cols] = e[:, half:]
        return carry

    lax.fori_loop(0, xg_ref.shape[0], body, 0)


def _s5_carry_kernel(ef_ref, eb_ref, a1_ref, a2_ref, s0f_ref, s0b_ref, sf_scr, sb_scr):
    @pl.when(pl.program_id(1) == 0)
    def _():
        sf_scr[...] = jnp.zeros_like(sf_scr)
        sb_scr[...] = jnp.zeros_like(sb_scr)

    half = sf_scr.shape[-1] // 2

    def step(s, e, d):
        return a1_ref[d] * s + a2_ref[d] * pltpu.roll(s, half, 1) + e

    for r in range(S5_CB):
        s = sf_scr[...]
        s0f_ref[r] = s
        sf_scr[...] = step(s, ef_ref[r], 0)
    for r in reversed(range(S5_CB)):
        s = sb_scr[...]
        s0b_ref[r] = s
        sb_scr[...] = step(s, eb_ref[r], 1)


def _s5_output_kernel(xg_ref, s0f_ref, s0b_ref, w_ref, yc_ref, yl_ref, xg_scr):
    kx = xg_scr.shape[-1]
    ks = s0f_ref.shape[-1] // xg_scr.shape[0]

    def body(g, carry):
        cols = pl.ds(pl.multiple_of(g * ks, LANES), ks)
        y = jnp.dot(xg_ref[g], w_ref[g, :kx, :], precision=HI, preferred_element_type=F32)
        y = y + jnp.dot(s0f_ref[:, cols], w_ref[g, kx:kx + ks, :], precision=HI, preferred_element_type=F32)
        y = y + jnp.dot(s0b_ref[:, cols], w_ref[g, kx + ks:, :], precision=HI, preferred_element_type=F32)
        xg_scr[g] = y
        return carry

    lax.fori_loop(0, xg_scr.shape[0], body, 0)
    _s5_scatter_rows(xg_scr, yc_ref, 0)
    _s5_scatter_rows(xg_scr, yl_ref, yc_ref.shape[0])


def s5_scan(in_ctx, in_lat, lam_re, lam_im, log_dt, b_re, b_im, c_re, c_im):
    T = S5_T
    bn, lc, d = in_ctx.shape
    ll = in_lat.shape[1]
    ng, p, gc = b_re.shape[1:]
    w_e, w_y, a1, a2 = _s5_operators(lam_re, lam_im, log_dt, b_re, b_im, c_re, c_im)
    ncc, ncl = lc // T, ll // T
    nch = ncc + ncl
    rows = bn * nch
    kx, ks = T * gc, 2 * p
    per = LANES // gc
    ncb = ng // per
    ctx_spec = pl.BlockSpec((ncc, T, LANES), lambda c, r: (r, 0, c))
    lat_spec = pl.BlockSpec((ncl, T, LANES), lambda c, r: (r, 0, c))

    grouped = pl.BlockSpec((per, nch, kx), lambda c, r: (c, r, 0))
    ef, eb, xg = pl.pallas_call(
        _s5_end_state_kernel,
        grid=(ncb, bn),
        in_specs=[ctx_spec, lat_spec,
                  pl.BlockSpec((per, kx, 2 * ks), lambda c, r: (c, 0, 0))],
        out_specs=[pl.BlockSpec((nch, per * ks), lambda c, r: (r, c)),
                   pl.BlockSpec((nch, per * ks), lambda c, r: (r, c)),
                   grouped],
        out_shape=[jax.ShapeDtypeStruct((rows, ng * ks), F32)] * 2
        + [jax.ShapeDtypeStruct((ng, rows, kx), F32)],
        compiler_params=pltpu.CompilerParams(vmem_limit_bytes=VMEM_LIMIT),
        name="s5_end_state",
    )(in_ctx.reshape(bn * ncc, T, d), in_lat.reshape(bn * ncl, T, d), w_e)

    nblk = nch // S5_CB
    cblk = (lc // T) // S5_CB
    fwd_map = lambda b, j: (b * nblk + j, 0, 0)
    bwd_map = lambda b, j: (b * nblk + jnp.where(j < cblk, cblk - 1 - j, nblk + cblk - 1 - j), 0, 0)
    s0f, s0b = pl.pallas_call(
        _s5_carry_kernel,
        grid=(bn, nblk),
        in_specs=[pl.BlockSpec((S5_CB, ng, ks), fwd_map),
                  pl.BlockSpec((S5_CB, ng, ks), bwd_map),
                  pl.BlockSpec((2, ng, ks), lambda b, j: (0, 0, 0)),
                  pl.BlockSpec((2, ng, ks), lambda b, j: (0, 0, 0))],
        out_specs=[pl.BlockSpec((S5_CB, ng, ks), fwd_map),
                   pl.BlockSpec((S5_CB, ng, ks), bwd_map)],
        out_shape=[jax.ShapeDtypeStruct((rows, ng, ks), F32)] * 2,
        scratch_shapes=[pltpu.VMEM((ng, ks), F32), pltpu.VMEM((ng, ks), F32)],
        name="s5_carry",
    )(ef.reshape(rows, ng, ks), eb.reshape(rows, ng, ks), a1, a2)

    y_ctx, y_lat = pl.pallas_call(
        _s5_output_kernel,
        grid=(ncb, bn),
        in_specs=[grouped,
                  pl.BlockSpec((nch, per * ks), lambda c, r: (r, c)),
                  pl.BlockSpec((nch, per * ks), lambda c, r: (r, c)),
                  pl.BlockSpec((per, kx + 2 * ks, kx), lambda c, r: (c, 0, 0))],
        out_specs=[ctx_spec, lat_spec],
        out_shape=[jax.ShapeDtypeStruct((bn * ncc, T, d), F32), jax.ShapeDtypeStruct((bn * ncl, T, d), F32)],
        scratch_shapes=[pltpu.VMEM((per, nch, kx), F32)],
        compiler_params=pltpu.CompilerParams(vmem_limit_bytes=VMEM_LIMIT),
        name="s5_output",
    )(xg, s0f.reshape(rows, ng * ks), s0b.reshape(rows, ng * ks), w_y)
    return y_ctx.reshape(bn, lc, d), y_lat.reshape(bn, ll, d)


PEER_TS = 256
PEER_TB = 512
PEER_EB = 1024
LANES = 128
NEG = -3.0e38
_NN = (((1,), (0,)), ((), ()))
_NT = (((1,), (1,)), ((), ()))


def _split_bf16(a):
    hi = a.astype(BF16)
    return hi, (a - hi.astype(F32)).astype(BF16)


def _dot3(ah, al, bh, bl, dims):
    f = lambda a, b: lax.dot_general(a, b, dims, preferred_element_type=F32)
    return f(ah, bh) + f(al, bh) + f(ah, bl)


def _peer_cells():
    return PEER_TOPK + 8 * (PEER_TOPK // 2 - 1) + 8


def _peer_score_kernel(x_ref, wq_ref, kh_ref, kl_ref,
                       n1_ref, w1_ref, r2_ref, e2_ref,
                       st_scr, a1_scr, a2_scr, c_scr):
    nh, _, nk, dk = kh_ref.shape
    tb = x_ref.shape[0]
    q = jnp.dot(x_ref[...].astype(BF16), wq_ref[...], preferred_element_type=F32)
    qh, ql = _split_bf16(q)
    for h in range(nh):
        for s in range(2):
            c0 = (2 * h + s) * dk
            st_scr[2 * h + s] = _dot3(kh_ref[h, s], kl_ref[h, s],
                                      qh[:, c0:c0 + dk], ql[:, c0:c0 + dk], _NT)

    def top_values(s, out_scr):
        rank = jnp.full(s.shape, float(PEER_TOPK), F32)
        for r in range(PEER_TOPK):
            m = jnp.max(s, axis=0, keepdims=True)
            out_scr[r:r + 1, :] = m
            hit = s == m
            rank = jnp.where(hit, float(r), rank)
            s = jnp.where(hit, NEG, s)
        return rank

    nsub = tb // LANES
    half = PEER_TOPK // 2

    def body(it, carry):
        h = it // nsub
        lanes = pl.ds(pl.multiple_of((it % nsub) * LANES, LANES), LANES)
        s1 = st_scr[2 * h, :, lanes]
        s2 = st_scr[2 * h + 1, :, lanes]
        top_values(s1, a1_scr)
        r2 = top_values(s2, a2_scr)
        a1 = a1_scr[...]
        a2 = a2_scr[...]
        c_scr[0:PEER_TOPK, :] = a1[0:1] + a2
        for k in range(1, half):
            c_scr[PEER_TOPK + 8 * (k - 1):PEER_TOPK + 8 * k, :] = a1[k:k + 1] + a2[0:8]
        c_scr[PEER_TOPK + 8 * (half - 1):, :] = a1[half:] + a2[0:1]
        c = c_scr[...]
        m0 = a1[0:1] + a2[0:1]
        z = jnp.zeros_like(m0)
        tau = m0
        for r in range(PEER_TOPK):
            tau = jnp.max(c, axis=0, keepdims=True)
            z = z + jnp.exp(tau - m0)
            c = jnp.where(c == tau, NEG, c)
        n1 = jnp.zeros_like(s1)
        for l in range(PEER_TOPK):
            n1 = n1 + jnp.where(s1 + a2[l:l + 1] >= tau, 1.0, 0.0)
        n1_ref[h, :, lanes] = n1
        w1_ref[h, :, lanes] = jnp.exp(s1 - a1[0:1]) / z
        r2_ref[h, :, lanes] = pltpu.bitcast(r2.astype(BF16), jnp.uint32)
        e2_ref[h, :, lanes] = pltpu.bitcast(jnp.exp(s2 - a2[0:1]).astype(BF16), jnp.uint32)
        return carry

    lax.fori_loop(0, nh * nsub, body, 0)


def _peer_expert_kernel(xt_ref, u_ref, vt_ref, n1_ref, w1_ref, r2_ref, e2_ref,
                        o_ref, acc_scr, a_scr, p_scr):
    e = pl.program_id(1)
    last = pl.num_programs(1) - 1
    slot = e % 2
    nh, nk, tb = r2_ref.shape[0], 2 * r2_ref.shape[1], r2_ref.shape[2]
    ni = a_scr.shape[0] // nk

    @pl.when(e == 0)
    def _():
        acc_scr[...] = jnp.zeros_like(acc_scr)
        p_scr[1] = jnp.zeros(p_scr.shape[1:], BF16)

    @pl.when(e < last)
    def _():
        acc_scr[...] += jnp.dot(vt_ref[...], p_scr[1 - slot], preferred_element_type=F32)
        a_scr[...] = jnp.dot(u_ref[...], xt_ref[...], preferred_element_type=F32)
        for ts in range(tb // LANES):
            lanes = slice(ts * LANES, (ts + 1) * LANES)
            for ii in range(ni):
                rows = slice(ii * nk, (ii + 1) * nk)
                g = jnp.zeros((nk, LANES), BF16)
                for h in range(nh):
                    n1row = jnp.broadcast_to(n1_ref[h, ii:ii + 1, lanes].astype(BF16), (nk, LANES))
                    w1row = jnp.broadcast_to(w1_ref[h, ii:ii + 1, lanes].astype(BF16), (nk, LANES))
                    r2 = pltpu.bitcast(r2_ref[h, :, lanes], BF16)
                    e2 = pltpu.bitcast(e2_ref[h, :, lanes], BF16)
                    g = g + jnp.where(r2 < n1row, e2, jnp.zeros((), BF16)) * w1row
                p_scr[slot, rows, lanes] = g * jax.nn.gelu(a_scr[rows, lanes].astype(BF16))

    @pl.when(e == last)
    def _():
        acc = acc_scr[...] + jnp.dot(vt_ref[...], p_scr[1 - slot], preferred_element_type=F32)
        o_ref[...] = acc.T


def peer_ffn_pallas(xin, xin_t, w_q, keys, u_bf, vt_bf):
    n, d = xin.shape
    nh, _, nk, dk = keys.shape
    ne = u_bf.shape[0]
    ts = min(PEER_TS, n)
    tb = min(PEER_TB, n)
    eb = min(PEER_EB, ne)
    wq = w_q.astype(BF16)
    kh, kl = _split_bf16(keys)
    feat32 = jax.ShapeDtypeStruct((nh, nk, n), F32)
    feat16 = jax.ShapeDtypeStruct((nh, nk // 2, n), jnp.uint32)
    pack_spec = pl.BlockSpec((nh, nk // 2, ts), lambda i: (0, 0, i))
    feat_spec = pl.BlockSpec((nh, nk, ts), lambda i: (0, 0, i))
    n1, w1, r2, e2 = pl.pallas_call(
        _peer_score_kernel,
        grid=(n // ts,),
        in_specs=[pl.BlockSpec((ts, d), lambda i: (i, 0)),
                  pl.BlockSpec(wq.shape, lambda i: (0, 0)),
                  pl.BlockSpec(kh.shape, lambda i: (0, 0, 0, 0)),
                  pl.BlockSpec(kl.shape, lambda i: (0, 0, 0, 0))],
        out_specs=[feat_spec, feat_spec, pack_spec, pack_spec],
        out_shape=[feat32, feat32, feat16, feat16],
        scratch_shapes=[pltpu.VMEM((2 * nh, nk, ts), F32),
                        pltpu.VMEM((PEER_TOPK, LANES), F32),
                        pltpu.VMEM((PEER_TOPK, LANES), F32),
                        pltpu.VMEM((_peer_cells(), LANES), F32)],
        compiler_params=pltpu.CompilerParams(vmem_limit_bytes=VMEM_LIMIT),
        name="peer_scores",
    )(xin, wq, kh, kl)

    ni = eb // nk
    nblk = ne // eb
    row_spec = pl.BlockSpec((nh, ni, tb), lambda t, e: (0, jnp.minimum(e, nblk - 1), t))
    col_spec = pl.BlockSpec((nh, nk // 2, tb), lambda t, e: (0, 0, t))
    return pl.pallas_call(
        _peer_expert_kernel,
        grid=(n // tb, nblk + 1),
        in_specs=[pl.BlockSpec((d, tb), lambda t, e: (0, t)),
                  pl.BlockSpec((eb, d), lambda t, e: (jnp.minimum(e, nblk - 1), 0)),
                  pl.BlockSpec((d, eb), lambda t, e: (0, jnp.maximum(e - 1, 0))),
                  row_spec, row_spec, col_spec, col_spec],
        out_specs=pl.BlockSpec((tb, d), lambda t, e: (t, 0)),
        out_shape=jax.ShapeDtypeStruct((n, d), F32),
        scratch_shapes=[pltpu.VMEM((d, tb), F32),
                        pltpu.VMEM((eb, tb), F32),
                        pltpu.VMEM((2, eb, tb), BF16)],
        compiler_params=pltpu.CompilerParams(vmem_limit_bytes=VMEM_LIMIT,
                                             dimension_semantics=("arbitrary", "arbitrary")),
        name="peer_experts",
    )(xin_t, u_bf, vt_bf, n1, w1, r2, e2)


MM_TM = 512
MM_TN = 512


def _mm_kernel(x_ref, w_ref, o_ref):
    o_ref[...] = jnp.dot(x_ref[...].astype(BF16), w_ref[...], preferred_element_type=F32)


def _glu_kernel(x_ref, wo_ref, wg_ref, o_ref):
    xb = x_ref[...].astype(BF16)
    a = jnp.dot(xb, wo_ref[...], preferred_element_type=F32)
    g = jnp.dot(xb, wg_ref[...], preferred_element_type=F32)
    o_ref[...] = a * jax.nn.sigmoid(g)


def _mm_call(body, x, ws, name):
    m, k = x.shape
    n = ws[0].shape[1]
    tm = min(MM_TM, m)
    tn = min(MM_TN, n)
    return pl.pallas_call(
        body,
        grid=(m // tm, n // tn),
        in_specs=[pl.BlockSpec((tm, k), lambda i, j: (i, 0))]
        + [pl.BlockSpec((k, tn), lambda i, j: (0, j)) for _ in ws],
        out_specs=pl.BlockSpec((tm, tn), lambda i, j: (i, j)),
        out_shape=jax.ShapeDtypeStruct((m, n), F32),
        compiler_params=pltpu.CompilerParams(vmem_limit_bytes=VMEM_LIMIT),
        name=name,
    )(x, *ws)


def matmul_pallas(x, w):
    n = w.shape[1]
    npad = -n % MM_TN if n > MM_TN else -n % LANES
    wb = jnp.pad(w.astype(BF16), ((0, 0), (0, npad)))
    out = _mm_call(_mm_kernel, x, [wb], "dense_proj")
    return out[:, :n] if npad else out


def _mm_pair_kernel(x1_ref, x2_ref, w1_ref, w2_ref, o_ref):
    o_ref[...] = (jnp.dot(x1_ref[...].astype(BF16), w1_ref[...], preferred_element_type=F32)
                  + jnp.dot(x2_ref[...].astype(BF16), w2_ref[...], preferred_element_type=F32))


def matmul_pair_pallas(x1, x2, w):
    m, k1 = x1.shape
    k2 = x2.shape[1]
    n = w.shape[1]
    tm, tn = min(MM_TM, m), min(MM_TN, n)
    wb = w.astype(BF16)
    return pl.pallas_call(
        _mm_pair_kernel,
        grid=(m // tm, n // tn),
        in_specs=[pl.BlockSpec((tm, k1), lambda i, j: (i, 0)),
                  pl.BlockSpec((tm, k2), lambda i, j: (i, 0)),
                  pl.BlockSpec((k1, tn), lambda i, j: (0, j)),
                  pl.BlockSpec((k2, tn), lambda i, j: (0, j))],
        out_specs=pl.BlockSpec((tm, tn), lambda i, j: (i, j)),
        out_shape=jax.ShapeDtypeStruct((m, n), F32),
        compiler_params=pltpu.CompilerParams(vmem_limit_bytes=VMEM_LIMIT),
        name="dense_proj_pair",
    )(x1, x2, wb[:k1], wb[k1:])


def glu_pallas(x, w_o, w_g):
    return _mm_call(_glu_kernel, x, [w_o.astype(BF16), w_g.astype(BF16)], "glu_proj")


ATT_TQ = 512
ATT_TK = 2816


def _attn_kernel(qn_ref, qr_ref, knt_ref, krt_ref, v_ref, o_ref, *, tk):
    q = jnp.concatenate([qn_ref[0, 0], qr_ref[0, 0]], axis=-1)
    tq = q.shape[0]
    nk = knt_ref.shape[-1]
    m = jnp.full((tq, 1), NEG, F32)
    l = jnp.zeros((tq, 1), F32)
    acc = jnp.zeros((tq, v_ref.shape[-1]), F32)
    for c in range(nk // tk):
        kt = jnp.concatenate([knt_ref[0, 0, :, c * tk:(c + 1) * tk],
                              krt_ref[0, :, c * tk:(c + 1) * tk]], axis=0)
        s = jnp.dot(q, kt, preferred_element_type=F32)
        m_new = jnp.maximum(m, jnp.max(s, axis=-1, keepdims=True))
        alpha = jnp.exp(m - m_new)
        p = jnp.exp(s - m_new)
        l = alpha * l + jnp.sum(p, axis=-1, keepdims=True)
        acc = alpha * acc + jnp.dot(p.astype(BF16), v_ref[0, 0, c * tk:(c + 1) * tk, :],
                                    preferred_element_type=F32)
        m = m_new
    o_ref[0] = acc / l


def mla_attention_pallas(qn, qr, kn, kr, v):
    bn, lq, nh, _ = qn.shape
    sk = kn.shape[1]
    dv = v.shape[-1]
    qnb = (qn * MLA_SCALE).astype(BF16).transpose(0, 2, 1, 3)
    qrb = (qr * MLA_SCALE).astype(BF16).transpose(0, 2, 1, 3)
    knt = kn.astype(BF16).transpose(0, 2, 3, 1)
    krt = kr.astype(BF16).transpose(0, 2, 1)
    vv = v.astype(BF16).transpose(0, 2, 1, 3)
    dn, dr = qn.shape[-1], qr.shape[-1]
    tq = min(ATT_TQ, lq)
    tk = ATT_TK if sk % ATT_TK == 0 else sk
    return pl.pallas_call(
        functools.partial(_attn_kernel, tk=tk),
        grid=(bn, nh, lq // tq),
        in_specs=[pl.BlockSpec((1, 1, tq, dn), lambda b, h, i: (b, h, i, 0)),
                  pl.BlockSpec((1, 1, tq, dr), lambda b, h, i: (b, h, i, 0)),
                  pl.BlockSpec((1, 1, dn, sk), lambda b, h, i: (b, h, 0, 0)),
                  pl.BlockSpec((1, dr, sk), lambda b, h, i: (b, 0, 0)),
                  pl.BlockSpec((1, 1, sk, dv), lambda b, h, i: (b, h, 0, 0))],
        out_specs=pl.BlockSpec((1, tq, dv), lambda b, h, i: (b, i, h)),
        out_shape=jax.ShapeDtypeStruct((bn, lq, nh * dv), F32),
        compiler_params=pltpu.CompilerParams(vmem_limit_bytes=VMEM_LIMIT),
        name="mla_attention",
    )(qnb, qrb, knt, krt, vv)


def layer_norm(x, g, b):
    xf = x.astype(F32)
    xc = xf - jnp.mean(xf, axis=-1, keepdims=True)
    var = jnp.mean(xc * xc, axis=-1, keepdims=True)
    return (xc * lax.rsqrt(var + LN_EPS) * g + b).astype(x.dtype)


def rms_norm(x, g):
    xf = x.astype(F32)
    return (xf * lax.rsqrt(jnp.mean(xf * xf, axis=-1, keepdims=True) + RMS_EPS) * g).astype(x.dtype)


def axial_rope(L):
    rows = L // GRID_W
    row = jnp.repeat(jnp.arange(rows), GRID_W).astype(F32)
    col = jnp.tile(jnp.arange(GRID_W), rows).astype(F32)
    half = MLA_ROPE // 2
    inv = ROPE_BASE ** (-jnp.arange(0, half, 2, dtype=F32) / half)
    ar, ac = row[:, None] * inv, col[:, None] * inv
    ang = jnp.concatenate([ar, ar, ac, ac], axis=-1)
    return jnp.cos(ang), jnp.sin(ang)


def apply_axial_rope(x, cos, sin):
    xs = x.reshape(x.shape[:-1] + (2, 2, MLA_ROPE // 4))
    rot = jnp.concatenate([-xs[..., 1:, :], xs[..., :1, :]], axis=-2).reshape(x.shape)
    return (x * cos + rot * sin).astype(x.dtype)


def short_conv3(z, w, b):
    L = z.shape[1]
    zp = jnp.pad(z, ((0, 0), (1, 1), (0, 0)))
    return zp[:, :L] * w[0] + zp[:, 1:L + 1] * w[1] + zp[:, 2:] * w[2] + b


def hyena_pos_features(L):
    t = jnp.linspace(0.0, 1.0, L, dtype=F32)[:, None]
    w = (2.0 * math.pi / L) * jnp.arange(L, dtype=F32)[:, None]
    f = jnp.linspace(1e-4, HY_BANDS - 1, HY_BANDS, dtype=F32)[None, :]
    return jnp.concatenate([t, jnp.cos(f * w), -jnp.sin(f * w)], axis=-1)


def hyena_filter_spectrum(L, w1, b1, w2, b2, w3, b3, w4, freq):
    act = lambda a: jnp.sin(freq * a)
    hdn = act(hyena_pos_features(L) @ w1 + b1)
    hdn = act(hdn @ w2 + b2)
    hdn = act(hdn @ w3 + b3)
    deltas = jnp.abs(jnp.linspace(math.log(HY_TARGET) / HY_SLOW_DECAY, math.log(HY_TARGET) / HY_FAST_DECAY,
                                  HY_ORDER * HY_W, dtype=F32)).reshape(HY_ORDER, HY_W)
    t = jnp.linspace(0.0, 1.0, L, dtype=F32)[:, None, None]
    w4d = w4.reshape(w4.shape[0], 2, HY_ORDER * HY_W)
    hf = (hdn @ w4d[:, 0]).astype(F32).reshape(L, HY_ORDER, HY_W) * jnp.exp(-t * deltas)
    hb = (hdn[::-1] @ w4d[:, 1]).astype(F32).reshape(L, HY_ORDER, HY_W) * jnp.exp(-t[::-1] * deltas)
    norm = jnp.sum(jnp.abs(hf), axis=0) + jnp.sum(jnp.abs(hb), axis=0)
    return jnp.concatenate([hf / norm, jnp.zeros((1, HY_ORDER, HY_W), F32), hb[:L - 1] / norm], axis=0)


def fft_long_conv(u, k_f):
    L = u.shape[1]
    u_f = jnp.fft.rfft(u.astype(F32), n=2 * L, axis=1)
    return jnp.fft.irfft(u_f * k_f[None], n=2 * L, axis=1)[:, :L]


FFT_R = 128
FFT_CB = 16


def _dft_constants():
    r = FFT_R
    n = r * r
    idx = np.arange(r)
    f = np.exp(-2j * np.pi * np.outer(idx, idx) / r)
    tw = np.exp(-2j * np.pi * np.outer(idx, idx) / n)
    blk = lambda w: np.block([[w.real, w.imag], [-w.imag, w.real]])
    mats = {
        'w1': blk(f[:r // 2, :]),
        'w1f': np.concatenate([f.real, f.imag], 1),
        'w2': blk(f),
        'w2c': blk(np.conj(f)),
        'w1c': blk(np.conj(f)[:, :r // 2]) / n,
    }
    out = {}
    for name, w in mats.items():
        w32 = jnp.asarray(w, F32)
        out[name] = _split_bf16(w32)
    out['twa'] = jnp.asarray(np.concatenate([tw.real, tw.real], 1), F32)
    out['twb'] = jnp.asarray(np.concatenate([-tw.imag, tw.imag], 1), F32)
    out['twbc'] = jnp.asarray(np.concatenate([tw.imag, -tw.imag], 1), F32)
    return out


def _cmm(x, wh_ref, wl_ref):
    xh, xl = _split_bf16(x)
    return _dot3(xh, xl, wh_ref[...], wl_ref[...], _NN)


def _swap_halves(x):
    h = x.shape[-1] // 2
    return jnp.concatenate([x[..., h:], x[..., :h]], axis=-1)


def _cmul(x, a, b):
    return x * a + _swap_halves(x) * b


def _tile_transpose(x):
    h = x.shape[-1] // 2
    return jnp.concatenate([jnp.swapaxes(x[..., :h], 1, 2), jnp.swapaxes(x[..., h:], 1, 2)], axis=-1)


def _fft_forward(z2d, cb, w1h, w1l, w2h, w2l, twa, twb):
    r = FFT_R
    x = _cmm(z2d, w1h, w1l).reshape(cb, r, 2 * r)
    x = _tile_transpose(_cmul(x, twa[...], twb[...]))
    return _cmm(x.reshape(cb * r, 2 * r), w2h, w2l).reshape(cb, r, 2 * r)


def _hy_filter_kernel(k_ref, w1h, w1l, w2h, w2l, twa, twb, o_ref):
    cb = k_ref.shape[0]
    o_ref[...] = _fft_forward(k_ref[...].reshape(cb * FFT_R, FFT_R), cb, w1h, w1l, w2h, w2l, twa, twb)


def _hy_conv_kernel(z_ref, k_ref, w1h, w1l, w2h, w2l, w2ch, w2cl, w1ch, w1cl, twa, twb, twbc, o_ref):
    cb = z_ref.shape[0]
    r = FFT_R
    x = _fft_forward(z_ref[...].reshape(cb * r, r), cb, w1h, w1l, w2h, w2l, twa, twb)
    k = k_ref[...]
    kr, ki = k[..., :r], k[..., r:]
    x = _cmul(x, jnp.concatenate([kr, kr], axis=-1), jnp.concatenate([-ki, ki], axis=-1))
    x = _cmm(x.reshape(cb * r, 2 * r), w2ch, w2cl).reshape(cb, r, 2 * r)
    x = _tile_transpose(_cmul(x, twa[...], twbc[...]))
    o_ref[...] = _cmm(x.reshape(cb * r, 2 * r), w1ch, w1cl).reshape(cb, r, r)


def _const_spec(a):
    return pl.BlockSpec(a.shape, lambda i: (0,) * a.ndim)


def hyena_filter_fft(k_time, consts):
    r = FFT_R
    c = k_time.shape[1]
    kt = k_time.reshape(r, r, c).transpose(2, 1, 0)
    ws = [*consts['w1f'], *consts['w2'], consts['twa'], consts['twb']]
    return pl.pallas_call(
        _hy_filter_kernel,
        grid=(c // FFT_CB,),
        in_specs=[pl.BlockSpec((FFT_CB, r, r), lambda i: (i, 0, 0))] + [_const_spec(w) for w in ws],
        out_specs=pl.BlockSpec((FFT_CB, r, 2 * r), lambda i: (i, 0, 0)),
        out_shape=jax.ShapeDtypeStruct((c, r, 2 * r), F32),
        compiler_params=pltpu.CompilerParams(vmem_limit_bytes=VMEM_LIMIT),
        name="hyena_filter_fft",
    )(kt, *ws)


def hyena_long_conv(u, k_spec, consts):
    r = FFT_R
    bn, ll, c = u.shape
    assert bn == 2 and 2 * ll == r * r
    z = u.reshape(2, r // 2, r, c).transpose(3, 2, 0, 1).reshape(c, r, r)
    ws = [*consts['w1'], *consts['w2'], *consts['w2c'], *consts['w1c'],
          consts['twa'], consts['twb'], consts['twbc']]
    y = pl.pallas_call(
        _hy_conv_kernel,
        grid=(c // FFT_CB,),
        in_specs=[pl.BlockSpec((FFT_CB, r, r), lambda i: (i, 0, 0)),
                  pl.BlockSpec((FFT_CB, r, 2 * r), lambda i: (i, 0, 0))] + [_const_spec(w) for w in ws],
        out_specs=pl.BlockSpec((FFT_CB, r, r), lambda i: (i, 0, 0)),
        out_shape=jax.ShapeDtypeStruct((c, r, r), F32),
        compiler_params=pltpu.CompilerParams(vmem_limit_bytes=VMEM_LIMIT),
        name="hyena_long_conv",
    )(z, k_spec, *ws)
    return y.reshape(c, r, 2, r // 2).transpose(2, 3, 1, 0).reshape(2, ll, c)


def hyena_mixer(z, conv_w, conv_b, w1, b1, w2, b2, w3, b3, w4, freq, bias):
    L = z.shape[1]
    zc = short_conv3(z, conv_w, conv_b)
    x1, x2, v = jnp.split(zc, 3, axis=-1)
    k_t = hyena_filter_spectrum(L, w1, b1, w2, b2, w3, b3, w4, freq)
    y = v.astype(F32)
    use_mxu_fft = z.shape[0] == 2 and 2 * L == FFT_R * FFT_R
    if use_mxu_fft:
        consts = _dft_constants()
    else:
        k_f = jnp.fft.rfft(k_t, axis=0)
    for o, gate in enumerate((x1, x2)):
        if use_mxu_fft:
            conv = hyena_long_conv(y, hyena_filter_fft(k_t[:, o], consts), consts)
        else:
            conv = fft_long_conv(y, k_f[:, o])
        y = gate * (conv + y * bias[o])
    return y.astype(z.dtype)


def _mm3(x, w):
    b, l, k = x.shape
    return matmul_pallas(x.reshape(b * l, k), w).reshape(b, l, w.shape[1])


def mla_heads(zs, q_norm, w_uq, kv_norm, w_ukv):
    Bn, L, _ = zs.shape
    q = _mm3(rms_norm(zs[..., :O_Q - O_HY], q_norm), w_uq).reshape(Bn, L, MLA_HEADS, MLA_NOPE + MLA_ROPE)
    kv = _mm3(rms_norm(zs[..., O_Q - O_HY:O_KV - O_HY], kv_norm), w_ukv).reshape(Bn, L, MLA_HEADS, MLA_NOPE + MLA_V)
    return q[..., :MLA_NOPE], q[..., MLA_NOPE:], kv[..., :MLA_NOPE], zs[..., O_KV - O_HY:], kv[..., MLA_NOPE:]


def even_mixer(in_ctx, in_lat, cos, sin, need_ctx, w_in, hy, mla, w_o):
    w_hy, w_att = w_in[:, :O_HY], w_in[:, O_HY:]
    cqn, cqr, ckn, ckr, cv = mla_heads(_mm3(in_ctx, w_att), *mla)
    lqn, lqr, lkn, lkr, lv = mla_heads(_mm3(in_lat, w_att), *mla)
    lqr = apply_axial_rope(lqr, cos[:, None, :], sin[:, None, :])
    lkr = apply_axial_rope(lkr, cos, sin)
    kn = jnp.concatenate([ckn, lkn], axis=1)
    kr = jnp.concatenate([ckr, lkr], axis=1)
    vv = jnp.concatenate([cv, lv], axis=1)

    def out_proj(hyena, att):
        b, l, k = hyena.shape
        return matmul_pair_pallas(hyena.reshape(b * l, k), att.reshape(b * l, -1), w_o).reshape(b, l, -1)

    y_lat = out_proj(hyena_mixer(_mm3(in_lat, w_hy), *hy), mla_attention_pallas(lqn, lqr, kn, kr, vv))
    y_ctx = None
    if need_ctx:
        y_ctx = out_proj(hyena_mixer(_mm3(in_ctx, w_hy), *hy), mla_attention_pallas(cqn, cqr, ckn, ckr, cv))
    return y_ctx, y_lat


def _scan_op(e1, e2):
    a1, b1 = e1
    a2, b2 = e2
    return a1 * a2, a2 * b1 + b2


def linear_scan(a_bar, bu, s0):
    bu = bu.at[:, 0].add(a_bar * s0)
    a = jnp.broadcast_to(a_bar, bu.shape)
    _, s = lax.associative_scan(_scan_op, (a, bu), axis=1)
    return s


def _rev(s, d):
    return s[:, ::-1] if d == 1 else s


def s5_mixer(in_ctx, in_lat, need_ctx, lam_re, lam_im, log_dt, b_re, b_im, c_re, c_im, d_skip, w_o, w_g):
    s_ctx, s_lat = s5_scan(in_ctx, in_lat, lam_re, lam_im, log_dt, b_re, b_im, c_re, c_im)

    def finish(ys, h):
        y = ys + d_skip * h.astype(F32)
        y = jax.nn.gelu(y).astype(h.dtype)
        b, l, dm = y.shape
        return glu_pallas(y.reshape(b * l, dm), w_o, w_g).reshape(b, l, dm)

    y_lat = finish(s_lat, in_lat)
    y_ctx = finish(s_ctx, in_ctx) if need_ctx else None
    return y_ctx, y_lat


def peer_ffn(h, w_q, keys, u_tab, v_tab):
    Bn, L, D = h.shape
    hb = h.reshape(Bn * L // PEER_BLOCK, PEER_BLOCK, D)

    def block_fn(xb):
        q = (xb @ w_q).reshape(PEER_BLOCK, PEER_HEADS, 2, PEER_DK // 2)
        s = jnp.einsum('thsk,hsnk->thsn', q, keys).astype(F32)
        top_s, top_i = lax.top_k(s, PEER_TOPK)
        cand = top_s[:, :, 0, :, None] + top_s[:, :, 1, None, :]
        cand_s, cand_j = lax.top_k(cand.reshape(PEER_BLOCK, PEER_HEADS, PEER_TOPK * PEER_TOPK), PEER_TOPK)
        idx = (jnp.take_along_axis(top_i[:, :, 0], cand_j // PEER_TOPK, axis=-1) * PEER_NK
               + jnp.take_along_axis(top_i[:, :, 1], cand_j % PEER_TOPK, axis=-1))
        g = jax.nn.softmax(cand_s, axis=-1)
        act = jax.nn.gelu(jnp.einsum('thkd,td->thk', u_tab[idx], xb).astype(F32))
        return jnp.einsum('thk,thkd->td', (g * act).astype(xb.dtype), v_tab[idx])

    return lax.map(block_fn, hb).reshape(Bn, L, D)


def kernel(x, c, ctx, c_ctx, mod_w, mod_b, ln_mix_g, ln_mix_b, ln_ffn_g, ln_ffn_b,
           ev_w_in, ev_conv_w, ev_conv_b, hy_w1, hy_b1, hy_w2, hy_b2, hy_w3, hy_b3, hy_w4,
           hy_freq, hy_bias, mla_q_norm, mla_w_uq, mla_kv_norm, mla_w_ukv, ev_w_o,
           s5_lam_re, s5_lam_im, s5_log_dt, s5_b_re, s5_b_im, s5_c_re, s5_c_im, s5_d,
           od_w_o, od_w_g, peer_w_q, peer_keys, peer_u, peer_v):
    cos, sin = axial_rope(x.shape[1])
    act_lat = jax.nn.silu(c)
    act_ctx = jax.nn.silu(c_ctx)
    bn, ll, dm = x.shape
    lc = ctx.shape[1]
    mods_lat = [jnp.split((act_lat @ mod_w[l] + mod_b[l])[:, None, :], 6, axis=-1) for l in range(DEPTH)]
    mods_ctx = [jnp.split((act_ctx @ mod_w[l] + mod_b[l])[None, None, :], 6, axis=-1) for l in range(DEPTH)]
    h_lat = x
    h_ctx = ctx.reshape(1, bn * lc, dm)
    in_lat = h_lat * (1.0 + mods_lat[0][1]) + mods_lat[0][0]
    in_ctx = (h_ctx * (1.0 + mods_ctx[0][1]) + mods_ctx[0][0]).reshape(bn, lc, dm)
    zero_lat = jnp.zeros_like(mods_lat[0][0])
    zero_ctx = jnp.zeros_like(mods_ctx[0][0])
    for layer in range(DEPTH):
        need_ctx = layer < DEPTH - 1
        m_lat, m_ctx = mods_lat[layer], mods_ctx[layer]
        i = layer // 2
        if layer % 2 == 0:
            hy = (ev_conv_w[i], ev_conv_b[i], hy_w1[i], hy_b1[i], hy_w2[i], hy_b2[i], hy_w3[i], hy_b3[i],
                  hy_w4[i], hy_freq[i], hy_bias[i])
            mla = (mla_q_norm[i], mla_w_uq[i], mla_kv_norm[i], mla_w_ukv[i])
            y_ctx, y_lat = even_mixer(in_ctx, in_lat, cos, sin, need_ctx, ev_w_in[i], hy, mla, ev_w_o[i])
        else:
            y_ctx, y_lat = s5_mixer(in_ctx, in_lat, need_ctx, s5_lam_re[i], s5_lam_im[i], s5_log_dt[i],
                                    s5_b_re[i], s5_b_im[i], s5_c_re[i], s5_c_im[i], s5_d[i],
                                    od_w_o[i], od_w_g[i])
        u_bf = peer_u[layer].astype(BF16)
        vt_bf = peer_v[layer].T.astype(BF16)
        peer = lambda xm, xt: peer_ffn_pallas(xm.reshape(-1, dm), xt, peer_w_q[layer], peer_keys[layer],
                                              u_bf, vt_bf).reshape(xm.shape)
        nxt_lat = (mods_lat[layer + 1][0], mods_lat[layer + 1][1]) if need_ctx else (zero_lat, zero_lat)
        h_lat, xin, xin_t = ln_mod_pallas(h_lat, y_lat, m_lat[2], ln_mix_g[layer], ln_mix_b[layer],
                                          m_lat[3], m_lat[4], True)
        h_lat, in_lat = ln_mod_pallas(h_lat, peer(xin, xin_t), m_lat[5], ln_ffn_g[layer], ln_ffn_b[layer],
                                      nxt_lat[0], nxt_lat[1], False)
        if need_ctx:
            h_ctx, xin, xin_t = ln_mod_pallas(h_ctx, y_ctx.reshape(h_ctx.shape), m_ctx[2],
                                              ln_mix_g[layer], ln_mix_b[layer], m_ctx[3], m_ctx[4], True)
            h_ctx, in_ctx = ln_mod_pallas(h_ctx, peer(xin, xin_t), m_ctx[5], ln_ffn_g[layer], ln_ffn_b[layer],
                                          mods_ctx[layer + 1][0], mods_ctx[layer + 1][1], False)
            in_ctx = in_ctx.reshape(bn, lc, dm)
    return h_lat
```

```python
import math
import functools
import numpy as np
import jax
import jax.numpy as jnp
from jax import lax
from jax.experimental import pallas as pl
from jax.experimental.pallas import tpu as pltpu

D_MODEL = 2048
BATCH = 2
SEQ = 8192
DEPTH = 4

GRID_W = 64
CTX_LEN = 256

HY_W = 1024
HY_ORDER = 2
HY_EMB = 33
HY_BANDS = (HY_EMB - 1) // 2
HY_HID = 64
HY_FAST_DECAY = 0.3
HY_SLOW_DECAY = 1.5
HY_TARGET = 1e-2

MLA_HEADS = 8
MLA_NOPE = 128
MLA_ROPE = 64
MLA_V = 128
MLA_Q_RANK = 512
MLA_KV_RANK = 256
MLA_SCALE = (MLA_NOPE + MLA_ROPE) ** -0.5
Q_BLOCK = 128
ROPE_BASE = 10000.0

O_HY = 3 * HY_W
O_Q = O_HY + MLA_Q_RANK
O_KV = O_Q + MLA_KV_RANK
IN_W = O_KV + MLA_ROPE

S5_GC = 16
S5_NG = D_MODEL // S5_GC
S5_P = 64
S5_GB = 16
S5_NBLK = S5_NG // S5_GB

PEER_HEADS = 8
PEER_NK = 128
PEER_N = PEER_NK * PEER_NK
PEER_DK = 256
PEER_TOPK = 16
PEER_BLOCK = 128

N_EVEN = (DEPTH + 1) // 2
N_ODD = DEPTH // 2
DN_ALPHA = (2.0 * DEPTH) ** 0.25
DN_BETA = (8.0 * DEPTH) ** -0.25
LN_EPS = 1e-5
RMS_EPS = 1e-6
F32 = jnp.float32


def _ln_kernel(x_ref, g_ref, b_ref, o_ref):
    xf = x_ref[...]
    xc = xf - jnp.mean(xf, axis=-1, keepdims=True)
    var = jnp.mean(xc * xc, axis=-1, keepdims=True)
    o_ref[...] = xc * lax.rsqrt(var + LN_EPS) * g_ref[...] + b_ref[...]


LN_TL = 256


def _ln_mod_kernel(a_ref, y_ref, gate_ref, g_ref, b_ref, shift_ref, scale_ref, h_ref, x_ref, *xt_ref):
    v = DN_ALPHA * a_ref[0] + gate_ref[0] * y_ref[0]
    vc = v - jnp.mean(v, axis=-1, keepdims=True)
    var = jnp.mean(vc * vc, axis=-1, keepdims=True)
    h = vc * lax.rsqrt(var + LN_EPS) * g_ref[...] + b_ref[...]
    h_ref[0] = h
    xm = h * (1.0 + scale_ref[0]) + shift_ref[0]
    x_ref[0] = xm
    if xt_ref:
        xt_ref[0][...] = xm.T.astype(BF16)


def ln_mod_pallas(a, y, gate, g, b, shift, scale, with_transpose):
    gn, t, d = a.shape
    tl = min(LN_TL, t)
    nt = t // tl
    tok = pl.BlockSpec((1, tl, d), lambda gi, i: (gi, i, 0))
    per_group = pl.BlockSpec((1, 1, d), lambda gi, i: (gi, 0, 0))
    vec = pl.BlockSpec((1, d), lambda gi, i: (0, 0))
    out_specs = [tok, tok]
    out_shape = [jax.ShapeDtypeStruct((gn, t, d), F32)] * 2
    if with_transpose:
        out_specs.append(pl.BlockSpec((d, tl), lambda gi, i: (0, gi * nt + i)))
        out_shape.append(jax.ShapeDtypeStruct((d, gn * t), BF16))
    return pl.pallas_call(
        _ln_mod_kernel,
        grid=(gn, nt),
        in_specs=[tok, tok, per_group, vec, vec, per_group, per_group],
        out_specs=out_specs,
        out_shape=out_shape,
        compiler_params=pltpu.CompilerParams(vmem_limit_bytes=VMEM_LIMIT),
        name="ln_modulate",
    )(a, y, gate, g.reshape(1, d), b.reshape(1, d), shift, scale)


def layer_norm_pallas(x, g, b):
    shp = x.shape
    x2 = x.reshape(-1, shp[-1])
    n, d = x2.shape
    tb = 256
    out = pl.pallas_call(
        _ln_kernel,
        grid=(n // tb,),
        in_specs=[pl.BlockSpec((tb, d), lambda i: (i, 0)),
                  pl.BlockSpec((1, d), lambda i: (0, 0)),
                  pl.BlockSpec((1, d), lambda i: (0, 0))],
        out_specs=pl.BlockSpec((tb, d), lambda i: (i, 0)),
        out_shape=jax.ShapeDtypeStruct((n, d), F32),
        name="layer_norm",
    )(x2, g.reshape(1, d), b.reshape(1, d))
    return out.reshape(shp)


BF16 = jnp.bfloat16
HI = lax.Precision.HIGHEST
VMEM_LIMIT = 56 * 1024 * 1024

S5_T = 16
S5_CB = 8


def _s5_operators(lam_re, lam_im, log_dt, b_re, b_im, c_re, c_im):
    T = S5_T
    ng, p, gc = b_re.shape[1:]
    dt = jnp.exp(log_dt)[..., None]
    zr, zi = lam_re * dt, lam_im * dt
    k = jnp.arange(T + 1, dtype=F32)[:, None, None, None]
    mag = jnp.exp(zr[None] * k)
    pr, pi = mag * jnp.cos(zi[None] * k), mag * jnp.sin(zi[None] * k)
    ar, ai = pr[1], pi[1]
    den = lam_re * lam_re + lam_im * lam_im
    cr = ((ar - 1.0) * lam_re + ai * lam_im) / den
    ci = (ai * lam_re - (ar - 1.0) * lam_im) / den
    bbr = cr[..., None] * b_re - ci[..., None] * b_im
    bbi = cr[..., None] * b_im + ci[..., None] * b_re
    car = c_re[None] * pr[:, :, :, None, :] - c_im[None] * pi[:, :, :, None, :]
    cai = c_re[None] * pi[:, :, :, None, :] + c_im[None] * pr[:, :, :, None, :]
    kk = (jnp.einsum('tdgcp,dgpe->tdgce', car[:T], bbr, precision=HI)
          - jnp.einsum('tdgcp,dgpe->tdgce', cai[:T], bbi, precision=HI))
    kf, kb = kk[:, 0], kk[:, 1]
    kfull = jnp.concatenate([kb[1:][::-1], (kf[0] + kb[0])[None], kf[1:]], axis=0)
    jj = jnp.arange(T)
    idx = jj[None, :] - jj[:, None] + (T - 1)
    m = kfull[idx]
    m = m.transpose(2, 0, 4, 1, 3).reshape(ng, T * gc, T * gc)
    def e_w(d, pw_idx):
        er = pr[pw_idx, d][..., None] * bbr[d][None] - pi[pw_idx, d][..., None] * bbi[d][None]
        ei = pr[pw_idx, d][..., None] * bbi[d][None] + pi[pw_idx, d][..., None] * bbr[d][None]
        w = jnp.concatenate([er, ei], axis=2)
        return w.transpose(1, 0, 3, 2).reshape(ng, T * gc, 2 * p)
    w_e = jnp.concatenate([e_w(0, T - 1 - jj), e_w(1, jj)], axis=-1)
    def s_w(d, pw_idx):
        w = jnp.concatenate([car[pw_idx, d], -cai[pw_idx, d]], axis=-1)
        return w.transpose(1, 3, 0, 2).reshape(ng, 2 * p, T * gc)
    w_y = jnp.concatenate([m, s_w(0, jj + 1), s_w(1, T - jj)], axis=1)
    a1 = jnp.concatenate([pr[T], pr[T]], axis=-1)
    a2 = jnp.concatenate([-pi[T], pi[T]], axis=-1)
    return w_e, w_y, a1, a2


def _s5_lane_group():
    return lax.broadcasted_iota(jnp.int32, (1, LANES), 1) // S5_GC


def _s5_row_block(n):
    return next(c for c in (32, 24, 16, 8) if n % c == 0)


def _s5_group_rows(x_ref, xg_ref, row0):
    nrow, T, _ = x_ref.shape
    per = LANES // S5_GC
    grp = _s5_lane_group()
    rb = _s5_row_block(nrow)

    def body(i, carry):
        rows = pl.ds(pl.multiple_of(i * rb, 8), rb)
        dst = pl.ds(pl.multiple_of(row0 + i * rb, 8), rb)
        for hf in range(T // per):
            accs = [None] * per
            for j in range(per):
                xt = x_ref[rows, hf * per + j, :]
                for g in range(per):
                    sh = ((j - g) * S5_GC) % LANES
                    r = pltpu.roll(xt, sh, 1) if sh else xt
                    accs[g] = r if accs[g] is None else jnp.where(grp == j, r, accs[g])
            for g in range(per):
                xg_ref[g, dst, hf * LANES:(hf + 1) * LANES] = accs[g]
        return carry

    lax.fori_loop(0, nrow // rb, body, 0)


def _s5_scatter_rows(yg_scr, o_ref, row0):
    nrow, T, _ = o_ref.shape
    per = LANES // S5_GC
    grp = _s5_lane_group()
    rb = _s5_row_block(nrow)

    def body(i, carry):
        rows = pl.ds(pl.multiple_of(i * rb, 8), rb)
        src = pl.ds(pl.multiple_of(row0 + i * rb, 8), rb)
        for t in range(T):
            hf, j = divmod(t, per)
            acc = None
            for g in range(per):
                piece = yg_scr[g, src, hf * LANES:(hf + 1) * LANES]
                sh = ((g - j) * S5_GC) % LANES
                r = pltpu.roll(piece, sh, 1) if sh else piece
                acc = r if acc is None else jnp.where(grp == g, r, acc)
            o_ref[rows, t, :] = acc
        return carry

    lax.fori_loop(0, nrow // rb, body, 0)


def _s5_end_state_kernel(xc_ref, xl_ref, w_ref, ef_ref, eb_ref, xg_ref):
    _s5_group_rows(xc_ref, xg_ref, 0)
    _s5_group_rows(xl_ref, xg_ref, xc_ref.shape[0])
    half = w_ref.shape[-1] // 2

    def body(g, carry):
        e = jnp.dot(xg_ref[g], w_ref[g], precision=HI, preferred_element_type=F32)
        cols = pl.ds(pl.multiple_of(g * half, LANES), half)
        ef_ref[:, cols] = e[:, :half]
        eb_ref[:, cols] = e[:, half:]
        return carry

    lax.fori_loop(0, xg_ref.shape[0], body, 0)


def _s5_carry_kernel(ef_ref, eb_ref, a1_ref, a2_ref, s0f_ref, s0b_ref, sf_scr, sb_scr):
    @pl.when(pl.program_id(1) == 0)
    def _():
        sf_scr[...] = jnp.zeros_like(sf_scr)
        sb_scr[...] = jnp.zeros_like(sb_scr)

    half = sf_scr.shape[-1] // 2

    def step(s, e, d):
        return a1_ref[d] * s + a2_ref[d] * pltpu.roll(s, half, 1) + e

    for r in range(S5_CB):
        s = sf_scr[...]
        s0f_ref[r] = s
        sf_scr[...] = step(s, ef_ref[r], 0)
    for r in reversed(range(S5_CB)):
        s = sb_scr[...]
        s0b_ref[r] = s
        sb_scr[...] = step(s, eb_ref[r], 1)


def _s5_output_kernel(xg_ref, s0f_ref, s0b_ref, w_ref, yc_ref, yl_ref, xg_scr):
    kx = xg_scr.shape[-1]
    ks = s0f_ref.shape[-1] // xg_scr.shape[0]

    def body(g, carry):
        cols = pl.ds(pl.multiple_of(g * ks, LANES), ks)
        y = jnp.dot(xg_ref[g], w_ref[g, :kx, :], precision=HI, preferred_element_type=F32)
        y = y + jnp.dot(s0f_ref[:, cols], w_ref[g, kx:kx + ks, :], precision=HI, preferred_element_type=F32)
        y = y + jnp.dot(s0b_ref[:, cols], w_ref[g, kx + ks:, :], precision=HI, preferred_element_type=F32)
        xg_scr[g] = y
        return carry

    lax.fori_loop(0, xg_scr.shape[0], body, 0)
    _s5_scatter_rows(xg_scr, yc_ref, 0)
    _s5_scatter_rows(xg_scr, yl_ref, yc_ref.shape[0])


def s5_scan(in_ctx, in_lat, lam_re, lam_im, log_dt, b_re, b_im, c_re, c_im):
    T = S5_T
    bn, lc, d = in_ctx.shape
    ll = in_lat.shape[1]
    ng, p, gc = b_re.shape[1:]
    w_e, w_y, a1, a2 = _s5_operators(lam_re, lam_im, log_dt, b_re, b_im, c_re, c_im)
    ncc, ncl = lc // T, ll // T
    nch = ncc + ncl
    rows = bn * nch
    kx, ks = T * gc, 2 * p
    per = LANES // gc
    ncb = ng // per
    ctx_spec = pl.BlockSpec((ncc, T, LANES), lambda c, r: (r, 0, c))
    lat_spec = pl.BlockSpec((ncl, T, LANES), lambda c, r: (r, 0, c))

    grouped = pl.BlockSpec((per, nch, kx), lambda c, r: (c, r, 0))
    ef, eb, xg = pl.pallas_call(
        _s5_end_state_kernel,
        grid=(ncb, bn),
        in_specs=[ctx_spec, lat_spec,
                  pl.BlockSpec((per, kx, 2 * ks), lambda c, r: (c, 0, 0))],
        out_specs=[pl.BlockSpec((nch, per * ks), lambda c, r: (r, c)),
                   pl.BlockSpec((nch, per * ks), lambda c, r: (r, c)),
                   grouped],
        out_shape=[jax.ShapeDtypeStruct((rows, ng * ks), F32)] * 2
        + [jax.ShapeDtypeStruct((ng, rows, kx), F32)],
        compiler_params=pltpu.CompilerParams(vmem_limit_bytes=VMEM_LIMIT),
        name="s5_end_state",
    )(in_ctx.reshape(bn * ncc, T, d), in_lat.reshape(bn * ncl, T, d), w_e)

    nblk = nch // S5_CB
    cblk = (lc // T) // S5_CB
    fwd_map = lambda b, j: (b * nblk + j, 0, 0)
    bwd_map = lambda b, j: (b * nblk + jnp.where(j < cblk, cblk - 1 - j, nblk + cblk - 1 - j), 0, 0)
    s0f, s0b = pl.pallas_call(
        _s5_carry_kernel,
        grid=(bn, nblk),
        in_specs=[pl.BlockSpec((S5_CB, ng, ks), fwd_map),
                  pl.BlockSpec((S5_CB, ng, ks), bwd_map),
                  pl.BlockSpec((2, ng, ks), lambda b, j: (0, 0, 0)),
                  pl.BlockSpec((2, ng, ks), lambda b, j: (0, 0, 0))],
        out_specs=[pl.BlockSpec((S5_CB, ng, ks), fwd_map),
                   pl.BlockSpec((S5_CB, ng, ks), bwd_map)],
        out_shape=[jax.ShapeDtypeStruct((rows, ng, ks), F32)] * 2,
        scratch_shapes=[pltpu.VMEM((ng, ks), F32), pltpu.VMEM((ng, ks), F32)],
        name="s5_carry",
    )(ef.reshape(rows, ng, ks), eb.reshape(rows, ng, ks), a1, a2)

    y_ctx, y_lat = pl.pallas_call(
        _s5_output_kernel,
        grid=(ncb, bn),
        in_specs=[grouped,
                  pl.BlockSpec((nch, per * ks), lambda c, r: (r, c)),
                  pl.BlockSpec((nch, per * ks), lambda c, r: (r, c)),
                  pl.BlockSpec((per, kx + 2 * ks, kx), lambda c, r: (c, 0, 0))],
        out_specs=[ctx_spec, lat_spec],
        out_shape=[jax.ShapeDtypeStruct((bn * ncc, T, d), F32), jax.ShapeDtypeStruct((bn * ncl, T, d), F32)],
        scratch_shapes=[pltpu.VMEM((per, nch, kx), F32)],
        compiler_params=pltpu.CompilerParams(vmem_limit_bytes=VMEM_LIMIT),
        name="s5_output",
    )(xg, s0f.reshape(rows, ng * ks), s0b.reshape(rows, ng * ks), w_y)
    return y_ctx.reshape(bn, lc, d), y_lat.reshape(bn, ll, d)


PEER_TS = 256
PEER_TB = 512
PEER_EB = 1024
LANES = 128
NEG = -3.0e38
_NN = (((1,), (0,)), ((), ()))
_NT = (((1,), (1,)), ((), ()))


def _split_bf16(a):
    hi = a.astype(BF16)
    return hi, (a - hi.astype(F32)).astype(BF16)


def _dot3(ah, al, bh, bl, dims):
    f = lambda a, b: lax.dot_general(a, b, dims, preferred_element_type=F32)
    return f(ah, bh) + f(al, bh) + f(ah, bl)


def _peer_cells():
    return PEER_TOPK + 8 * (PEER_TOPK // 2 - 1) + 8


def _peer_score_kernel(x_ref, wq_ref, kh_ref, kl_ref,
                       n1_ref, w1_ref, r2_ref, e2_ref,
                       st_scr, a1_scr, a2_scr, c_scr):
    nh, _, nk, dk = kh_ref.shape
    tb = x_ref.shape[0]
    q = jnp.dot(x_ref[...].astype(BF16), wq_ref[...], preferred_element_type=F32)
    qh, ql = _split_bf16(q)
    for h in range(nh):
        for s in range(2):
            c0 = (2 * h + s) * dk
            st_scr[2 * h + s] = _dot3(kh_ref[h, s], kl_ref[h, s],
                                      qh[:, c0:c0 + dk], ql[:, c0:c0 + dk], _NT)

    def top_values(s, out_scr):
        rank = jnp.full(s.shape, float(PEER_TOPK), F32)
        for r in range(PEER_TOPK):
            m = jnp.max(s, axis=0, keepdims=True)
            out_scr[r:r + 1, :] = m
            hit = s == m
            rank = jnp.where(hit, float(r), rank)
            s = jnp.where(hit, NEG, s)
        return rank

    nsub = tb // LANES
    half = PEER_TOPK // 2

    def body(it, carry):
        h = it // nsub
        lanes = pl.ds(pl.multiple_of((it % nsub) * LANES, LANES), LANES)
        s1 = st_scr[2 * h, :, lanes]
        s2 = st_scr[2 * h + 1, :, lanes]
        top_values(s1, a1_scr)
        r2 = top_values(s2, a2_scr)
        a1 = a1_scr[...]
        a2 = a2_scr[...]
        c_scr[0:PEER_TOPK, :] = a1[0:1] + a2
        for k in range(1, half):
            c_scr[PEER_TOPK + 8 * (k - 1):PEER_TOPK + 8 * k, :] = a1[k:k + 1] + a2[0:8]
        c_scr[PEER_TOPK + 8 * (half - 1):, :] = a1[half:] + a2[0:1]
        c = c_scr[...]
        m0 = a1[0:1] + a2[0:1]
        z = jnp.zeros_like(m0)
        tau = m0
        for r in range(PEER_TOPK):
            tau = jnp.max(c, axis=0, keepdims=True)
            z = z + jnp.exp(tau - m0)
            c = jnp.where(c == tau, NEG, c)
        n1 = jnp.zeros_like(s1)
        for l in range(PEER_TOPK):
            n1 = n1 + jnp.where(s1 + a2[l:l + 1] >= tau, 1.0, 0.0)
        n1_ref[h, :, lanes] = n1
        w1_ref[h, :, lanes] = jnp.exp(s1 - a1[0:1]) / z
        r2_ref[h, :, lanes] = pltpu.bitcast(r2.astype(BF16), jnp.uint32)
        e2_ref[h, :, lanes] = pltpu.bitcast(jnp.exp(s2 - a2[0:1]).astype(BF16), jnp.uint32)
        return carry

    lax.fori_loop(0, nh * nsub, body, 0)


def _peer_expert_kernel(xt_ref, u_ref, vt_ref, n1_ref, w1_ref, r2_ref, e2_ref,
                        o_ref, acc_scr, a_scr, p_scr):
    e = pl.program_id(1)
    last = pl.num_programs(1) - 1
    slot = e % 2
    nh, nk, tb = r2_ref.shape[0], 2 * r2_ref.shape[1], r2_ref.shape[2]
    ni = a_scr.shape[0] // nk

    @pl.when(e == 0)
    def _():
        acc_scr[...] = jnp.zeros_like(acc_scr)
        p_scr[1] = jnp.zeros(p_scr.shape[1:], BF16)

    @pl.when(e < last)
    def _():
        acc_scr[...] += jnp.dot(vt_ref[...], p_scr[1 - slot], preferred_element_type=F32)
        a_scr[...] = jnp.dot(u_ref[...], xt_ref[...], preferred_element_type=F32)
        for ts in range(tb // LANES):
            lanes = slice(ts * LANES, (ts + 1) * LANES)
            for ii in range(ni):
                rows = slice(ii * nk, (ii + 1) * nk)
                g = jnp.zeros((nk, LANES), BF16)
                for h in range(nh):
                    n1row = jnp.broadcast_to(n1_ref[h, ii:ii + 1, lanes].astype(BF16), (nk, LANES))
                    w1row = jnp.broadcast_to(w1_ref[h, ii:ii + 1, lanes].astype(BF16), (nk, LANES))
                    r2 = pltpu.bitcast(r2_ref[h, :, lanes], BF16)
                    e2 = pltpu.bitcast(e2_ref[h, :, lanes], BF16)
                    g = g + jnp.where(r2 < n1row, e2, jnp.zeros((), BF16)) * w1row
                p_scr[slot, rows, lanes] = g * jax.nn.gelu(a_scr[rows, lanes].astype(BF16))

    @pl.when(e == last)
    def _():
        acc = acc_scr[...] + jnp.dot(vt_ref[...], p_scr[1 - slot], preferred_element_type=F32)
        o_ref[...] = acc.T


def peer_ffn_pallas(xin, xin_t, w_q, keys, u_bf, vt_bf):
    n, d = xin.shape
    nh, _, nk, dk = keys.shape
    ne = u_bf.shape[0]
    ts = min(PEER_TS, n)
    tb = min(PEER_TB, n)
    eb = min(PEER_EB, ne)
    wq = w_q.astype(BF16)
    kh, kl = _split_bf16(keys)
    feat32 = jax.ShapeDtypeStruct((nh, nk, n), F32)
    feat16 = jax.ShapeDtypeStruct((nh, nk // 2, n), jnp.uint32)
    pack_spec = pl.BlockSpec((nh, nk // 2, ts), lambda i: (0, 0, i))
    feat_spec = pl.BlockSpec((nh, nk, ts), lambda i: (0, 0, i))
    n1, w1, r2, e2 = pl.pallas_call(
        _peer_score_kernel,
        grid=(n // ts,),
        in_specs=[pl.BlockSpec((ts, d), lambda i: (i, 0)),
                  pl.BlockSpec(wq.shape, lambda i: (0, 0)),
                  pl.BlockSpec(kh.shape, lambda i: (0, 0, 0, 0)),
                  pl.BlockSpec(kl.shape, lambda i: (0, 0, 0, 0))],
        out_specs=[feat_spec, feat_spec, pack_spec, pack_spec],
        out_shape=[feat32, feat32, feat16, feat16],
        scratch_shapes=[pltpu.VMEM((2 * nh, nk, ts), F32),
                        pltpu.VMEM((PEER_TOPK, LANES), F32),
                        pltpu.VMEM((PEER_TOPK, LANES), F32),
                        pltpu.VMEM((_peer_cells(), LANES), F32)],
        compiler_params=pltpu.CompilerParams(vmem_limit_bytes=VMEM_LIMIT),
        name="peer_scores",
    )(xin, wq, kh, kl)

    ni = eb // nk
    nblk = ne // eb
    row_spec = pl.BlockSpec((nh, ni, tb), lambda t, e: (0, jnp.minimum(e, nblk - 1), t))
    col_spec = pl.BlockSpec((nh, nk // 2, tb), lambda t, e: (0, 0, t))
    return pl.pallas_call(
        _peer_expert_kernel,
        grid=(n // tb, nblk + 1),
        in_specs=[pl.BlockSpec((d, tb), lambda t, e: (0, t)),
                  pl.BlockSpec((eb, d), lambda t, e: (jnp.minimum(e, nblk - 1), 0)),
                  pl.BlockSpec((d, eb), lambda t, e: (0, jnp.maximum(e - 1, 0))),
                  row_spec, row_spec, col_spec, col_spec],
        out_specs=pl.BlockSpec((tb, d), lambda t, e: (t, 0)),
        out_shape=jax.ShapeDtypeStruct((n, d), F32),
        scratch_shapes=[pltpu.VMEM((d, tb), F32),
                        pltpu.VMEM((eb, tb), F32),
                        pltpu.VMEM((2, eb, tb), BF16)],
        compiler_params=pltpu.CompilerParams(vmem_limit_bytes=VMEM_LIMIT,
                                             dimension_semantics=("arbitrary", "arbitrary")),
        name="peer_experts",
    )(xin_t, u_bf, vt_bf, n1, w1, r2, e2)


MM_TM = 512
MM_TN = 512


def _row_tile(m):
    return next(c for c in (MM_TM, 256, 128, 64, 32, 16, 8) if m % c == 0)


def _mm_kernel(x_ref, w_ref, o_ref):
    o_ref[...] = jnp.dot(x_ref[...].astype(BF16), w_ref[...], preferred_element_type=F32)


def _glu_kernel(x_ref, wo_ref, wg_ref, o_ref):
    xb = x_ref[...].astype(BF16)
    a = jnp.dot(xb, wo_ref[...], preferred_element_type=F32)
    g = jnp.dot(xb, wg_ref[...], preferred_element_type=F32)
    o_ref[...] = a * jax.nn.sigmoid(g)


def _mm_call(body, x, ws, name):
    m, k = x.shape
    n = ws[0].shape[1]
    tm = _row_tile(m)
    tn = min(MM_TN, n)
    return pl.pallas_call(
        body,
        grid=(m // tm, n // tn),
        in_specs=[pl.BlockSpec((tm, k), lambda i, j: (i, 0))]
        + [pl.BlockSpec((k, tn), lambda i, j: (0, j)) for _ in ws],
        out_specs=pl.BlockSpec((tm, tn), lambda i, j: (i, j)),
        out_shape=jax.ShapeDtypeStruct((m, n), F32),
        compiler_params=pltpu.CompilerParams(vmem_limit_bytes=VMEM_LIMIT),
        name=name,
    )(x, *ws)


def matmul_pallas(x, w):
    n = w.shape[1]
    npad = -n % MM_TN if n > MM_TN else -n % LANES
    wb = jnp.pad(w.astype(BF16), ((0, 0), (0, npad)))
    out = _mm_call(_mm_kernel, x, [wb], "dense_proj")
    return out[:, :n] if npad else out


def _mm_pair_kernel(x1_ref, x2_ref, w1_ref, w2_ref, o_ref):
    o_ref[...] = (jnp.dot(x1_ref[...].astype(BF16), w1_ref[...], preferred_element_type=F32)
                  + jnp.dot(x2_ref[...].astype(BF16), w2_ref[...], preferred_element_type=F32))


def matmul_pair_pallas(x1, x2, w):
    m, k1 = x1.shape
    k2 = x2.shape[1]
    n = w.shape[1]
    tm, tn = _row_tile(m), min(MM_TN, n)
    wb = w.astype(BF16)
    return pl.pallas_call(
        _mm_pair_kernel,
        grid=(m // tm, n // tn),
        in_specs=[pl.BlockSpec((tm, k1), lambda i, j: (i, 0)),
                  pl.BlockSpec((tm, k2), lambda i, j: (i, 0)),
                  pl.BlockSpec((k1, tn), lambda i, j: (0, j)),
                  pl.BlockSpec((k2, tn), lambda i, j: (0, j))],
        out_specs=pl.BlockSpec((tm, tn), lambda i, j: (i, j)),
        out_shape=jax.ShapeDtypeStruct((m, n), F32),
        compiler_params=pltpu.CompilerParams(vmem_limit_bytes=VMEM_LIMIT),
        name="dense_proj_pair",
    )(x1, x2, wb[:k1], wb[k1:])


def glu_pallas(x, w_o, w_g):
    return _mm_call(_glu_kernel, x, [w_o.astype(BF16), w_g.astype(BF16)], "glu_proj")


ATT_TQ = 512
ATT_TK = 2816


def _attn_kernel(qn_ref, qr_ref, cos_ref, sin_ref, knt_ref, krt_ref, v_ref, o_ref, *, tk):
    x = qr_ref[0]
    lane = lax.broadcasted_iota(jnp.int32, (1, LANES), 1)
    quarter = MLA_ROPE // 4
    rot = jnp.where(lane % (2 * quarter) < quarter,
                    -pltpu.roll(x, LANES - quarter, 1), pltpu.roll(x, quarter, 1))
    qr = x * cos_ref[...] + rot * sin_ref[...]
    q = jnp.concatenate([qn_ref[0].astype(BF16), qr.astype(BF16)], axis=-1)
    tq = q.shape[0]
    nk = knt_ref.shape[-1]
    m = jnp.full((tq, 1), NEG, F32)
    l = jnp.zeros((tq, 1), F32)
    acc = jnp.zeros((tq, v_ref.shape[-1]), F32)
    for c in range(nk // tk):
        kt = jnp.concatenate([knt_ref[0, 0, :, c * tk:(c + 1) * tk],
                              krt_ref[0, :, c * tk:(c + 1) * tk]], axis=0)
        s = jnp.dot(q, kt, preferred_element_type=F32)
        m_new = jnp.maximum(m, jnp.max(s, axis=-1, keepdims=True))
        alpha = jnp.exp(m - m_new)
        p = jnp.exp(s - m_new)
        l = alpha * l + jnp.sum(p, axis=-1, keepdims=True)
        acc = alpha * acc + jnp.dot(p.astype(BF16), v_ref[0, 0, c * tk:(c + 1) * tk, :],
                                    preferred_element_type=F32)
        m = m_new
    o_ref[0] = acc / l


def _mla_query_weights(w_uq):
    r = w_uq.shape[0]
    w = w_uq.reshape(r, MLA_HEADS, MLA_NOPE + MLA_ROPE) * MLA_SCALE
    rope = jnp.pad(w[..., MLA_NOPE:], ((0, 0), (0, 0), (0, LANES - MLA_ROPE)))
    return jnp.concatenate([w[..., :MLA_NOPE].reshape(r, -1), rope.reshape(r, -1)], axis=1)


def mla_attention_pallas(q2, cos, sin, kn, kr, v):
    bn, lq, _ = q2.shape
    sk, nh = kn.shape[1], kn.shape[2]
    dv = v.shape[-1]
    pad = ((0, 0), (0, LANES - cos.shape[-1]))
    cosp, sinp = jnp.pad(cos, pad), jnp.pad(sin, pad)
    knt = kn.astype(BF16).transpose(0, 2, 3, 1)
    krt = jnp.pad(kr.astype(BF16).transpose(0, 2, 1), ((0, 0), (0, LANES - kr.shape[-1]), (0, 0)))
    vv = v.astype(BF16).transpose(0, 2, 1, 3)
    tq = min(ATT_TQ, lq)
    tk = ATT_TK if sk % ATT_TK == 0 else sk
    return pl.pallas_call(
        functools.partial(_attn_kernel, tk=tk),
        grid=(bn, nh, lq // tq),
        in_specs=[pl.BlockSpec((1, tq, LANES), lambda b, h, i: (b, i, h)),
                  pl.BlockSpec((1, tq, LANES), lambda b, h, i: (b, i, nh + h)),
                  pl.BlockSpec((tq, LANES), lambda b, h, i: (i, 0)),
                  pl.BlockSpec((tq, LANES), lambda b, h, i: (i, 0)),
                  pl.BlockSpec((1, 1, kn.shape[-1], sk), lambda b, h, i: (b, h, 0, 0)),
                  pl.BlockSpec((1, LANES, sk), lambda b, h, i: (b, 0, 0)),
                  pl.BlockSpec((1, 1, sk, dv), lambda b, h, i: (b, h, 0, 0))],
        out_specs=pl.BlockSpec((1, tq, dv), lambda b, h, i: (b, i, h)),
        out_shape=jax.ShapeDtypeStruct((bn, lq, nh * dv), F32),
        compiler_params=pltpu.CompilerParams(vmem_limit_bytes=VMEM_LIMIT),
        name="mla_attention",
    )(q2, q2, cosp, sinp, knt, krt, vv)


def layer_norm(x, g, b):
    xf = x.astype(F32)
    xc = xf - jnp.mean(xf, axis=-1, keepdims=True)
    var = jnp.mean(xc * xc, axis=-1, keepdims=True)
    return (xc * lax.rsqrt(var + LN_EPS) * g + b).astype(x.dtype)


def rms_norm(x, g):
    xf = x.astype(F32)
    return (xf * lax.rsqrt(jnp.mean(xf * xf, axis=-1, keepdims=True) + RMS_EPS) * g).astype(x.dtype)


def axial_rope(L):
    rows = L // GRID_W
    row = jnp.repeat(jnp.arange(rows), GRID_W).astype(F32)
    col = jnp.tile(jnp.arange(GRID_W), rows).astype(F32)
    half = MLA_ROPE // 2
    inv = ROPE_BASE ** (-jnp.arange(0, half, 2, dtype=F32) / half)
    ar, ac = row[:, None] * inv, col[:, None] * inv
    ang = jnp.concatenate([ar, ar, ac, ac], axis=-1)
    return jnp.cos(ang), jnp.sin(ang)


def apply_axial_rope(x, cos, sin):
    xs = x.reshape(x.shape[:-1] + (2, 2, MLA_ROPE // 4))
    rot = jnp.concatenate([-xs[..., 1:, :], xs[..., :1, :]], axis=-2).reshape(x.shape)
    return (x * cos + rot * sin).astype(x.dtype)


def short_conv3(z, w, b):
    L = z.shape[1]
    zp = jnp.pad(z, ((0, 0), (1, 1), (0, 0)))
    return zp[:, :L] * w[0] + zp[:, 1:L + 1] * w[1] + zp[:, 2:] * w[2] + b


def hyena_pos_features(L):
    t = jnp.linspace(0.0, 1.0, L, dtype=F32)[:, None]
    w = (2.0 * math.pi / L) * jnp.arange(L, dtype=F32)[:, None]
    f = jnp.linspace(1e-4, HY_BANDS - 1, HY_BANDS, dtype=F32)[None, :]
    return jnp.concatenate([t, jnp.cos(f * w), -jnp.sin(f * w)], axis=-1)


def hyena_filter_spectrum(L, w1, b1, w2, b2, w3, b3, w4, freq):
    act = lambda a: jnp.sin(freq * a)
    hdn = act(hyena_pos_features(L) @ w1 + b1)
    hdn = act(hdn @ w2 + b2)
    hdn = act(hdn @ w3 + b3)
    deltas = jnp.abs(jnp.linspace(math.log(HY_TARGET) / HY_SLOW_DECAY, math.log(HY_TARGET) / HY_FAST_DECAY,
                                  HY_ORDER * HY_W, dtype=F32)).reshape(HY_ORDER, HY_W)
    t = jnp.linspace(0.0, 1.0, L, dtype=F32)[:, None, None]
    w4d = w4.reshape(w4.shape[0], 2, HY_ORDER * HY_W)
    hf = (hdn @ w4d[:, 0]).astype(F32).reshape(L, HY_ORDER, HY_W) * jnp.exp(-t * deltas)
    hb = (hdn[::-1] @ w4d[:, 1]).astype(F32).reshape(L, HY_ORDER, HY_W) * jnp.exp(-t[::-1] * deltas)
    norm = jnp.sum(jnp.abs(hf), axis=0) + jnp.sum(jnp.abs(hb), axis=0)
    return jnp.concatenate([hf / norm, jnp.zeros((1, HY_ORDER, HY_W), F32), hb[:L - 1] / norm], axis=0)


def fft_long_conv(u, k_f):
    L = u.shape[1]
    u_f = jnp.fft.rfft(u.astype(F32), n=2 * L, axis=1)
    return jnp.fft.irfft(u_f * k_f[None], n=2 * L, axis=1)[:, :L]


FFT_R = 128
FFT_CB = 16


def _dft_constants():
    r = FFT_R
    n = r * r
    idx = np.arange(r)
    f = np.exp(-2j * np.pi * np.outer(idx, idx) / r)
    tw = np.exp(-2j * np.pi * np.outer(idx, idx) / n)
    blk = lambda w: np.block([[w.real, w.imag], [-w.imag, w.real]])
    mats = {
        'w1': blk(f[:r // 2, :]),
        'w1f': np.concatenate([f.real, f.imag], 1),
        'w2': blk(f),
        'w2c': blk(np.conj(f)),
        'w1c': blk(np.conj(f)[:, :r // 2]) / n,
    }
    out = {}
    for name, w in mats.items():
        w32 = jnp.asarray(w, F32)
        out[name] = _split_bf16(w32)
    out['twa'] = jnp.asarray(np.concatenate([tw.real, tw.real], 1), F32)
    out['twb'] = jnp.asarray(np.concatenate([-tw.imag, tw.imag], 1), F32)
    out['twbc'] = jnp.asarray(np.concatenate([tw.imag, -tw.imag], 1), F32)
    return out


def _cmm(x, wh_ref, wl_ref):
    xh, xl = _split_bf16(x)
    return _dot3(xh, xl, wh_ref[...], wl_ref[...], _NN)


def _swap_halves(x):
    h = x.shape[-1] // 2
    return jnp.concatenate([x[..., h:], x[..., :h]], axis=-1)


def _cmul(x, a, b):
    return x * a + _swap_halves(x) * b


def _tile_transpose(x):
    h = x.shape[-1] // 2
    return jnp.concatenate([jnp.swapaxes(x[..., :h], 1, 2), jnp.swapaxes(x[..., h:], 1, 2)], axis=-1)


def _fft_forward(z2d, cb, w1h, w1l, w2h, w2l, twa, twb):
    r = FFT_R
    x = _cmm(z2d, w1h, w1l).reshape(cb, r, 2 * r)
    x = _tile_transpose(_cmul(x, twa[...], twb[...]))
    return _cmm(x.reshape(cb * r, 2 * r), w2h, w2l).reshape(cb, r, 2 * r)


def _hy_filter_kernel(k_ref, w1h, w1l, w2h, w2l, twa, twb, o_ref):
    cb = k_ref.shape[0]
    o_ref[...] = _fft_forward(k_ref[...].reshape(cb * FFT_R, FFT_R), cb, w1h, w1l, w2h, w2l, twa, twb)


def _hy_conv_kernel(z_ref, k_ref, w1h, w1l, w2h, w2l, w2ch, w2cl, w1ch, w1cl, twa, twb, twbc, o_ref):
    cb = z_ref.shape[0]
    r = FFT_R
    x = _fft_forward(z_ref[...].reshape(cb * r, r), cb, w1h, w1l, w2h, w2l, twa, twb)
    k = k_ref[...]
    kr, ki = k[..., :r], k[..., r:]
    x = _cmul(x, jnp.concatenate([kr, kr], axis=-1), jnp.concatenate([-ki, ki], axis=-1))
    x = _cmm(x.reshape(cb * r, 2 * r), w2ch, w2cl).reshape(cb, r, 2 * r)
    x = _tile_transpose(_cmul(x, twa[...], twbc[...]))
    o_ref[...] = _cmm(x.reshape(cb * r, 2 * r), w1ch, w1cl).reshape(cb, r, r)


def _const_spec(a):
    return pl.BlockSpec(a.shape, lambda i: (0,) * a.ndim)


def hyena_filter_fft(k_time, consts):
    r = FFT_R
    c = k_time.shape[1]
    kt = k_time.reshape(r, r, c).transpose(2, 1, 0)
    ws = [*consts['w1f'], *consts['w2'], consts['twa'], consts['twb']]
    return pl.pallas_call(
        _hy_filter_kernel,
        grid=(c // FFT_CB,),
        in_specs=[pl.BlockSpec((FFT_CB, r, r), lambda i: (i, 0, 0))] + [_const_spec(w) for w in ws],
        out_specs=pl.BlockSpec((FFT_CB, r, 2 * r), lambda i: (i, 0, 0)),
        out_shape=jax.ShapeDtypeStruct((c, r, 2 * r), F32),
        compiler_params=pltpu.CompilerParams(vmem_limit_bytes=VMEM_LIMIT),
        name="hyena_filter_fft",
    )(kt, *ws)


def hyena_long_conv(u, k_spec, consts):
    r = FFT_R
    bn, ll, c = u.shape
    assert bn == 2 and 2 * ll == r * r
    z = u.reshape(2, r // 2, r, c).transpose(3, 2, 0, 1).reshape(c, r, r)
    ws = [*consts['w1'], *consts['w2'], *consts['w2c'], *consts['w1c'],
          consts['twa'], consts['twb'], consts['twbc']]
    y = pl.pallas_call(
        _hy_conv_kernel,
        grid=(c // FFT_CB,),
        in_specs=[pl.BlockSpec((FFT_CB, r, r), lambda i: (i, 0, 0)),
                  pl.BlockSpec((FFT_CB, r, 2 * r), lambda i: (i, 0, 0))] + [_const_spec(w) for w in ws],
        out_specs=pl.BlockSpec((FFT_CB, r, r), lambda i: (i, 0, 0)),
        out_shape=jax.ShapeDtypeStruct((c, r, r), F32),
        compiler_params=pltpu.CompilerParams(vmem_limit_bytes=VMEM_LIMIT),
        name="hyena_long_conv",
    )(z, k_spec, *ws)
    return y.reshape(c, r, 2, r // 2).transpose(2, 3, 1, 0).reshape(2, ll, c)


def hyena_mixer(z, conv_w, conv_b, w1, b1, w2, b2, w3, b3, w4, freq, bias):
    L = z.shape[1]
    zc = short_conv3(z, conv_w, conv_b)
    x1, x2, v = jnp.split(zc, 3, axis=-1)
    k_t = hyena_filter_spectrum(L, w1, b1, w2, b2, w3, b3, w4, freq)
    y = v.astype(F32)
    use_mxu_fft = z.shape[0] == 2 and 2 * L == FFT_R * FFT_R
    if use_mxu_fft:
        consts = _dft_constants()
    else:
        k_f = jnp.fft.rfft(k_t, axis=0)
    for o, gate in enumerate((x1, x2)):
        if use_mxu_fft:
            conv = hyena_long_conv(y, hyena_filter_fft(k_t[:, o], consts), consts)
        else:
            conv = fft_long_conv(y, k_f[:, o])
        y = gate * (conv + y * bias[o])
    return y.astype(z.dtype)


def _mm3(x, w):
    b, l, k = x.shape
    return matmul_pallas(x.reshape(b * l, k), w).reshape(b, l, w.shape[1])


def even_mixer(in_ctx, in_lat, cos, sin, need_ctx, w_in, hy, mla, w_o):
    q_norm, w_uq, kv_norm, w_ukv = mla
    bn, lc, _ = in_ctx.shape
    w_hy, w_att = w_in[:, :O_HY], w_in[:, O_HY:]
    nq, nkv = O_Q - O_HY, O_KV - O_HY
    zc, zl = _mm3(in_ctx, w_att), _mm3(in_lat, w_att)
    kv_lat = jnp.concatenate([zc[..., nq:nkv], zl[..., nq:nkv]], axis=1)
    kv = _mm3(rms_norm(kv_lat, kv_norm), w_ukv).reshape(bn, kv_lat.shape[1], MLA_HEADS, MLA_NOPE + MLA_V)
    kn, vv = kv[..., :MLA_NOPE], kv[..., MLA_NOPE:]
    kr = jnp.concatenate([zc[..., nkv:], apply_axial_rope(zl[..., nkv:], cos, sin)], axis=1)
    w_q2 = _mla_query_weights(w_uq)

    def out_proj(hyena, att):
        b, l, k = hyena.shape
        return matmul_pair_pallas(hyena.reshape(b * l, k), att.reshape(b * l, -1), w_o).reshape(b, l, -1)

    att = mla_attention_pallas(_mm3(rms_norm(zl[..., :nq], q_norm), w_q2), cos, sin, kn, kr, vv)
    y_lat = out_proj(hyena_mixer(_mm3(in_lat, w_hy), *hy), att)
    y_ctx = None
    if need_ctx:
        att_c = mla_attention_pallas(_mm3(rms_norm(zc[..., :nq], q_norm), w_q2),
                                     jnp.ones((lc, MLA_ROPE), F32), jnp.zeros((lc, MLA_ROPE), F32),
                                     kn[:, :lc], kr[:, :lc], vv[:, :lc])
        y_ctx = out_proj(hyena_mixer(_mm3(in_ctx, w_hy), *hy), att_c)
    return y_ctx, y_lat


def _scan_op(e1, e2):
    a1, b1 = e1
    a2, b2 = e2
    return a1 * a2, a2 * b1 + b2


def linear_scan(a_bar, bu, s0):
    bu = bu.at[:, 0].add(a_bar * s0)
    a = jnp.broadcast_to(a_bar, bu.shape)
    _, s = lax.associative_scan(_scan_op, (a, bu), axis=1)
    return s


def _rev(s, d):
    return s[:, ::-1] if d == 1 else s


def s5_mixer(in_ctx, in_lat, need_ctx, lam_re, lam_im, log_dt, b_re, b_im, c_re, c_im, d_skip, w_o, w_g):
    s_ctx, s_lat = s5_scan(in_ctx, in_lat, lam_re, lam_im, log_dt, b_re, b_im, c_re, c_im)

    def finish(ys, h):
        y = ys + d_skip * h.astype(F32)
        y = jax.nn.gelu(y).astype(h.dtype)
        b, l, dm = y.shape
        return glu_pallas(y.reshape(b * l, dm), w_o, w_g).reshape(b, l, dm)

    y_lat = finish(s_lat, in_lat)
    y_ctx = finish(s_ctx, in_ctx) if need_ctx else None
    return y_ctx, y_lat


def peer_ffn(h, w_q, keys, u_tab, v_tab):
    Bn, L, D = h.shape
    hb = h.reshape(Bn * L // PEER_BLOCK, PEER_BLOCK, D)

    def block_fn(xb):
        q = (xb @ w_q).reshape(PEER_BLOCK, PEER_HEADS, 2, PEER_DK // 2)
        s = jnp.einsum('thsk,hsnk->thsn', q, keys).astype(F32)
        top_s, top_i = lax.top_k(s, PEER_TOPK)
        cand = top_s[:, :, 0, :, None] + top_s[:, :, 1, None, :]
        cand_s, cand_j = lax.top_k(cand.reshape(PEER_BLOCK, PEER_HEADS, PEER_TOPK * PEER_TOPK), PEER_TOPK)
        idx = (jnp.take_along_axis(top_i[:, :, 0], cand_j // PEER_TOPK, axis=-1) * PEER_NK
               + jnp.take_along_axis(top_i[:, :, 1], cand_j % PEER_TOPK, axis=-1))
        g = jax.nn.softmax(cand_s, axis=-1)
        act = jax.nn.gelu(jnp.einsum('thkd,td->thk', u_tab[idx], xb).astype(F32))
        return jnp.einsum('thk,thkd->td', (g * act).astype(xb.dtype), v_tab[idx])

    return lax.map(block_fn, hb).reshape(Bn, L, D)


def kernel(x, c, ctx, c_ctx, mod_w, mod_b, ln_mix_g, ln_mix_b, ln_ffn_g, ln_ffn_b,
           ev_w_in, ev_conv_w, ev_conv_b, hy_w1, hy_b1, hy_w2, hy_b2, hy_w3, hy_b3, hy_w4,
           hy_freq, hy_bias, mla_q_norm, mla_w_uq, mla_kv_norm, mla_w_ukv, ev_w_o,
           s5_lam_re, s5_lam_im, s5_log_dt, s5_b_re, s5_b_im, s5_c_re, s5_c_im, s5_d,
           od_w_o, od_w_g, peer_w_q, peer_keys, peer_u, peer_v):
    cos, sin = axial_rope(x.shape[1])
    act_lat = jax.nn.silu(c)
    act_ctx = jax.nn.silu(c_ctx)
    bn, ll, dm = x.shape
    lc = ctx.shape[1]
    mods_lat = [jnp.split((act_lat @ mod_w[l] + mod_b[l])[:, None, :], 6, axis=-1) for l in range(DEPTH)]
    mods_ctx = [jnp.split((act_ctx @ mod_w[l] + mod_b[l])[None, None, :], 6, axis=-1) for l in range(DEPTH)]
    h_lat = x
    h_ctx = ctx.reshape(1, bn * lc, dm)
    in_lat = h_lat * (1.0 + mods_lat[0][1]) + mods_lat[0][0]
    in_ctx = (h_ctx * (1.0 + mods_ctx[0][1]) + mods_ctx[0][0]).reshape(bn, lc, dm)
    zero_lat = jnp.zeros_like(mods_lat[0][0])
    zero_ctx = jnp.zeros_like(mods_ctx[0][0])
    for layer in range(DEPTH):
        need_ctx = layer < DEPTH - 1
        m_lat, m_ctx = mods_lat[layer], mods_ctx[layer]
        i = layer // 2
        if layer % 2 == 0:
            hy = (ev_conv_w[i], ev_conv_b[i], hy_w1[i], hy_b1[i], hy_w2[i], hy_b2[i], hy_w3[i], hy_b3[i],
                  hy_w4[i], hy_freq[i], hy_bias[i])
            mla = (mla_q_norm[i], mla_w_uq[i], mla_kv_norm[i], mla_w_ukv[i])
            y_ctx, y_lat = even_mixer(in_ctx, in_lat, cos, sin, need_ctx, ev_w_in[i], hy, mla, ev_w_o[i])
        else:
            y_ctx, y_lat = s5_mixer(in_ctx, in_lat, need_ctx, s5_lam_re[i], s5_lam_im[i], s5_log_dt[i],
                                    s5_b_re[i], s5_b_im[i], s5_c_re[i], s5_c_im[i], s5_d[i],
                                    od_w_o[i], od_w_g[i])
        u_bf = peer_u[layer].astype(BF16)
        vt_bf = peer_v[layer].T.astype(BF16)
        peer = lambda xm, xt: peer_ffn_pallas(xm.reshape(-1, dm), xt, peer_w_q[layer], peer_keys[layer],
                                              u_bf, vt_bf).reshape(xm.shape)
        nxt_lat = (mods_lat[layer + 1][0], mods_lat[layer + 1][1]) if need_ctx else (zero_lat, zero_lat)
        h_lat, xin, xin_t = ln_mod_pallas(h_lat, y_lat, m_lat[2], ln_mix_g[layer], ln_mix_b[layer],
                                          m_lat[3], m_lat[4], True)
        h_lat, in_lat = ln_mod_pallas(h_lat, peer(xin, xin_t), m_lat[5], ln_ffn_g[layer], ln_ffn_b[layer],
                                      nxt_lat[0], nxt_lat[1], False)
        if need_ctx:
            h_ctx, xin, xin_t = ln_mod_pallas(h_ctx, y_ctx.reshape(h_ctx.shape), m_ctx[2],
                                              ln_mix_g[layer], ln_mix_b[layer], m_ctx[3], m_ctx[4], True)
            h_ctx, in_ctx = ln_mod_pallas(h_ctx, peer(xin, xin_t), m_ctx[5], ln_ffn_g[layer], ln_ffn_b[layer],
                                          mods_ctx[layer + 1][0], mods_ctx[layer + 1][1], False)
            in_ctx = in_ctx.reshape(bn, lc, dm)
    return h_lat
```

```python
import math
import functools
import numpy as np
import jax
import jax.numpy as jnp
from jax import lax
from jax.experimental import pallas as pl
from jax.experimental.pallas import tpu as pltpu

D_MODEL = 2048
BATCH = 2
SEQ = 8192
DEPTH = 4

GRID_W = 64
CTX_LEN = 256

HY_W = 1024
HY_ORDER = 2
HY_EMB = 33
HY_BANDS = (HY_EMB - 1) // 2
HY_HID = 64
HY_FAST_DECAY = 0.3
HY_SLOW_DECAY = 1.5
HY_TARGET = 1e-2

MLA_HEADS = 8
MLA_NOPE = 128
MLA_ROPE = 64
MLA_V = 128
MLA_Q_RANK = 512
MLA_KV_RANK = 256
MLA_SCALE = (MLA_NOPE + MLA_ROPE) ** -0.5
Q_BLOCK = 128
ROPE_BASE = 10000.0

O_HY = 3 * HY_W
O_Q = O_HY + MLA_Q_RANK
O_KV = O_Q + MLA_KV_RANK
IN_W = O_KV + MLA_ROPE

S5_GC = 16
S5_NG = D_MODEL // S5_GC
S5_P = 64
S5_GB = 16
S5_NBLK = S5_NG // S5_GB

PEER_HEADS = 8
PEER_NK = 128
PEER_N = PEER_NK * PEER_NK
PEER_DK = 256
PEER_TOPK = 16
PEER_BLOCK = 128

N_EVEN = (DEPTH + 1) // 2
N_ODD = DEPTH // 2
DN_ALPHA = (2.0 * DEPTH) ** 0.25
DN_BETA = (8.0 * DEPTH) ** -0.25
LN_EPS = 1e-5
RMS_EPS = 1e-6
F32 = jnp.float32


def _ln_kernel(x_ref, g_ref, b_ref, o_ref):
    xf = x_ref[...]
    xc = xf - jnp.mean(xf, axis=-1, keepdims=True)
    var = jnp.mean(xc * xc, axis=-1, keepdims=True)
    o_ref[...] = xc * lax.rsqrt(var + LN_EPS) * g_ref[...] + b_ref[...]


LN_TL = 256


def _ln_mod_kernel(a_ref, y_ref, gate_ref, g_ref, b_ref, shift_ref, scale_ref, h_ref, x_ref, *xt_ref):
    v = DN_ALPHA * a_ref[0] + gate_ref[0] * y_ref[0]
    vc = v - jnp.mean(v, axis=-1, keepdims=True)
    var = jnp.mean(vc * vc, axis=-1, keepdims=True)
    h = vc * lax.rsqrt(var + LN_EPS) * g_ref[...] + b_ref[...]
    h_ref[0] = h
    xm = h * (1.0 + scale_ref[0]) + shift_ref[0]
    x_ref[0] = xm
    if xt_ref:
        xt_ref[0][...] = xm.T.astype(BF16)


def ln_mod_pallas(a, y, gate, g, b, shift, scale, with_transpose):
    gn, t, d = a.shape
    tl = min(LN_TL, t)
    nt = t // tl
    tok = pl.BlockSpec((1, tl, d), lambda gi, i: (gi, i, 0))
    per_group = pl.BlockSpec((1, 1, d), lambda gi, i: (gi, 0, 0))
    vec = pl.BlockSpec((1, d), lambda gi, i: (0, 0))
    out_specs = [tok, tok]
    out_shape = [jax.ShapeDtypeStruct((gn, t, d), F32)] * 2
    if with_transpose:
        out_specs.append(pl.BlockSpec((d, tl), lambda gi, i: (0, gi * nt + i)))
        out_shape.append(jax.ShapeDtypeStruct((d, gn * t), BF16))
    return pl.pallas_call(
        _ln_mod_kernel,
        grid=(gn, nt),
        in_specs=[tok, tok, per_group, vec, vec, per_group, per_group],
        out_specs=out_specs,
        out_shape=out_shape,
        compiler_params=pltpu.CompilerParams(vmem_limit_bytes=VMEM_LIMIT),
        name="ln_modulate",
    )(a, y, gate, g.reshape(1, d), b.reshape(1, d), shift, scale)


def layer_norm_pallas(x, g, b):
    shp = x.shape
    x2 = x.reshape(-1, shp[-1])
    n, d = x2.shape
    tb = 256
    out = pl.pallas_call(
        _ln_kernel,
        grid=(n // tb,),
        in_specs=[pl.BlockSpec((tb, d), lambda i: (i, 0)),
                  pl.BlockSpec((1, d), lambda i: (0, 0)),
                  pl.BlockSpec((1, d), lambda i: (0, 0))],
        out_specs=pl.BlockSpec((tb, d), lambda i: (i, 0)),
        out_shape=jax.ShapeDtypeStruct((n, d), F32),
        name="layer_norm",
    )(x2, g.reshape(1, d), b.reshape(1, d))
    return out.reshape(shp)


BF16 = jnp.bfloat16
HI = lax.Precision.HIGHEST
VMEM_LIMIT = 56 * 1024 * 1024

S5_T = 16
S5_CB = 8


def _s5_operators(lam_re, lam_im, log_dt, b_re, b_im, c_re, c_im):
    T = S5_T
    ng, p, gc = b_re.shape[1:]
    dt = jnp.exp(log_dt)[..., None]
    zr, zi = lam_re * dt, lam_im * dt
    k = jnp.arange(T + 1, dtype=F32)[:, None, None, None]
    mag = jnp.exp(zr[None] * k)
    pr, pi = mag * jnp.cos(zi[None] * k), mag * jnp.sin(zi[None] * k)
    ar, ai = pr[1], pi[1]
    den = lam_re * lam_re + lam_im * lam_im
    cr = ((ar - 1.0) * lam_re + ai * lam_im) / den
    ci = (ai * lam_re - (ar - 1.0) * lam_im) / den
    bbr = cr[..., None] * b_re - ci[..., None] * b_im
    bbi = cr[..., None] * b_im + ci[..., None] * b_re
    car = c_re[None] * pr[:, :, :, None, :] - c_im[None] * pi[:, :, :, None, :]
    cai = c_re[None] * pi[:, :, :, None, :] + c_im[None] * pr[:, :, :, None, :]
    kk = (jnp.einsum('tdgcp,dgpe->tdgce', car[:T], bbr, precision=HI)
          - jnp.einsum('tdgcp,dgpe->tdgce', cai[:T], bbi, precision=HI))
    kf, kb = kk[:, 0], kk[:, 1]
    kfull = jnp.concatenate([kb[1:][::-1], (kf[0] + kb[0])[None], kf[1:]], axis=0)
    jj = jnp.arange(T)
    idx = jj[None, :] - jj[:, None] + (T - 1)
    m = kfull[idx]
    m = m.transpose(2, 0, 4, 1, 3).reshape(ng, T * gc, T * gc)
    def e_w(d, pw_idx):
        er = pr[pw_idx, d][..., None] * bbr[d][None] - pi[pw_idx, d][..., None] * bbi[d][None]
        ei = pr[pw_idx, d][..., None] * bbi[d][None] + pi[pw_idx, d][..., None] * bbr[d][None]
        w = jnp.concatenate([er, ei], axis=2)
        return w.transpose(1, 0, 3, 2).reshape(ng, T * gc, 2 * p)
    w_e = jnp.concatenate([e_w(0, T - 1 - jj), e_w(1, jj)], axis=-1)
    def s_w(d, pw_idx):
        w = jnp.concatenate([car[pw_idx, d], -cai[pw_idx, d]], axis=-1)
        return w.transpose(1, 3, 0, 2).reshape(ng, 2 * p, T * gc)
    w_y = jnp.concatenate([m, s_w(0, jj + 1), s_w(1, T - jj)], axis=1)
    a1 = jnp.concatenate([pr[T], pr[T]], axis=-1)
    a2 = jnp.concatenate([-pi[T], pi[T]], axis=-1)
    return w_e, w_y, a1, a2


def _s5_lane_group():
    return lax.broadcasted_iota(jnp.int32, (1, LANES), 1) // S5_GC


def _s5_row_block(n):
    return next(c for c in (32, 24, 16, 8) if n % c == 0)


def _s5_group_rows(x_ref, xg_ref, row0):
    nrow, T, _ = x_ref.shape
    per = LANES // S5_GC
    grp = _s5_lane_group()
    rb = _s5_row_block(nrow)

    def body(i, carry):
        rows = pl.ds(pl.multiple_of(i * rb, 8), rb)
        dst = pl.ds(pl.multiple_of(row0 + i * rb, 8), rb)
        for hf in range(T // per):
            accs = [None] * per
            for j in range(per):
                xt = x_ref[rows, hf * per + j, :]
                for g in range(per):
                    sh = ((j - g) * S5_GC) % LANES
                    r = pltpu.roll(xt, sh, 1) if sh else xt
                    accs[g] = r if accs[g] is None else jnp.where(grp == j, r, accs[g])
            for g in range(per):
                xg_ref[g, dst, hf * LANES:(hf + 1) * LANES] = accs[g]
        return carry

    lax.fori_loop(0, nrow // rb, body, 0)


def _s5_scatter_rows(yg_scr, o_ref, row0):
    nrow, T, _ = o_ref.shape
    per = LANES // S5_GC
    grp = _s5_lane_group()
    rb = _s5_row_block(nrow)

    def body(i, carry):
        rows = pl.ds(pl.multiple_of(i * rb, 8), rb)
        src = pl.ds(pl.multiple_of(row0 + i * rb, 8), rb)
        for t in range(T):
            hf, j = divmod(t, per)
            acc = None
            for g in range(per):
                piece = yg_scr[g, src, hf * LANES:(hf + 1) * LANES]
                sh = ((g - j) * S5_GC) % LANES
                r = pltpu.roll(piece, sh, 1) if sh else piece
                acc = r if acc is None else jnp.where(grp == g, r, acc)
            o_ref[rows, t, :] = acc
        return carry

    lax.fori_loop(0, nrow // rb, body, 0)


def _s5_end_state_kernel(xc_ref, xl_ref, w_ref, ef_ref, eb_ref, xg_ref):
    _s5_group_rows(xc_ref, xg_ref, 0)
    _s5_group_rows(xl_ref, xg_ref, xc_ref.shape[0])
    half = w_ref.shape[-1] // 2

    def body(g, carry):
        e = jnp.dot(xg_ref[g], w_ref[g], precision=HI, preferred_element_type=F32)
        cols = pl.ds(pl.multiple_of(g * half, LANES), half)
        ef_ref[:, cols] = e[:, :half]
        eb_ref[:, cols] = e[:, half:]
        return carry

    lax.fori_loop(0, xg_ref.shape[0], body, 0)


def _s5_carry_kernel(ef_ref, eb_ref, a1_ref, a2_ref, s0f_ref, s0b_ref, sf_scr, sb_scr):
    @pl.when(pl.program_id(1) == 0)
    def _():
        sf_scr[...] = jnp.zeros_like(sf_scr)
        sb_scr[...] = jnp.zeros_like(sb_scr)

    half = sf_scr.shape[-1] // 2

    def step(s, e, d):
        return a1_ref[d] * s + a2_ref[d] * pltpu.roll(s, half, 1) + e

    for r in range(S5_CB):
        s = sf_scr[...]
        s0f_ref[r] = s
        sf_scr[...] = step(s, ef_ref[r], 0)
    for r in reversed(range(S5_CB)):
        s = sb_scr[...]
        s0b_ref[r] = s
        sb_scr[...] = step(s, eb_ref[r], 1)


def _s5_output_kernel(xg_ref, s0f_ref, s0b_ref, w_ref, yc_ref, yl_ref, xg_scr):
    kx = xg_scr.shape[-1]
    ks = s0f_ref.shape[-1] // xg_scr.shape[0]

    def body(g, carry):
        cols = pl.ds(pl.multiple_of(g * ks, LANES), ks)
        y = jnp.dot(xg_ref[g], w_ref[g, :kx, :], precision=HI, preferred_element_type=F32)
        y = y + jnp.dot(s0f_ref[:, cols], w_ref[g, kx:kx + ks, :], precision=HI, preferred_element_type=F32)
        y = y + jnp.dot(s0b_ref[:, cols], w_ref[g, kx + ks:, :], precision=HI, preferred_element_type=F32)
        xg_scr[g] = y
        return carry

    lax.fori_loop(0, xg_scr.shape[0], body, 0)
    _s5_scatter_rows(xg_scr, yc_ref, 0)
    _s5_scatter_rows(xg_scr, yl_ref, yc_ref.shape[0])


def s5_scan(in_ctx, in_lat, lam_re, lam_im, log_dt, b_re, b_im, c_re, c_im):
    T = S5_T
    bn, lc, d = in_ctx.shape
    ll = in_lat.shape[1]
    ng, p, gc = b_re.shape[1:]
    w_e, w_y, a1, a2 = _s5_operators(lam_re, lam_im, log_dt, b_re, b_im, c_re, c_im)
    ncc, ncl = lc // T, ll // T
    nch = ncc + ncl
    rows = bn * nch
    kx, ks = T * gc, 2 * p
    per = LANES // gc
    ncb = ng // per
    ctx_spec = pl.BlockSpec((ncc, T, LANES), lambda c, r: (r, 0, c))
    lat_spec = pl.BlockSpec((ncl, T, LANES), lambda c, r: (r, 0, c))

    grouped = pl.BlockSpec((per, nch, kx), lambda c, r: (c, r, 0))
    ef, eb, xg = pl.pallas_call(
        _s5_end_state_kernel,
        grid=(ncb, bn),
        in_specs=[ctx_spec, lat_spec,
                  pl.BlockSpec((per, kx, 2 * ks), lambda c, r: (c, 0, 0))],
        out_specs=[pl.BlockSpec((nch, per * ks), lambda c, r: (r, c)),
                   pl.BlockSpec((nch, per * ks), lambda c, r: (r, c)),
                   grouped],
        out_shape=[jax.ShapeDtypeStruct((rows, ng * ks), F32)] * 2
        + [jax.ShapeDtypeStruct((ng, rows, kx), F32)],
        compiler_params=pltpu.CompilerParams(vmem_limit_bytes=VMEM_LIMIT),
        name="s5_end_state",
    )(in_ctx.reshape(bn * ncc, T, d), in_lat.reshape(bn * ncl, T, d), w_e)

    nblk = nch // S5_CB
    cblk = (lc // T) // S5_CB
    fwd_map = lambda b, j: (b * nblk + j, 0, 0)
    bwd_map = lambda b, j: (b * nblk + jnp.where(j < cblk, cblk - 1 - j, nblk + cblk - 1 - j), 0, 0)
    s0f, s0b = pl.pallas_call(
        _s5_carry_kernel,
        grid=(bn, nblk),
        in_specs=[pl.BlockSpec((S5_CB, ng, ks), fwd_map),
                  pl.BlockSpec((S5_CB, ng, ks), bwd_map),
                  pl.BlockSpec((2, ng, ks), lambda b, j: (0, 0, 0)),
                  pl.BlockSpec((2, ng, ks), lambda b, j: (0, 0, 0))],
        out_specs=[pl.BlockSpec((S5_CB, ng, ks), fwd_map),
                   pl.BlockSpec((S5_CB, ng, ks), bwd_map)],
        out_shape=[jax.ShapeDtypeStruct((rows, ng, ks), F32)] * 2,
        scratch_shapes=[pltpu.VMEM((ng, ks), F32), pltpu.VMEM((ng, ks), F32)],
        name="s5_carry",
    )(ef.reshape(rows, ng, ks), eb.reshape(rows, ng, ks), a1, a2)

    y_ctx, y_lat = pl.pallas_call(
        _s5_output_kernel,
        grid=(ncb, bn),
        in_specs=[grouped,
                  pl.BlockSpec((nch, per * ks), lambda c, r: (r, c)),
                  pl.BlockSpec((nch, per * ks), lambda c, r: (r, c)),
                  pl.BlockSpec((per, kx + 2 * ks, kx), lambda c, r: (c, 0, 0))],
        out_specs=[ctx_spec, lat_spec],
        out_shape=[jax.ShapeDtypeStruct((bn * ncc, T, d), F32), jax.ShapeDtypeStruct((bn * ncl, T, d), F32)],
        scratch_shapes=[pltpu.VMEM((per, nch, kx), F32)],
        compiler_params=pltpu.CompilerParams(vmem_limit_bytes=VMEM_LIMIT),
        name="s5_output",
    )(xg, s0f.reshape(rows, ng * ks), s0b.reshape(rows, ng * ks), w_y)
    return y_ctx.reshape(bn, lc, d), y_lat.reshape(bn, ll, d)


PEER_TS = 256
PEER_TB = 512
PEER_EB = 1024
LANES = 128
NEG = -3.0e38
_NN = (((1,), (0,)), ((), ()))
_NT = (((1,), (1,)), ((), ()))


def _split_bf16(a):
    hi = a.astype(BF16)
    return hi, (a - hi.astype(F32)).astype(BF16)


def _dot3(ah, al, bh, bl, dims):
    f = lambda a, b: lax.dot_general(a, b, dims, preferred_element_type=F32)
    return f(ah, bh) + f(al, bh) + f(ah, bl)


def _peer_cells():
    return PEER_TOPK + 8 * (PEER_TOPK // 2 - 1) + 8


def _peer_score_kernel(x_ref, wq_ref, kh_ref, kl_ref,
                       n1_ref, w1_ref, r2_ref, e2_ref,
                       st_scr, a1_scr, a2_scr, c_scr):
    nh, _, nk, dk = kh_ref.shape
    tb = x_ref.shape[0]
    q = jnp.dot(x_ref[...].astype(BF16), wq_ref[...], preferred_element_type=F32)
    qh, ql = _split_bf16(q)
    for h in range(nh):
        for s in range(2):
            c0 = (2 * h + s) * dk
            st_scr[2 * h + s] = _dot3(kh_ref[h, s], kl_ref[h, s],
                                      qh[:, c0:c0 + dk], ql[:, c0:c0 + dk], _NT)

    def top_values(s, out_scr):
        rank = jnp.full(s.shape, float(PEER_TOPK), F32)
        for r in range(PEER_TOPK):
            m = jnp.max(s, axis=0, keepdims=True)
            out_scr[r:r + 1, :] = m
            hit = s == m
            rank = jnp.where(hit, float(r), rank)
            s = jnp.where(hit, NEG, s)
        return rank

    nsub = tb // LANES
    half = PEER_TOPK // 2

    def body(it, carry):
        h = it // nsub
        lanes = pl.ds(pl.multiple_of((it % nsub) * LANES, LANES), LANES)
        s1 = st_scr[2 * h, :, lanes]
        s2 = st_scr[2 * h + 1, :, lanes]
        top_values(s1, a1_scr)
        r2 = top_values(s2, a2_scr)
        a1 = a1_scr[...]
        a2 = a2_scr[...]
        c_scr[0:PEER_TOPK, :] = a1[0:1] + a2
        for k in range(1, half):
            c_scr[PEER_TOPK + 8 * (k - 1):PEER_TOPK + 8 * k, :] = a1[k:k + 1] + a2[0:8]
        c_scr[PEER_TOPK + 8 * (half - 1):, :] = a1[half:] + a2[0:1]
        c = c_scr[...]
        m0 = a1[0:1] + a2[0:1]
        z = jnp.zeros_like(m0)
        tau = m0
        for r in range(PEER_TOPK):
            tau = jnp.max(c, axis=0, keepdims=True)
            z = z + jnp.exp(tau - m0)
            c = jnp.where(c == tau, NEG, c)
        n1 = jnp.zeros_like(s1)
        for l in range(PEER_TOPK):
            n1 = n1 + jnp.where(s1 + a2[l:l + 1] >= tau, 1.0, 0.0)
        n1_ref[h, :, lanes] = n1
        w1_ref[h, :, lanes] = jnp.exp(s1 - a1[0:1]) / z
        r2_ref[h, :, lanes] = pltpu.bitcast(r2.astype(BF16), jnp.uint32)
        e2_ref[h, :, lanes] = pltpu.bitcast(jnp.exp(s2 - a2[0:1]).astype(BF16), jnp.uint32)
        return carry

    lax.fori_loop(0, nh * nsub, body, 0)


def _peer_expert_kernel(xt_ref, u_ref, vt_ref, n1_ref, w1_ref, r2_ref, e2_ref,
                        o_ref, acc_scr, a_scr, p_scr):
    e = pl.program_id(1)
    last = pl.num_programs(1) - 1
    slot = e % 2
    nh, nk, tb = r2_ref.shape[0], 2 * r2_ref.shape[1], r2_ref.shape[2]
    ni = a_scr.shape[0] // nk

    @pl.when(e == 0)
    def _():
        acc_scr[...] = jnp.zeros_like(acc_scr)
        p_scr[1] = jnp.zeros(p_scr.shape[1:], BF16)

    @pl.when(e < last)
    def _():
        acc_scr[...] += jnp.dot(vt_ref[...], p_scr[1 - slot], preferred_element_type=F32)
        a_scr[...] = jnp.dot(u_ref[...], xt_ref[...], preferred_element_type=F32)
        for ts in range(tb // LANES):
            lanes = slice(ts * LANES, (ts + 1) * LANES)
            for ii in range(ni):
                rows = slice(ii * nk, (ii + 1) * nk)
                g = jnp.zeros((nk, LANES), BF16)
                for h in range(nh):
                    n1row = jnp.broadcast_to(n1_ref[h, ii:ii + 1, lanes].astype(BF16), (nk, LANES))
                    w1row = jnp.broadcast_to(w1_ref[h, ii:ii + 1, lanes].astype(BF16), (nk, LANES))
                    r2 = pltpu.bitcast(r2_ref[h, :, lanes], BF16)
                    e2 = pltpu.bitcast(e2_ref[h, :, lanes], BF16)
                    g = g + jnp.where(r2 < n1row, e2, jnp.zeros((), BF16)) * w1row
                p_scr[slot, rows, lanes] = g * jax.nn.gelu(a_scr[rows, lanes].astype(BF16))

    @pl.when(e == last)
    def _():
        acc = acc_scr[...] + jnp.dot(vt_ref[...], p_scr[1 - slot], preferred_element_type=F32)
        o_ref[...] = acc.T


def peer_ffn_pallas(xin, xin_t, w_q, keys, u_bf, vt_bf, layer):
    n, d = xin.shape
    nh, _, nk, dk = keys.shape
    ne = u_bf.shape[1]
    ts = min(PEER_TS, n)
    tb = min(PEER_TB, n)
    eb = min(PEER_EB, ne)
    wq = w_q.astype(BF16)
    kh, kl = _split_bf16(keys)
    feat32 = jax.ShapeDtypeStruct((nh, nk, n), F32)
    feat16 = jax.ShapeDtypeStruct((nh, nk // 2, n), jnp.uint32)
    pack_spec = pl.BlockSpec((nh, nk // 2, ts), lambda i: (0, 0, i))
    feat_spec = pl.BlockSpec((nh, nk, ts), lambda i: (0, 0, i))
    n1, w1, r2, e2 = pl.pallas_call(
        _peer_score_kernel,
        grid=(n // ts,),
        in_specs=[pl.BlockSpec((ts, d), lambda i: (i, 0)),
                  pl.BlockSpec(wq.shape, lambda i: (0, 0)),
                  pl.BlockSpec(kh.shape, lambda i: (0, 0, 0, 0)),
                  pl.BlockSpec(kl.shape, lambda i: (0, 0, 0, 0))],
        out_specs=[feat_spec, feat_spec, pack_spec, pack_spec],
        out_shape=[feat32, feat32, feat16, feat16],
        scratch_shapes=[pltpu.VMEM((2 * nh, nk, ts), F32),
                        pltpu.VMEM((PEER_TOPK, LANES), F32),
                        pltpu.VMEM((PEER_TOPK, LANES), F32),
                        pltpu.VMEM((_peer_cells(), LANES), F32)],
        compiler_params=pltpu.CompilerParams(vmem_limit_bytes=VMEM_LIMIT),
        name="peer_scores",
    )(xin, wq, kh, kl)

    ni = eb // nk
    nblk = ne // eb
    row_spec = pl.BlockSpec((nh, ni, tb), lambda t, e: (0, jnp.minimum(e, nblk - 1), t))
    col_spec = pl.BlockSpec((nh, nk // 2, tb), lambda t, e: (0, 0, t))
    return pl.pallas_call(
        _peer_expert_kernel,
        grid=(n // tb, nblk + 1),
        in_specs=[pl.BlockSpec((d, tb), lambda t, e: (0, t)),
                  pl.BlockSpec((None, eb, d), lambda t, e: (layer, jnp.minimum(e, nblk - 1), 0)),
                  pl.BlockSpec((None, d, eb), lambda t, e: (layer, 0, jnp.maximum(e - 1, 0))),
                  row_spec, row_spec, col_spec, col_spec],
        out_specs=pl.BlockSpec((tb, d), lambda t, e: (t, 0)),
        out_shape=jax.ShapeDtypeStruct((n, d), F32),
        scratch_shapes=[pltpu.VMEM((d, tb), F32),
                        pltpu.VMEM((eb, tb), F32),
                        pltpu.VMEM((2, eb, tb), BF16)],
        compiler_params=pltpu.CompilerParams(vmem_limit_bytes=VMEM_LIMIT,
                                             dimension_semantics=("arbitrary", "arbitrary")),
        name="peer_experts",
    )(xin_t, u_bf, vt_bf, n1, w1, r2, e2)


MM_TM = 512
MM_TN = 512


def _row_tile(m):
    return next(c for c in (MM_TM, 256, 128, 64, 32, 16, 8) if m % c == 0)


def _mm_kernel(x_ref, w_ref, o_ref):
    o_ref[...] = jnp.dot(x_ref[...].astype(BF16), w_ref[...], preferred_element_type=F32)


def _glu_kernel(x_ref, wo_ref, wg_ref, o_ref):
    xb = x_ref[...].astype(BF16)
    a = jnp.dot(xb, wo_ref[...], preferred_element_type=F32)
    g = jnp.dot(xb, wg_ref[...], preferred_element_type=F32)
    o_ref[...] = a * jax.nn.sigmoid(g)


def _mm_call(body, x, ws, name):
    m, k = x.shape
    n = ws[0].shape[1]
    tm = _row_tile(m)
    tn = min(MM_TN, n)
    return pl.pallas_call(
        body,
        grid=(m // tm, n // tn),
        in_specs=[pl.BlockSpec((tm, k), lambda i, j: (i, 0))]
        + [pl.BlockSpec((k, tn), lambda i, j: (0, j)) for _ in ws],
        out_specs=pl.BlockSpec((tm, tn), lambda i, j: (i, j)),
        out_shape=jax.ShapeDtypeStruct((m, n), F32),
        compiler_params=pltpu.CompilerParams(vmem_limit_bytes=VMEM_LIMIT),
        name=name,
    )(x, *ws)


def matmul_pallas(x, w):
    n = w.shape[1]
    npad = -n % MM_TN if n > MM_TN else -n % LANES
    wb = jnp.pad(w.astype(BF16), ((0, 0), (0, npad)))
    out = _mm_call(_mm_kernel, x, [wb], "dense_proj")
    return out[:, :n] if npad else out


def _mm_pair_kernel(x1_ref, x2_ref, w1_ref, w2_ref, o_ref):
    o_ref[...] = (jnp.dot(x1_ref[...].astype(BF16), w1_ref[...], preferred_element_type=F32)
                  + jnp.dot(x2_ref[...].astype(BF16), w2_ref[...], preferred_element_type=F32))


def matmul_pair_pallas(x1, x2, w):
    m, k1 = x1.shape
    k2 = x2.shape[1]
    n = w.shape[1]
    tm, tn = _row_tile(m), min(MM_TN, n)
    wb = w.astype(BF16)
    return pl.pallas_call(
        _mm_pair_kernel,
        grid=(m // tm, n // tn),
        in_specs=[pl.BlockSpec((tm, k1), lambda i, j: (i, 0)),
                  pl.BlockSpec((tm, k2), lambda i, j: (i, 0)),
                  pl.BlockSpec((k1, tn), lambda i, j: (0, j)),
                  pl.BlockSpec((k2, tn), lambda i, j: (0, j))],
        out_specs=pl.BlockSpec((tm, tn), lambda i, j: (i, j)),
        out_shape=jax.ShapeDtypeStruct((m, n), F32),
        compiler_params=pltpu.CompilerParams(vmem_limit_bytes=VMEM_LIMIT),
        name="dense_proj_pair",
    )(x1, x2, wb[:k1], wb[k1:])


def glu_pallas(x, w_o, w_g):
    return _mm_call(_glu_kernel, x, [w_o.astype(BF16), w_g.astype(BF16)], "glu_proj")


ATT_TQ = 512
ATT_TK = 2816


def _attn_kernel(qn_ref, qr_ref, cos_ref, sin_ref, knt_ref, krt_ref, v_ref, o_ref, *, tk):
    x = qr_ref[0]
    lane = lax.broadcasted_iota(jnp.int32, (1, LANES), 1)
    quarter = MLA_ROPE // 4
    rot = jnp.where(lane % (2 * quarter) < quarter,
                    -pltpu.roll(x, LANES - quarter, 1), pltpu.roll(x, quarter, 1))
    qr = x * cos_ref[...] + rot * sin_ref[...]
    q = jnp.concatenate([qn_ref[0].astype(BF16), qr.astype(BF16)], axis=-1)
    tq = q.shape[0]
    nk = knt_ref.shape[-1]
    m = jnp.full((tq, 1), NEG, F32)
    l = jnp.zeros((tq, 1), F32)
    acc = jnp.zeros((tq, v_ref.shape[-1]), F32)
    for c in range(nk // tk):
        kt = jnp.concatenate([knt_ref[0, 0, :, c * tk:(c + 1) * tk],
                              krt_ref[0, :, c * tk:(c + 1) * tk]], axis=0)
        s = jnp.dot(q, kt, preferred_element_type=F32)
        m_new = jnp.maximum(m, jnp.max(s, axis=-1, keepdims=True))
        alpha = jnp.exp(m - m_new)
        p = jnp.exp(s - m_new)
        l = alpha * l + jnp.sum(p, axis=-1, keepdims=True)
        acc = alpha * acc + jnp.dot(p.astype(BF16), v_ref[0, 0, c * tk:(c + 1) * tk, :],
                                    preferred_element_type=F32)
        m = m_new
    o_ref[0] = acc / l


def _mla_query_weights(w_uq):
    r = w_uq.shape[0]
    w = w_uq.reshape(r, MLA_HEADS, MLA_NOPE + MLA_ROPE) * MLA_SCALE
    rope = jnp.pad(w[..., MLA_NOPE:], ((0, 0), (0, 0), (0, LANES - MLA_ROPE)))
    return jnp.concatenate([w[..., :MLA_NOPE].reshape(r, -1), rope.reshape(r, -1)], axis=1)


def mla_attention_pallas(q2, cos, sin, kn, kr, v):
    bn, lq, _ = q2.shape
    sk, nh = kn.shape[1], kn.shape[2]
    dv = v.shape[-1]
    pad = ((0, 0), (0, LANES - cos.shape[-1]))
    cosp, sinp = jnp.pad(cos, pad), jnp.pad(sin, pad)
    knt = kn.astype(BF16).transpose(0, 2, 3, 1)
    krt = jnp.pad(kr.astype(BF16).transpose(0, 2, 1), ((0, 0), (0, LANES - kr.shape[-1]), (0, 0)))
    vv = v.astype(BF16).transpose(0, 2, 1, 3)
    tq = min(ATT_TQ, lq)
    tk = ATT_TK if sk % ATT_TK == 0 else sk
    return pl.pallas_call(
        functools.partial(_attn_kernel, tk=tk),
        grid=(bn, nh, lq // tq),
        in_specs=[pl.BlockSpec((1, tq, LANES), lambda b, h, i: (b, i, h)),
                  pl.BlockSpec((1, tq, LANES), lambda b, h, i: (b, i, nh + h)),
                  pl.BlockSpec((tq, LANES), lambda b, h, i: (i, 0)),
                  pl.BlockSpec((tq, LANES), lambda b, h, i: (i, 0)),
                  pl.BlockSpec((1, 1, kn.shape[-1], sk), lambda b, h, i: (b, h, 0, 0)),
                  pl.BlockSpec((1, LANES, sk), lambda b, h, i: (b, 0, 0)),
                  pl.BlockSpec((1, 1, sk, dv), lambda b, h, i: (b, h, 0, 0))],
        out_specs=pl.BlockSpec((1, tq, dv), lambda b, h, i: (b, i, h)),
        out_shape=jax.ShapeDtypeStruct((bn, lq, nh * dv), F32),
        compiler_params=pltpu.CompilerParams(vmem_limit_bytes=VMEM_LIMIT),
        name="mla_attention",
    )(q2, q2, cosp, sinp, knt, krt, vv)


def layer_norm(x, g, b):
    xf = x.astype(F32)
    xc = xf - jnp.mean(xf, axis=-1, keepdims=True)
    var = jnp.mean(xc * xc, axis=-1, keepdims=True)
    return (xc * lax.rsqrt(var + LN_EPS) * g + b).astype(x.dtype)


def rms_norm(x, g):
    xf = x.astype(F32)
    return (xf * lax.rsqrt(jnp.mean(xf * xf, axis=-1, keepdims=True) + RMS_EPS) * g).astype(x.dtype)


def axial_rope(L):
    rows = L // GRID_W
    row = jnp.repeat(jnp.arange(rows), GRID_W).astype(F32)
    col = jnp.tile(jnp.arange(GRID_W), rows).astype(F32)
    half = MLA_ROPE // 2
    inv = ROPE_BASE ** (-jnp.arange(0, half, 2, dtype=F32) / half)
    ar, ac = row[:, None] * inv, col[:, None] * inv
    ang = jnp.concatenate([ar, ar, ac, ac], axis=-1)
    return jnp.cos(ang), jnp.sin(ang)


def apply_axial_rope(x, cos, sin):
    xs = x.reshape(x.shape[:-1] + (2, 2, MLA_ROPE // 4))
    rot = jnp.concatenate([-xs[..., 1:, :], xs[..., :1, :]], axis=-2).reshape(x.shape)
    return (x * cos + rot * sin).astype(x.dtype)


def short_conv3(z, w, b):
    L = z.shape[1]
    zp = jnp.pad(z, ((0, 0), (1, 1), (0, 0)))
    return zp[:, :L] * w[0] + zp[:, 1:L + 1] * w[1] + zp[:, 2:] * w[2] + b


def hyena_pos_features(L):
    t = jnp.linspace(0.0, 1.0, L, dtype=F32)[:, None]
    w = (2.0 * math.pi / L) * jnp.arange(L, dtype=F32)[:, None]
    f = jnp.linspace(1e-4, HY_BANDS - 1, HY_BANDS, dtype=F32)[None, :]
    return jnp.concatenate([t, jnp.cos(f * w), -jnp.sin(f * w)], axis=-1)


def hyena_filter_spectrum(L, w1, b1, w2, b2, w3, b3, w4, freq):
    act = lambda a: jnp.sin(freq * a)
    hdn = act(hyena_pos_features(L) @ w1 + b1)
    hdn = act(hdn @ w2 + b2)
    hdn = act(hdn @ w3 + b3)
    deltas = jnp.abs(jnp.linspace(math.log(HY_TARGET) / HY_SLOW_DECAY, math.log(HY_TARGET) / HY_FAST_DECAY,
                                  HY_ORDER * HY_W, dtype=F32)).reshape(HY_ORDER, HY_W)
    t = jnp.linspace(0.0, 1.0, L, dtype=F32)[:, None, None]
    w4d = w4.reshape(w4.shape[0], 2, HY_ORDER * HY_W)
    hf = (hdn @ w4d[:, 0]).astype(F32).reshape(L, HY_ORDER, HY_W) * jnp.exp(-t * deltas)
    hb = (hdn[::-1] @ w4d[:, 1]).astype(F32).reshape(L, HY_ORDER, HY_W) * jnp.exp(-t[::-1] * deltas)
    norm = jnp.sum(jnp.abs(hf), axis=0) + jnp.sum(jnp.abs(hb), axis=0)
    return jnp.concatenate([hf / norm, jnp.zeros((1, HY_ORDER, HY_W), F32), hb[:L - 1] / norm], axis=0)


def fft_long_conv(u, k_f):
    L = u.shape[1]
    u_f = jnp.fft.rfft(u.astype(F32), n=2 * L, axis=1)
    return jnp.fft.irfft(u_f * k_f[None], n=2 * L, axis=1)[:, :L]


FFT_R = 128
FFT_CB = 16


def _dft_constants():
    r = FFT_R
    n = r * r
    idx = np.arange(r)
    f = np.exp(-2j * np.pi * np.outer(idx, idx) / r)
    tw = np.exp(-2j * np.pi * np.outer(idx, idx) / n)
    blk = lambda w: np.block([[w.real, w.imag], [-w.imag, w.real]])
    mats = {
        'w1': blk(f[:r // 2, :]),
        'w1f': np.concatenate([f.real, f.imag], 1),
        'w2': blk(f),
        'w2c': blk(np.conj(f)),
        'w1c': blk(np.conj(f)[:, :r // 2]) / n,
    }
    out = {}
    for name, w in mats.items():
        w32 = jnp.asarray(w, F32)
        out[name] = _split_bf16(w32)
    out['twa'] = jnp.asarray(np.concatenate([tw.real, tw.real], 1), F32)
    out['twb'] = jnp.asarray(np.concatenate([-tw.imag, tw.imag], 1), F32)
    out['twbc'] = jnp.asarray(np.concatenate([tw.imag, -tw.imag], 1), F32)
    return out


def _cmm(x, wh_ref, wl_ref):
    xh, xl = _split_bf16(x)
    return _dot3(xh, xl, wh_ref[...], wl_ref[...], _NN)


def _swap_halves(x):
    h = x.shape[-1] // 2
    return jnp.concatenate([x[..., h:], x[..., :h]], axis=-1)


def _cmul(x, a, b):
    return x * a + _swap_halves(x) * b


def _tile_transpose(x):
    h = x.shape[-1] // 2
    return jnp.concatenate([jnp.swapaxes(x[..., :h], 1, 2), jnp.swapaxes(x[..., h:], 1, 2)], axis=-1)


def _fft_forward(z2d, cb, w1h, w1l, w2h, w2l, twa, twb):
    r = FFT_R
    x = _cmm(z2d, w1h, w1l).reshape(cb, r, 2 * r)
    x = _tile_transpose(_cmul(x, twa[...], twb[...]))
    return _cmm(x.reshape(cb * r, 2 * r), w2h, w2l).reshape(cb, r, 2 * r)


def _hy_filter_kernel(k_ref, w1h, w1l, w2h, w2l, twa, twb, o_ref):
    cb = k_ref.shape[0]
    o_ref[...] = _fft_forward(k_ref[...].reshape(cb * FFT_R, FFT_R), cb, w1h, w1l, w2h, w2l, twa, twb)


def _hy_conv_kernel(z_ref, k_ref, w1h, w1l, w2h, w2l, w2ch, w2cl, w1ch, w1cl, twa, twb, twbc, o_ref):
    cb = z_ref.shape[0]
    r = FFT_R
    x = _fft_forward(z_ref[...].reshape(cb * r, r), cb, w1h, w1l, w2h, w2l, twa, twb)
    k = k_ref[...]
    kr, ki = k[..., :r], k[..., r:]
    x = _cmul(x, jnp.concatenate([kr, kr], axis=-1), jnp.concatenate([-ki, ki], axis=-1))
    x = _cmm(x.reshape(cb * r, 2 * r), w2ch, w2cl).reshape(cb, r, 2 * r)
    x = _tile_transpose(_cmul(x, twa[...], twbc[...]))
    o_ref[...] = _cmm(x.reshape(cb * r, 2 * r), w1ch, w1cl).reshape(cb, r, r)


def _const_spec(a):
    return pl.BlockSpec(a.shape, lambda i: (0,) * a.ndim)


def hyena_filter_fft(k_time, consts):
    r = FFT_R
    c = k_time.shape[1]
    kt = k_time.reshape(r, r, c).transpose(2, 1, 0)
    ws = [*consts['w1f'], *consts['w2'], consts['twa'], consts['twb']]
    return pl.pallas_call(
        _hy_filter_kernel,
        grid=(c // FFT_CB,),
        in_specs=[pl.BlockSpec((FFT_CB, r, r), lambda i: (i, 0, 0))] + [_const_spec(w) for w in ws],
        out_specs=pl.BlockSpec((FFT_CB, r, 2 * r), lambda i: (i, 0, 0)),
        out_shape=jax.ShapeDtypeStruct((c, r, 2 * r), F32),
        compiler_params=pltpu.CompilerParams(vmem_limit_bytes=VMEM_LIMIT),
        name="hyena_filter_fft",
    )(kt, *ws)


def hyena_long_conv(u, k_spec, consts):
    r = FFT_R
    bn, ll, c = u.shape
    assert bn == 2 and 2 * ll == r * r
    z = u.reshape(2, r // 2, r, c).transpose(3, 2, 0, 1).reshape(c, r, r)
    ws = [*consts['w1'], *consts['w2'], *consts['w2c'], *consts['w1c'],
          consts['twa'], consts['twb'], consts['twbc']]
    y = pl.pallas_call(
        _hy_conv_kernel,
        grid=(c // FFT_CB,),
        in_specs=[pl.BlockSpec((FFT_CB, r, r), lambda i: (i, 0, 0)),
                  pl.BlockSpec((FFT_CB, r, 2 * r), lambda i: (i, 0, 0))] + [_const_spec(w) for w in ws],
        out_specs=pl.BlockSpec((FFT_CB, r, r), lambda i: (i, 0, 0)),
        out_shape=jax.ShapeDtypeStruct((c, r, r), F32),
        compiler_params=pltpu.CompilerParams(vmem_limit_bytes=VMEM_LIMIT),
        name="hyena_long_conv",
    )(z, k_spec, *ws)
    return y.reshape(c, r, 2, r // 2).transpose(2, 3, 1, 0).reshape(2, ll, c)


def hyena_mixer(z, conv_w, conv_b, w1, b1, w2, b2, w3, b3, w4, freq, bias):
    L = z.shape[1]
    zc = short_conv3(z, conv_w, conv_b)
    x1, x2, v = jnp.split(zc, 3, axis=-1)
    k_t = hyena_filter_spectrum(L, w1, b1, w2, b2, w3, b3, w4, freq)
    y = v.astype(F32)
    use_mxu_fft = z.shape[0] == 2 and 2 * L == FFT_R * FFT_R
    if use_mxu_fft:
        consts = _dft_constants()
    else:
        k_f = jnp.fft.rfft(k_t, axis=0)
    for o, gate in enumerate((x1, x2)):
        if use_mxu_fft:
            conv = hyena_long_conv(y, hyena_filter_fft(k_t[:, o], consts), consts)
        else:
            conv = fft_long_conv(y, k_f[:, o])
        y = gate * (conv + y * bias[o])
    return y.astype(z.dtype)


def _mm3(x, w):
    b, l, k = x.shape
    return matmul_pallas(x.reshape(b * l, k), w).reshape(b, l, w.shape[1])


def even_mixer(in_ctx, in_lat, cos, sin, need_ctx, w_in, hy, mla, w_o):
    q_norm, w_uq, kv_norm, w_ukv = mla
    bn, lc, _ = in_ctx.shape
    w_hy, w_att = w_in[:, :O_HY], w_in[:, O_HY:]
    nq, nkv = O_Q - O_HY, O_KV - O_HY
    zc, zl = _mm3(in_ctx, w_att), _mm3(in_lat, w_att)
    kv_lat = jnp.concatenate([zc[..., nq:nkv], zl[..., nq:nkv]], axis=1)
    kv = _mm3(rms_norm(kv_lat, kv_norm), w_ukv).reshape(bn, kv_lat.shape[1], MLA_HEADS, MLA_NOPE + MLA_V)
    kn, vv = kv[..., :MLA_NOPE], kv[..., MLA_NOPE:]
    kr = jnp.concatenate([zc[..., nkv:], apply_axial_rope(zl[..., nkv:], cos, sin)], axis=1)
    w_q2 = _mla_query_weights(w_uq)

    def out_proj(hyena, att):
        b, l, k = hyena.shape
        return matmul_pair_pallas(hyena.reshape(b * l, k), att.reshape(b * l, -1), w_o).reshape(b, l, -1)

    att = mla_attention_pallas(_mm3(rms_norm(zl[..., :nq], q_norm), w_q2), cos, sin, kn, kr, vv)
    y_lat = out_proj(hyena_mixer(_mm3(in_lat, w_hy), *hy), att)
    y_ctx = None
    if need_ctx:
        att_c = mla_attention_pallas(_mm3(rms_norm(zc[..., :nq], q_norm), w_q2),
                                     jnp.ones((lc, MLA_ROPE), F32), jnp.zeros((lc, MLA_ROPE), F32),
                                     kn[:, :lc], kr[:, :lc], vv[:, :lc])
        y_ctx = out_proj(hyena_mixer(_mm3(in_ctx, w_hy), *hy), att_c)
    return y_ctx, y_lat


def _scan_op(e1, e2):
    a1, b1 = e1
    a2, b2 = e2
    return a1 * a2, a2 * b1 + b2


def linear_scan(a_bar, bu, s0):
    bu = bu.at[:, 0].add(a_bar * s0)
    a = jnp.broadcast_to(a_bar, bu.shape)
    _, s = lax.associative_scan(_scan_op, (a, bu), axis=1)
    return s


def _rev(s, d):
    return s[:, ::-1] if d == 1 else s


def s5_mixer(in_ctx, in_lat, need_ctx, lam_re, lam_im, log_dt, b_re, b_im, c_re, c_im, d_skip, w_o, w_g):
    s_ctx, s_lat = s5_scan(in_ctx, in_lat, lam_re, lam_im, log_dt, b_re, b_im, c_re, c_im)

    def finish(ys, h):
        y = ys + d_skip * h.astype(F32)
        y = jax.nn.gelu(y).astype(h.dtype)
        b, l, dm = y.shape
        return glu_pallas(y.reshape(b * l, dm), w_o, w_g).reshape(b, l, dm)

    y_lat = finish(s_lat, in_lat)
    y_ctx = finish(s_ctx, in_ctx) if need_ctx else None
    return y_ctx, y_lat


def peer_ffn(h, w_q, keys, u_tab, v_tab):
    Bn, L, D = h.shape
    hb = h.reshape(Bn * L // PEER_BLOCK, PEER_BLOCK, D)

    def block_fn(xb):
        q = (xb @ w_q).reshape(PEER_BLOCK, PEER_HEADS, 2, PEER_DK // 2)
        s = jnp.einsum('thsk,hsnk->thsn', q, keys).astype(F32)
        top_s, top_i = lax.top_k(s, PEER_TOPK)
        cand = top_s[:, :, 0, :, None] + top_s[:, :, 1, None, :]
        cand_s, cand_j = lax.top_k(cand.reshape(PEER_BLOCK, PEER_HEADS, PEER_TOPK * PEER_TOPK), PEER_TOPK)
        idx = (jnp.take_along_axis(top_i[:, :, 0], cand_j // PEER_TOPK, axis=-1) * PEER_NK
               + jnp.take_along_axis(top_i[:, :, 1], cand_j % PEER_TOPK, axis=-1))
        g = jax.nn.softmax(cand_s, axis=-1)
        act = jax.nn.gelu(jnp.einsum('thkd,td->thk', u_tab[idx], xb).astype(F32))
        return jnp.einsum('thk,thkd->td', (g * act).astype(xb.dtype), v_tab[idx])

    return lax.map(block_fn, hb).reshape(Bn, L, D)


def kernel(x, c, ctx, c_ctx, mod_w, mod_b, ln_mix_g, ln_mix_b, ln_ffn_g, ln_ffn_b,
           ev_w_in, ev_conv_w, ev_conv_b, hy_w1, hy_b1, hy_w2, hy_b2, hy_w3, hy_b3, hy_w4,
           hy_freq, hy_bias, mla_q_norm, mla_w_uq, mla_kv_norm, mla_w_ukv, ev_w_o,
           s5_lam_re, s5_lam_im, s5_log_dt, s5_b_re, s5_b_im, s5_c_re, s5_c_im, s5_d,
           od_w_o, od_w_g, peer_w_q, peer_keys, peer_u, peer_v):
    cos, sin = axial_rope(x.shape[1])
    act_lat = jax.nn.silu(c)
    act_ctx = jax.nn.silu(c_ctx)
    bn, ll, dm = x.shape
    lc = ctx.shape[1]
    mods_lat = [jnp.split((act_lat @ mod_w[l] + mod_b[l])[:, None, :], 6, axis=-1) for l in range(DEPTH)]
    mods_ctx = [jnp.split((act_ctx @ mod_w[l] + mod_b[l])[None, None, :], 6, axis=-1) for l in range(DEPTH)]
    h_lat = x
    h_ctx = ctx.reshape(1, bn * lc, dm)
    in_lat = h_lat * (1.0 + mods_lat[0][1]) + mods_lat[0][0]
    in_ctx = (h_ctx * (1.0 + mods_ctx[0][1]) + mods_ctx[0][0]).reshape(bn, lc, dm)
    zero_lat = jnp.zeros_like(mods_lat[0][0])
    u_bf = peer_u.astype(BF16)
    vt_bf = peer_v.transpose(0, 2, 1).astype(BF16)
    for layer in range(DEPTH):
        need_ctx = layer < DEPTH - 1
        m_lat, m_ctx = mods_lat[layer], mods_ctx[layer]
        i = layer // 2
        if layer % 2 == 0:
            hy = (ev_conv_w[i], ev_conv_b[i], hy_w1[i], hy_b1[i], hy_w2[i], hy_b2[i], hy_w3[i], hy_b3[i],
                  hy_w4[i], hy_freq[i], hy_bias[i])
            mla = (mla_q_norm[i], mla_w_uq[i], mla_kv_norm[i], mla_w_ukv[i])
            y_ctx, y_lat = even_mixer(in_ctx, in_lat, cos, sin, need_ctx, ev_w_in[i], hy, mla, ev_w_o[i])
        else:
            y_ctx, y_lat = s5_mixer(in_ctx, in_lat, need_ctx, s5_lam_re[i], s5_lam_im[i], s5_log_dt[i],
                                    s5_b_re[i], s5_b_im[i], s5_c_re[i], s5_c_im[i], s5_d[i],
                                    od_w_o[i], od_w_g[i])
        peer = lambda xm, xt: peer_ffn_pallas(xm.reshape(-1, dm), xt, peer_w_q[layer], peer_keys[layer],
                                              u_bf, vt_bf, layer).reshape(xm.shape)
        nxt_lat = (mods_lat[layer + 1][0], mods_lat[layer + 1][1]) if need_ctx else (zero_lat, zero_lat)
        h_lat, xin, xin_t = ln_mod_pallas(h_lat, y_lat, m_lat[2], ln_mix_g[layer], ln_mix_b[layer],
                                          m_lat[3], m_lat[4], True)
        h_lat, in_lat = ln_mod_pallas(h_lat, peer(xin, xin_t), m_lat[5], ln_ffn_g[layer], ln_ffn_b[layer],
                                      nxt_lat[0], nxt_lat[1], False)
        if need_ctx:
            h_ctx, xin, xin_t = ln_mod_pallas(h_ctx, y_ctx.reshape(h_ctx.shape), m_ctx[2],
                                              ln_mix_g[layer], ln_mix_b[layer], m_ctx[3], m_ctx[4], True)
            h_ctx, in_ctx = ln_mod_pallas(h_ctx, peer(xin, xin_t), m_ctx[5], ln_ffn_g[layer], ln_ffn_b[layer],
                                          mods_ctx[layer + 1][0], mods_ctx[layer + 1][1], False)
            in_ctx = in_ctx.reshape(bn, lc, dm)
    return h_lat
```

```python
import math
import functools
import numpy as np
import jax
import jax.numpy as jnp
from jax import lax
from jax.experimental import pallas as pl
from jax.experimental.pallas import tpu as pltpu

D_MODEL = 2048
BATCH = 2
SEQ = 8192
DEPTH = 4

GRID_W = 64
CTX_LEN = 256

HY_W = 1024
HY_ORDER = 2
HY_EMB = 33
HY_BANDS = (HY_EMB - 1) // 2
HY_HID = 64
HY_FAST_DECAY = 0.3
HY_SLOW_DECAY = 1.5
HY_TARGET = 1e-2

MLA_HEADS = 8
MLA_NOPE = 128
MLA_ROPE = 64
MLA_V = 128
MLA_Q_RANK = 512
MLA_KV_RANK = 256
MLA_SCALE = (MLA_NOPE + MLA_ROPE) ** -0.5
Q_BLOCK = 128
ROPE_BASE = 10000.0

O_HY = 3 * HY_W
O_Q = O_HY + MLA_Q_RANK
O_KV = O_Q + MLA_KV_RANK
IN_W = O_KV + MLA_ROPE

S5_GC = 16
S5_NG = D_MODEL // S5_GC
S5_P = 64
S5_GB = 16
S5_NBLK = S5_NG // S5_GB

PEER_HEADS = 8
PEER_NK = 128
PEER_N = PEER_NK * PEER_NK
PEER_DK = 256
PEER_TOPK = 16
PEER_BLOCK = 128

N_EVEN = (DEPTH + 1) // 2
N_ODD = DEPTH // 2
DN_ALPHA = (2.0 * DEPTH) ** 0.25
DN_BETA = (8.0 * DEPTH) ** -0.25
LN_EPS = 1e-5
RMS_EPS = 1e-6
F32 = jnp.float32


LN_TL = 256


def _ln_mod_kernel(a_ref, y_ref, gate_ref, g_ref, b_ref, shift_ref, scale_ref, h_ref, x_ref, *xt_ref):
    v = DN_ALPHA * a_ref[0] + gate_ref[0] * y_ref[0]
    vc = v - jnp.mean(v, axis=-1, keepdims=True)
    var = jnp.mean(vc * vc, axis=-1, keepdims=True)
    h = vc * lax.rsqrt(var + LN_EPS) * g_ref[...] + b_ref[...]
    h_ref[0] = h
    xm = h * (1.0 + scale_ref[0]) + shift_ref[0]
    x_ref[0] = xm
    if xt_ref:
        xt_ref[0][...] = xm.T.astype(BF16)


def ln_mod_pallas(a, y, gate, g, b, shift, scale, with_transpose):
    gn, t, d = a.shape
    tl = min(LN_TL, t)
    nt = t // tl
    tok = pl.BlockSpec((1, tl, d), lambda gi, i: (gi, i, 0))
    per_group = pl.BlockSpec((1, 1, d), lambda gi, i: (gi, 0, 0))
    vec = pl.BlockSpec((1, d), lambda gi, i: (0, 0))
    out_specs = [tok, tok]
    out_shape = [jax.ShapeDtypeStruct((gn, t, d), F32)] * 2
    if with_transpose:
        out_specs.append(pl.BlockSpec((d, tl), lambda gi, i: (0, gi * nt + i)))
        out_shape.append(jax.ShapeDtypeStruct((d, gn * t), BF16))
    return pl.pallas_call(
        _ln_mod_kernel,
        grid=(gn, nt),
        in_specs=[tok, tok, per_group, vec, vec, per_group, per_group],
        out_specs=out_specs,
        out_shape=out_shape,
        compiler_params=pltpu.CompilerParams(vmem_limit_bytes=VMEM_LIMIT),
        name="ln_modulate",
    )(a, y, gate, g.reshape(1, d), b.reshape(1, d), shift, scale)


BF16 = jnp.bfloat16
HI = lax.Precision.HIGHEST
VMEM_LIMIT = 56 * 1024 * 1024

S5_T = 16
S5_CB = 8


def _s5_operators(lam_re, lam_im, log_dt, b_re, b_im, c_re, c_im):
    T = S5_T
    ng, p, gc = b_re.shape[1:]
    dt = jnp.exp(log_dt)[..., None]
    zr, zi = lam_re * dt, lam_im * dt
    k = jnp.arange(T + 1, dtype=F32)[:, None, None, None]
    mag = jnp.exp(zr[None] * k)
    pr, pi = mag * jnp.cos(zi[None] * k), mag * jnp.sin(zi[None] * k)
    ar, ai = pr[1], pi[1]
    den = lam_re * lam_re + lam_im * lam_im
    cr = ((ar - 1.0) * lam_re + ai * lam_im) / den
    ci = (ai * lam_re - (ar - 1.0) * lam_im) / den
    bbr = cr[..., None] * b_re - ci[..., None] * b_im
    bbi = cr[..., None] * b_im + ci[..., None] * b_re
    car = c_re[None] * pr[:, :, :, None, :] - c_im[None] * pi[:, :, :, None, :]
    cai = c_re[None] * pi[:, :, :, None, :] + c_im[None] * pr[:, :, :, None, :]
    kk = (jnp.einsum('tdgcp,dgpe->tdgce', car[:T], bbr, precision=HI)
          - jnp.einsum('tdgcp,dgpe->tdgce', cai[:T], bbi, precision=HI))
    kf, kb = kk[:, 0], kk[:, 1]
    kfull = jnp.concatenate([kb[1:][::-1], (kf[0] + kb[0])[None], kf[1:]], axis=0)
    jj = jnp.arange(T)
    idx = jj[None, :] - jj[:, None] + (T - 1)
    m = kfull[idx]
    m = m.transpose(2, 0, 4, 1, 3).reshape(ng, T * gc, T * gc)
    def e_w(d, pw_idx):
        er = pr[pw_idx, d][..., None] * bbr[d][None] - pi[pw_idx, d][..., None] * bbi[d][None]
        ei = pr[pw_idx, d][..., None] * bbi[d][None] + pi[pw_idx, d][..., None] * bbr[d][None]
        w = jnp.concatenate([er, ei], axis=2)
        return w.transpose(1, 0, 3, 2).reshape(ng, T * gc, 2 * p)
    w_e = jnp.concatenate([e_w(0, T - 1 - jj), e_w(1, jj)], axis=-1)
    def s_w(d, pw_idx):
        w = jnp.concatenate([car[pw_idx, d], -cai[pw_idx, d]], axis=-1)
        return w.transpose(1, 3, 0, 2).reshape(ng, 2 * p, T * gc)
    w_y = jnp.concatenate([m, s_w(0, jj + 1), s_w(1, T - jj)], axis=1)
    a1 = jnp.concatenate([pr[T], pr[T]], axis=-1)
    a2 = jnp.concatenate([-pi[T], pi[T]], axis=-1)
    return w_e, w_y, a1, a2


def _s5_lane_group():
    return lax.broadcasted_iota(jnp.int32, (1, LANES), 1) // S5_GC


def _s5_row_block(n):
    return next(c for c in (32, 24, 16, 8) if n % c == 0)


def _s5_group_rows(x_ref, xg_ref, row0):
    nrow, T, _ = x_ref.shape
    per = LANES // S5_GC
    grp = _s5_lane_group()
    rb = _s5_row_block(nrow)

    def body(i, carry):
        rows = pl.ds(pl.multiple_of(i * rb, 8), rb)
        dst = pl.ds(pl.multiple_of(row0 + i * rb, 8), rb)
        for hf in range(T // per):
            accs = [None] * per
            for j in range(per):
                xt = x_ref[rows, hf * per + j, :]
                for g in range(per):
                    sh = ((j - g) * S5_GC) % LANES
                    r = pltpu.roll(xt, sh, 1) if sh else xt
                    accs[g] = r if accs[g] is None else jnp.where(grp == j, r, accs[g])
            for g in range(per):
                xg_ref[g, dst, hf * LANES:(hf + 1) * LANES] = accs[g]
        return carry

    lax.fori_loop(0, nrow // rb, body, 0)


def _s5_scatter_rows(yg_scr, o_ref, row0):
    nrow, T, _ = o_ref.shape
    per = LANES // S5_GC
    grp = _s5_lane_group()
    rb = _s5_row_block(nrow)

    def body(i, carry):
        rows = pl.ds(pl.multiple_of(i * rb, 8), rb)
        src = pl.ds(pl.multiple_of(row0 + i * rb, 8), rb)
        for t in range(T):
            hf, j = divmod(t, per)
            acc = None
            for g in range(per):
                piece = yg_scr[g, src, hf * LANES:(hf + 1) * LANES]
                sh = ((g - j) * S5_GC) % LANES
                r = pltpu.roll(piece, sh, 1) if sh else piece
                acc = r if acc is None else jnp.where(grp == g, r, acc)
            o_ref[rows, t, :] = acc
        return carry

    lax.fori_loop(0, nrow // rb, body, 0)


def _s5_end_state_kernel(xc_ref, xl_ref, w_ref, ef_ref, eb_ref, xg_ref):
    _s5_group_rows(xc_ref, xg_ref, 0)
    _s5_group_rows(xl_ref, xg_ref, xc_ref.shape[0])
    half = w_ref.shape[-1] // 2

    def body(g, carry):
        e = jnp.dot(xg_ref[g], w_ref[g], precision=HI, preferred_element_type=F32)
        cols = pl.ds(pl.multiple_of(g * half, LANES), half)
        ef_ref[:, cols] = e[:, :half]
        eb_ref[:, cols] = e[:, half:]
        return carry

    lax.fori_loop(0, xg_ref.shape[0], body, 0)


def _s5_carry_kernel(ef_ref, eb_ref, a1_ref, a2_ref, s0f_ref, s0b_ref, sf_scr, sb_scr):
    @pl.when(pl.program_id(1) == 0)
    def _():
        sf_scr[...] = jnp.zeros_like(sf_scr)
        sb_scr[...] = jnp.zeros_like(sb_scr)

    half = sf_scr.shape[-1] // 2

    def step(s, e, d):
        return a1_ref[d] * s + a2_ref[d] * pltpu.roll(s, half, 1) + e

    for r in range(S5_CB):
        s = sf_scr[...]
        s0f_ref[r] = s
        sf_scr[...] = step(s, ef_ref[r], 0)
    for r in reversed(range(S5_CB)):
        s = sb_scr[...]
        s0b_ref[r] = s
        sb_scr[...] = step(s, eb_ref[r], 1)


def _s5_output_kernel(xg_ref, s0f_ref, s0b_ref, w_ref, yc_ref, yl_ref, xg_scr):
    kx = xg_scr.shape[-1]
    ks = s0f_ref.shape[-1] // xg_scr.shape[0]

    def body(g, carry):
        cols = pl.ds(pl.multiple_of(g * ks, LANES), ks)
        y = jnp.dot(xg_ref[g], w_ref[g, :kx, :], precision=HI, preferred_element_type=F32)
        y = y + jnp.dot(s0f_ref[:, cols], w_ref[g, kx:kx + ks, :], precision=HI, preferred_element_type=F32)
        y = y + jnp.dot(s0b_ref[:, cols], w_ref[g, kx + ks:, :], precision=HI, preferred_element_type=F32)
        xg_scr[g] = y
        return carry

    lax.fori_loop(0, xg_scr.shape[0], body, 0)
    _s5_scatter_rows(xg_scr, yc_ref, 0)
    _s5_scatter_rows(xg_scr, yl_ref, yc_ref.shape[0])


def s5_scan(in_ctx, in_lat, lam_re, lam_im, log_dt, b_re, b_im, c_re, c_im):
    T = S5_T
    bn, lc, d = in_ctx.shape
    ll = in_lat.shape[1]
    ng, p, gc = b_re.shape[1:]
    w_e, w_y, a1, a2 = _s5_operators(lam_re, lam_im, log_dt, b_re, b_im, c_re, c_im)
    ncc, ncl = lc // T, ll // T
    nch = ncc + ncl
    rows = bn * nch
    kx, ks = T * gc, 2 * p
    per = LANES // gc
    ncb = ng // per
    ctx_spec = pl.BlockSpec((ncc, T, LANES), lambda c, r: (r, 0, c))
    lat_spec = pl.BlockSpec((ncl, T, LANES), lambda c, r: (r, 0, c))

    grouped = pl.BlockSpec((per, nch, kx), lambda c, r: (c, r, 0))
    ef, eb, xg = pl.pallas_call(
        _s5_end_state_kernel,
        grid=(ncb, bn),
        in_specs=[ctx_spec, lat_spec,
                  pl.BlockSpec((per, kx, 2 * ks), lambda c, r: (c, 0, 0))],
        out_specs=[pl.BlockSpec((nch, per * ks), lambda c, r: (r, c)),
                   pl.BlockSpec((nch, per * ks), lambda c, r: (r, c)),
                   grouped],
        out_shape=[jax.ShapeDtypeStruct((rows, ng * ks), F32)] * 2
        + [jax.ShapeDtypeStruct((ng, rows, kx), F32)],
        compiler_params=pltpu.CompilerParams(vmem_limit_bytes=VMEM_LIMIT),
        name="s5_end_state",
    )(in_ctx.reshape(bn * ncc, T, d), in_lat.reshape(bn * ncl, T, d), w_e)

    nblk = nch // S5_CB
    cblk = (lc // T) // S5_CB
    fwd_map = lambda b, j: (b * nblk + j, 0, 0)
    bwd_map = lambda b, j: (b * nblk + jnp.where(j < cblk, cblk - 1 - j, nblk + cblk - 1 - j), 0, 0)
    s0f, s0b = pl.pallas_call(
        _s5_carry_kernel,
        grid=(bn, nblk),
        in_specs=[pl.BlockSpec((S5_CB, ng, ks), fwd_map),
                  pl.BlockSpec((S5_CB, ng, ks), bwd_map),
                  pl.BlockSpec((2, ng, ks), lambda b, j: (0, 0, 0)),
                  pl.BlockSpec((2, ng, ks), lambda b, j: (0, 0, 0))],
        out_specs=[pl.BlockSpec((S5_CB, ng, ks), fwd_map),
                   pl.BlockSpec((S5_CB, ng, ks), bwd_map)],
        out_shape=[jax.ShapeDtypeStruct((rows, ng, ks), F32)] * 2,
        scratch_shapes=[pltpu.VMEM((ng, ks), F32), pltpu.VMEM((ng, ks), F32)],
        name="s5_carry",
    )(ef.reshape(rows, ng, ks), eb.reshape(rows, ng, ks), a1, a2)

    y_ctx, y_lat = pl.pallas_call(
        _s5_output_kernel,
        grid=(ncb, bn),
        in_specs=[grouped,
                  pl.BlockSpec((nch, per * ks), lambda c, r: (r, c)),
                  pl.BlockSpec((nch, per * ks), lambda c, r: (r, c)),
                  pl.BlockSpec((per, kx + 2 * ks, kx), lambda c, r: (c, 0, 0))],
        out_specs=[ctx_spec, lat_spec],
        out_shape=[jax.ShapeDtypeStruct((bn * ncc, T, d), F32), jax.ShapeDtypeStruct((bn * ncl, T, d), F32)],
        scratch_shapes=[pltpu.VMEM((per, nch, kx), F32)],
        compiler_params=pltpu.CompilerParams(vmem_limit_bytes=VMEM_LIMIT),
        name="s5_output",
    )(xg, s0f.reshape(rows, ng * ks), s0b.reshape(rows, ng * ks), w_y)
    return y_ctx.reshape(bn, lc, d), y_lat.reshape(bn, ll, d)


PEER_TS = 256
PEER_TB = 512
PEER_EB = 1024
LANES = 128
NEG = -3.0e38
_NN = (((1,), (0,)), ((), ()))
_NT = (((1,), (1,)), ((), ()))


def _split_bf16(a):
    hi = a.astype(BF16)
    return hi, (a - hi.astype(F32)).astype(BF16)


def _dot3(ah, al, bh, bl, dims):
    f = lambda a, b: lax.dot_general(a, b, dims, preferred_element_type=F32)
    return f(ah, bh) + f(al, bh) + f(ah, bl)


def _peer_cells():
    return PEER_TOPK + 8 * (PEER_TOPK // 2 - 1) + 8


def _peer_score_kernel(x_ref, wq_ref, kh_ref, kl_ref,
                       n1_ref, w1_ref, r2_ref, e2_ref,
                       st_scr, a1_scr, a2_scr, c_scr):
    nh, _, nk, dk = kh_ref.shape
    tb = x_ref.shape[0]
    q = jnp.dot(x_ref[...].astype(BF16), wq_ref[...], preferred_element_type=F32)
    qh, ql = _split_bf16(q)
    for h in range(nh):
        for s in range(2):
            c0 = (2 * h + s) * dk
            st_scr[2 * h + s] = _dot3(kh_ref[h, s], kl_ref[h, s],
                                      qh[:, c0:c0 + dk], ql[:, c0:c0 + dk], _NT)

    def top_values(s, out_scr):
        rank = jnp.full(s.shape, float(PEER_TOPK), F32)
        for r in range(PEER_TOPK):
            m = jnp.max(s, axis=0, keepdims=True)
            out_scr[r:r + 1, :] = m
            hit = s == m
            rank = jnp.where(hit, float(r), rank)
            s = jnp.where(hit, NEG, s)
        return rank

    nsub = tb // LANES
    half = PEER_TOPK // 2

    def body(it, carry):
        h = it // nsub
        lanes = pl.ds(pl.multiple_of((it % nsub) * LANES, LANES), LANES)
        s1 = st_scr[2 * h, :, lanes]
        s2 = st_scr[2 * h + 1, :, lanes]
        top_values(s1, a1_scr)
        r2 = top_values(s2, a2_scr)
        a1 = a1_scr[...]
        a2 = a2_scr[...]
        c_scr[0:PEER_TOPK, :] = a1[0:1] + a2
        for k in range(1, half):
            c_scr[PEER_TOPK + 8 * (k - 1):PEER_TOPK + 8 * k, :] = a1[k:k + 1] + a2[0:8]
        c_scr[PEER_TOPK + 8 * (half - 1):, :] = a1[half:] + a2[0:1]
        c = c_scr[...]
        m0 = a1[0:1] + a2[0:1]
        z = jnp.zeros_like(m0)
        tau = m0
        for r in range(PEER_TOPK):
            tau = jnp.max(c, axis=0, keepdims=True)
            z = z + jnp.exp(tau - m0)
            c = jnp.where(c == tau, NEG, c)
        n1 = jnp.zeros_like(s1)
        for l in range(PEER_TOPK):
            n1 = n1 + jnp.where(s1 + a2[l:l + 1] >= tau, 1.0, 0.0)
        n1_ref[h, :, lanes] = n1
        w1_ref[h, :, lanes] = jnp.exp(s1 - a1[0:1]) / z
        r2_ref[h, :, lanes] = pltpu.bitcast(r2.astype(BF16), jnp.uint32)
        e2_ref[h, :, lanes] = pltpu.bitcast(jnp.exp(s2 - a2[0:1]).astype(BF16), jnp.uint32)
        return carry

    lax.fori_loop(0, nh * nsub, body, 0)


def _peer_expert_kernel(xt_ref, u_ref, vt_ref, n1_ref, w1_ref, r2_ref, e2_ref,
                        o_ref, acc_scr, a_scr, p_scr):
    e = pl.program_id(1)
    last = pl.num_programs(1) - 1
    slot = e % 2
    nh, nk, tb = r2_ref.shape[0], 2 * r2_ref.shape[1], r2_ref.shape[2]
    ni = a_scr.shape[0] // nk

    @pl.when(e == 0)
    def _():
        acc_scr[...] = jnp.zeros_like(acc_scr)
        p_scr[1] = jnp.zeros(p_scr.shape[1:], BF16)

    @pl.when(e < last)
    def _():
        acc_scr[...] += jnp.dot(vt_ref[...], p_scr[1 - slot], preferred_element_type=F32)
        a_scr[...] = jnp.dot(u_ref[...], xt_ref[...], preferred_element_type=F32)
        for ts in range(tb // LANES):
            lanes = slice(ts * LANES, (ts + 1) * LANES)
            for ii in range(ni):
                rows = slice(ii * nk, (ii + 1) * nk)
                g = jnp.zeros((nk, LANES), BF16)
                for h in range(nh):
                    n1row = jnp.broadcast_to(n1_ref[h, ii:ii + 1, lanes].astype(BF16), (nk, LANES))
                    w1row = jnp.broadcast_to(w1_ref[h, ii:ii + 1, lanes].astype(BF16), (nk, LANES))
                    r2 = pltpu.bitcast(r2_ref[h, :, lanes], BF16)
                    e2 = pltpu.bitcast(e2_ref[h, :, lanes], BF16)
                    g = g + jnp.where(r2 < n1row, e2, jnp.zeros((), BF16)) * w1row
                p_scr[slot, rows, lanes] = g * jax.nn.gelu(a_scr[rows, lanes].astype(BF16))

    @pl.when(e == last)
    def _():
        acc = acc_scr[...] + jnp.dot(vt_ref[...], p_scr[1 - slot], preferred_element_type=F32)
        o_ref[...] = acc.T


def peer_ffn_pallas(xin, xin_t, w_q, keys, u_bf, vt_bf, layer):
    n, d = xin.shape
    nh, _, nk, dk = keys.shape
    ne = u_bf.shape[1]
    ts = min(PEER_TS, n)
    tb = min(PEER_TB, n)
    eb = min(PEER_EB, ne)
    wq = w_q.astype(BF16)
    kh, kl = _split_bf16(keys)
    feat32 = jax.ShapeDtypeStruct((nh, nk, n), F32)
    feat16 = jax.ShapeDtypeStruct((nh, nk // 2, n), jnp.uint32)
    pack_spec = pl.BlockSpec((nh, nk // 2, ts), lambda i: (0, 0, i))
    feat_spec = pl.BlockSpec((nh, nk, ts), lambda i: (0, 0, i))
    n1, w1, r2, e2 = pl.pallas_call(
        _peer_score_kernel,
        grid=(n // ts,),
        in_specs=[pl.BlockSpec((ts, d), lambda i: (i, 0)),
                  pl.BlockSpec(wq.shape, lambda i: (0, 0)),
                  pl.BlockSpec(kh.shape, lambda i: (0, 0, 0, 0)),
                  pl.BlockSpec(kl.shape, lambda i: (0, 0, 0, 0))],
        out_specs=[feat_spec, feat_spec, pack_spec, pack_spec],
        out_shape=[feat32, feat32, feat16, feat16],
        scratch_shapes=[pltpu.VMEM((2 * nh, nk, ts), F32),
                        pltpu.VMEM((PEER_TOPK, LANES), F32),
                        pltpu.VMEM((PEER_TOPK, LANES), F32),
                        pltpu.VMEM((_peer_cells(), LANES), F32)],
        compiler_params=pltpu.CompilerParams(vmem_limit_bytes=VMEM_LIMIT),
        name="peer_scores",
    )(xin, wq, kh, kl)

    ni = eb // nk
    nblk = ne // eb
    row_spec = pl.BlockSpec((nh, ni, tb), lambda t, e: (0, jnp.minimum(e, nblk - 1), t))
    col_spec = pl.BlockSpec((nh, nk // 2, tb), lambda t, e: (0, 0, t))
    return pl.pallas_call(
        _peer_expert_kernel,
        grid=(n // tb, nblk + 1),
        in_specs=[pl.BlockSpec((d, tb), lambda t, e: (0, t)),
                  pl.BlockSpec((None, eb, d), lambda t, e: (layer, jnp.minimum(e, nblk - 1), 0)),
                  pl.BlockSpec((None, d, eb), lambda t, e: (layer, 0, jnp.maximum(e - 1, 0))),
                  row_spec, row_spec, col_spec, col_spec],
        out_specs=pl.BlockSpec((tb, d), lambda t, e: (t, 0)),
        out_shape=jax.ShapeDtypeStruct((n, d), F32),
        scratch_shapes=[pltpu.VMEM((d, tb), F32),
                        pltpu.VMEM((eb, tb), F32),
                        pltpu.VMEM((2, eb, tb), BF16)],
        compiler_params=pltpu.CompilerParams(vmem_limit_bytes=VMEM_LIMIT,
                                             dimension_semantics=("arbitrary", "arbitrary")),
        name="peer_experts",
    )(xin_t, u_bf, vt_bf, n1, w1, r2, e2)


MM_TM = 512
MM_TN = 512


def _row_tile(m):
    return next(c for c in (MM_TM, 256, 128, 64, 32, 16, 8) if m % c == 0)


def _mm_kernel(x_ref, w_ref, o_ref):
    o_ref[...] = jnp.dot(x_ref[...].astype(BF16), w_ref[...], preferred_element_type=F32)


def _glu_kernel(x_ref, wo_ref, wg_ref, o_ref):
    xb = x_ref[...].astype(BF16)
    a = jnp.dot(xb, wo_ref[...], preferred_element_type=F32)
    g = jnp.dot(xb, wg_ref[...], preferred_element_type=F32)
    o_ref[...] = a * jax.nn.sigmoid(g)


def _mm_call(body, x, ws, name):
    m, k = x.shape
    n = ws[0].shape[1]
    tm = _row_tile(m)
    tn = min(MM_TN, n)
    return pl.pallas_call(
        body,
        grid=(m // tm, n // tn),
        in_specs=[pl.BlockSpec((tm, k), lambda i, j: (i, 0))]
        + [pl.BlockSpec((k, tn), lambda i, j: (0, j)) for _ in ws],
        out_specs=pl.BlockSpec((tm, tn), lambda i, j: (i, j)),
        out_shape=jax.ShapeDtypeStruct((m, n), F32),
        compiler_params=pltpu.CompilerParams(vmem_limit_bytes=VMEM_LIMIT),
        name=name,
    )(x, *ws)


def matmul_pallas(x, w):
    n = w.shape[1]
    npad = -n % MM_TN if n > MM_TN else -n % LANES
    wb = jnp.pad(w.astype(BF16), ((0, 0), (0, npad)))
    out = _mm_call(_mm_kernel, x, [wb], "dense_proj")
    return out[:, :n] if npad else out


def _mm_pair_kernel(x1_ref, x2_ref, w1_ref, w2_ref, o_ref):
    o_ref[...] = (jnp.dot(x1_ref[...].astype(BF16), w1_ref[...], preferred_element_type=F32)
                  + jnp.dot(x2_ref[...].astype(BF16), w2_ref[...], preferred_element_type=F32))


def matmul_pair_pallas(x1, x2, w):
    m, k1 = x1.shape
    k2 = x2.shape[1]
    n = w.shape[1]
    tm, tn = _row_tile(m), min(MM_TN, n)
    wb = w.astype(BF16)
    return pl.pallas_call(
        _mm_pair_kernel,
        grid=(m // tm, n // tn),
        in_specs=[pl.BlockSpec((tm, k1), lambda i, j: (i, 0)),
                  pl.BlockSpec((tm, k2), lambda i, j: (i, 0)),
                  pl.BlockSpec((k1, tn), lambda i, j: (0, j)),
                  pl.BlockSpec((k2, tn), lambda i, j: (0, j))],
        out_specs=pl.BlockSpec((tm, tn), lambda i, j: (i, j)),
        out_shape=jax.ShapeDtypeStruct((m, n), F32),
        compiler_params=pltpu.CompilerParams(vmem_limit_bytes=VMEM_LIMIT),
        name="dense_proj_pair",
    )(x1, x2, wb[:k1], wb[k1:])


def glu_pallas(x, w_o, w_g):
    return _mm_call(_glu_kernel, x, [w_o.astype(BF16), w_g.astype(BF16)], "glu_proj")


ATT_TQ = 512
ATT_TK = 2816


def _attn_kernel(qn_ref, qr_ref, cos_ref, sin_ref, knt_ref, krt_ref, v_ref, o_ref, *, tk):
    x = qr_ref[0]
    lane = lax.broadcasted_iota(jnp.int32, (1, LANES), 1)
    quarter = MLA_ROPE // 4
    rot = jnp.where(lane % (2 * quarter) < quarter,
                    -pltpu.roll(x, LANES - quarter, 1), pltpu.roll(x, quarter, 1))
    qr = x * cos_ref[...] + rot * sin_ref[...]
    q = jnp.concatenate([qn_ref[0].astype(BF16), qr.astype(BF16)], axis=-1)
    tq = q.shape[0]
    nk = knt_ref.shape[-1]
    m = jnp.full((tq, 1), NEG, F32)
    l = jnp.zeros((tq, 1), F32)
    acc = jnp.zeros((tq, v_ref.shape[-1]), F32)
    for c in range(nk // tk):
        kt = jnp.concatenate([knt_ref[0, 0, :, c * tk:(c + 1) * tk],
                              krt_ref[0, :, c * tk:(c + 1) * tk]], axis=0)
        s = jnp.dot(q, kt, preferred_element_type=F32)
        m_new = jnp.maximum(m, jnp.max(s, axis=-1, keepdims=True))
        alpha = jnp.exp(m - m_new)
        p = jnp.exp(s - m_new)
        l = alpha * l + jnp.sum(p, axis=-1, keepdims=True)
        acc = alpha * acc + jnp.dot(p.astype(BF16), v_ref[0, 0, c * tk:(c + 1) * tk, :],
                                    preferred_element_type=F32)
        m = m_new
    o_ref[0] = acc / l


def _mla_query_weights(w_uq):
    r = w_uq.shape[0]
    w = w_uq.reshape(r, MLA_HEADS, MLA_NOPE + MLA_ROPE) * MLA_SCALE
    rope = jnp.pad(w[..., MLA_NOPE:], ((0, 0), (0, 0), (0, LANES - MLA_ROPE)))
    return jnp.concatenate([w[..., :MLA_NOPE].reshape(r, -1), rope.reshape(r, -1)], axis=1)


def mla_attention_pallas(q2, cos, sin, kn, kr, v):
    bn, lq, _ = q2.shape
    sk, nh = kn.shape[1], kn.shape[2]
    dv = v.shape[-1]
    pad = ((0, 0), (0, LANES - cos.shape[-1]))
    cosp, sinp = jnp.pad(cos, pad), jnp.pad(sin, pad)
    knt = kn.astype(BF16).transpose(0, 2, 3, 1)
    krt = jnp.pad(kr.astype(BF16).transpose(0, 2, 1), ((0, 0), (0, LANES - kr.shape[-1]), (0, 0)))
    vv = v.astype(BF16).transpose(0, 2, 1, 3)
    tq = min(ATT_TQ, lq)
    tk = ATT_TK if sk % ATT_TK == 0 else sk
    return pl.pallas_call(
        functools.partial(_attn_kernel, tk=tk),
        grid=(bn, nh, lq // tq),
        in_specs=[pl.BlockSpec((1, tq, LANES), lambda b, h, i: (b, i, h)),
                  pl.BlockSpec((1, tq, LANES), lambda b, h, i: (b, i, nh + h)),
                  pl.BlockSpec((tq, LANES), lambda b, h, i: (i, 0)),
                  pl.BlockSpec((tq, LANES), lambda b, h, i: (i, 0)),
                  pl.BlockSpec((1, 1, kn.shape[-1], sk), lambda b, h, i: (b, h, 0, 0)),
                  pl.BlockSpec((1, LANES, sk), lambda b, h, i: (b, 0, 0)),
                  pl.BlockSpec((1, 1, sk, dv), lambda b, h, i: (b, h, 0, 0))],
        out_specs=pl.BlockSpec((1, tq, dv), lambda b, h, i: (b, i, h)),
        out_shape=jax.ShapeDtypeStruct((bn, lq, nh * dv), F32),
        compiler_params=pltpu.CompilerParams(vmem_limit_bytes=VMEM_LIMIT),
        name="mla_attention",
    )(q2, q2, cosp, sinp, knt, krt, vv)


def rms_norm(x, g):
    xf = x.astype(F32)
    return (xf * lax.rsqrt(jnp.mean(xf * xf, axis=-1, keepdims=True) + RMS_EPS) * g).astype(x.dtype)


def axial_rope(L):
    rows = L // GRID_W
    row = jnp.repeat(jnp.arange(rows), GRID_W).astype(F32)
    col = jnp.tile(jnp.arange(GRID_W), rows).astype(F32)
    half = MLA_ROPE // 2
    inv = ROPE_BASE ** (-jnp.arange(0, half, 2, dtype=F32) / half)
    ar, ac = row[:, None] * inv, col[:, None] * inv
    ang = jnp.concatenate([ar, ar, ac, ac], axis=-1)
    return jnp.cos(ang), jnp.sin(ang)


def apply_axial_rope(x, cos, sin):
    xs = x.reshape(x.shape[:-1] + (2, 2, MLA_ROPE // 4))
    rot = jnp.concatenate([-xs[..., 1:, :], xs[..., :1, :]], axis=-2).reshape(x.shape)
    return (x * cos + rot * sin).astype(x.dtype)


def short_conv3(z, w, b):
    L = z.shape[1]
    zp = jnp.pad(z, ((0, 0), (1, 1), (0, 0)))
    return zp[:, :L] * w[0] + zp[:, 1:L + 1] * w[1] + zp[:, 2:] * w[2] + b


def hyena_pos_features(L):
    t = jnp.linspace(0.0, 1.0, L, dtype=F32)[:, None]
    w = (2.0 * math.pi / L) * jnp.arange(L, dtype=F32)[:, None]
    f = jnp.linspace(1e-4, HY_BANDS - 1, HY_BANDS, dtype=F32)[None, :]
    return jnp.concatenate([t, jnp.cos(f * w), -jnp.sin(f * w)], axis=-1)


def hyena_filter_spectrum(L, w1, b1, w2, b2, w3, b3, w4, freq):
    act = lambda a: jnp.sin(freq * a)
    hdn = act(hyena_pos_features(L) @ w1 + b1)
    hdn = act(hdn @ w2 + b2)
    hdn = act(hdn @ w3 + b3)
    deltas = jnp.abs(jnp.linspace(math.log(HY_TARGET) / HY_SLOW_DECAY, math.log(HY_TARGET) / HY_FAST_DECAY,
                                  HY_ORDER * HY_W, dtype=F32)).reshape(HY_ORDER, HY_W)
    t = jnp.linspace(0.0, 1.0, L, dtype=F32)[:, None, None]
    w4d = w4.reshape(w4.shape[0], 2, HY_ORDER * HY_W)
    hf = (hdn @ w4d[:, 0]).astype(F32).reshape(L, HY_ORDER, HY_W) * jnp.exp(-t * deltas)
    hb = (hdn[::-1] @ w4d[:, 1]).astype(F32).reshape(L, HY_ORDER, HY_W) * jnp.exp(-t[::-1] * deltas)
    norm = jnp.sum(jnp.abs(hf), axis=0) + jnp.sum(jnp.abs(hb), axis=0)
    return jnp.concatenate([hf / norm, jnp.zeros((1, HY_ORDER, HY_W), F32), hb[:L - 1] / norm], axis=0)


def fft_long_conv(u, k_f):
    L = u.shape[1]
    u_f = jnp.fft.rfft(u.astype(F32), n=2 * L, axis=1)
    return jnp.fft.irfft(u_f * k_f[None], n=2 * L, axis=1)[:, :L]


FFT_R = 128
FFT_CB = 16


def _dft_constants():
    r = FFT_R
    n = r * r
    idx = np.arange(r)
    f = np.exp(-2j * np.pi * np.outer(idx, idx) / r)
    tw = np.exp(-2j * np.pi * np.outer(idx, idx) / n)
    blk = lambda w: np.block([[w.real, w.imag], [-w.imag, w.real]])
    mats = {
        'w1': blk(f[:r // 2, :]),
        'w1f': np.concatenate([f.real, f.imag], 1),
        'w2': blk(f),
        'w2c': blk(np.conj(f)),
        'w1c': blk(np.conj(f)[:, :r // 2]) / n,
    }
    out = {}
    for name, w in mats.items():
        w32 = jnp.asarray(w, F32)
        out[name] = _split_bf16(w32)
    out['twa'] = jnp.asarray(np.concatenate([tw.real, tw.real], 1), F32)
    out['twb'] = jnp.asarray(np.concatenate([-tw.imag, tw.imag], 1), F32)
    out['twbc'] = jnp.asarray(np.concatenate([tw.imag, -tw.imag], 1), F32)
    return out


def _cmm(x, wh_ref, wl_ref):
    xh, xl = _split_bf16(x)
    return _dot3(xh, xl, wh_ref[...], wl_ref[...], _NN)


def _swap_halves(x):
    h = x.shape[-1] // 2
    return jnp.concatenate([x[..., h:], x[..., :h]], axis=-1)


def _cmul(x, a, b):
    return x * a + _swap_halves(x) * b


def _tile_transpose(x):
    h = x.shape[-1] // 2
    return jnp.concatenate([jnp.swapaxes(x[..., :h], 1, 2), jnp.swapaxes(x[..., h:], 1, 2)], axis=-1)


def _fft_forward(z2d, cb, w1h, w1l, w2h, w2l, twa, twb):
    r = FFT_R
    x = _cmm(z2d, w1h, w1l).reshape(cb, r, 2 * r)
    x = _tile_transpose(_cmul(x, twa[...], twb[...]))
    return _cmm(x.reshape(cb * r, 2 * r), w2h, w2l).reshape(cb, r, 2 * r)


def _hy_filter_kernel(k_ref, w1h, w1l, w2h, w2l, twa, twb, o_ref):
    cb = k_ref.shape[0]
    o_ref[...] = _fft_forward(k_ref[...].reshape(cb * FFT_R, FFT_R), cb, w1h, w1l, w2h, w2l, twa, twb)


def _hy_conv_kernel(z_ref, k_ref, w1h, w1l, w2h, w2l, w2ch, w2cl, w1ch, w1cl, twa, twb, twbc, o_ref):
    cb = z_ref.shape[0]
    r = FFT_R
    x = _fft_forward(z_ref[...].reshape(cb * r, r), cb, w1h, w1l, w2h, w2l, twa, twb)
    k = k_ref[...]
    kr, ki = k[..., :r], k[..., r:]
    x = _cmul(x, jnp.concatenate([kr, kr], axis=-1), jnp.concatenate([-ki, ki], axis=-1))
    x = _cmm(x.reshape(cb * r, 2 * r), w2ch, w2cl).reshape(cb, r, 2 * r)
    x = _tile_transpose(_cmul(x, twa[...], twbc[...]))
    o_ref[...] = _cmm(x.reshape(cb * r, 2 * r), w1ch, w1cl).reshape(cb, r, r)


def _const_spec(a):
    return pl.BlockSpec(a.shape, lambda i: (0,) * a.ndim)


def hyena_filter_fft(k_time, consts):
    r = FFT_R
    c = k_time.shape[1]
    kt = k_time.reshape(r, r, c).transpose(2, 1, 0)
    ws = [*consts['w1f'], *consts['w2'], consts['twa'], consts['twb']]
    return pl.pallas_call(
        _hy_filter_kernel,
        grid=(c // FFT_CB,),
        in_specs=[pl.BlockSpec((FFT_CB, r, r), lambda i: (i, 0, 0))] + [_const_spec(w) for w in ws],
        out_specs=pl.BlockSpec((FFT_CB, r, 2 * r), lambda i: (i, 0, 0)),
        out_shape=jax.ShapeDtypeStruct((c, r, 2 * r), F32),
        compiler_params=pltpu.CompilerParams(vmem_limit_bytes=VMEM_LIMIT),
        name="hyena_filter_fft",
    )(kt, *ws)


def hyena_long_conv(u, k_spec, consts):
    r = FFT_R
    bn, ll, c = u.shape
    assert bn == 2 and 2 * ll == r * r
    z = u.reshape(2, r // 2, r, c).transpose(3, 2, 0, 1).reshape(c, r, r)
    ws = [*consts['w1'], *consts['w2'], *consts['w2c'], *consts['w1c'],
          consts['twa'], consts['twb'], consts['twbc']]
    y = pl.pallas_call(
        _hy_conv_kernel,
        grid=(c // FFT_CB,),
        in_specs=[pl.BlockSpec((FFT_CB, r, r), lambda i: (i, 0, 0)),
                  pl.BlockSpec((FFT_CB, r, 2 * r), lambda i: (i, 0, 0))] + [_const_spec(w) for w in ws],
        out_specs=pl.BlockSpec((FFT_CB, r, r), lambda i: (i, 0, 0)),
        out_shape=jax.ShapeDtypeStruct((c, r, r), F32),
        compiler_params=pltpu.CompilerParams(vmem_limit_bytes=VMEM_LIMIT),
        name="hyena_long_conv",
    )(z, k_spec, *ws)
    return y.reshape(c, r, 2, r // 2).transpose(2, 3, 1, 0).reshape(2, ll, c)


def hyena_mixer(z, conv_w, conv_b, w1, b1, w2, b2, w3, b3, w4, freq, bias):
    L = z.shape[1]
    zc = short_conv3(z, conv_w, conv_b)
    x1, x2, v = jnp.split(zc, 3, axis=-1)
    k_t = hyena_filter_spectrum(L, w1, b1, w2, b2, w3, b3, w4, freq)
    y = v.astype(F32)
    use_mxu_fft = z.shape[0] == 2 and 2 * L == FFT_R * FFT_R
    if use_mxu_fft:
        consts = _dft_constants()
    else:
        k_f = jnp.fft.rfft(k_t, axis=0)
    for o, gate in enumerate((x1, x2)):
        if use_mxu_fft:
            conv = hyena_long_conv(y, hyena_filter_fft(k_t[:, o], consts), consts)
        else:
            conv = fft_long_conv(y, k_f[:, o])
        y = gate * (conv + y * bias[o])
    return y.astype(z.dtype)


def _mm3(x, w):
    b, l, k = x.shape
    return matmul_pallas(x.reshape(b * l, k), w).reshape(b, l, w.shape[1])


def even_mixer(in_ctx, in_lat, cos, sin, need_ctx, w_in, hy, mla, w_o):
    q_norm, w_uq, kv_norm, w_ukv = mla
    bn, lc, _ = in_ctx.shape
    w_hy, w_att = w_in[:, :O_HY], w_in[:, O_HY:]
    nq, nkv = O_Q - O_HY, O_KV - O_HY
    zc, zl = _mm3(in_ctx, w_att), _mm3(in_lat, w_att)
    kv_lat = jnp.concatenate([zc[..., nq:nkv], zl[..., nq:nkv]], axis=1)
    kv = _mm3(rms_norm(kv_lat, kv_norm), w_ukv).reshape(bn, kv_lat.shape[1], MLA_HEADS, MLA_NOPE + MLA_V)
    kn, vv = kv[..., :MLA_NOPE], kv[..., MLA_NOPE:]
    kr = jnp.concatenate([zc[..., nkv:], apply_axial_rope(zl[..., nkv:], cos, sin)], axis=1)
    w_q2 = _mla_query_weights(w_uq)

    def out_proj(hyena, att):
        b, l, k = hyena.shape
        return matmul_pair_pallas(hyena.reshape(b * l, k), att.reshape(b * l, -1), w_o).reshape(b, l, -1)

    att = mla_attention_pallas(_mm3(rms_norm(zl[..., :nq], q_norm), w_q2), cos, sin, kn, kr, vv)
    y_lat = out_proj(hyena_mixer(_mm3(in_lat, w_hy), *hy), att)
    y_ctx = None
    if need_ctx:
        att_c = mla_attention_pallas(_mm3(rms_norm(zc[..., :nq], q_norm), w_q2),
                                     jnp.ones((lc, MLA_ROPE), F32), jnp.zeros((lc, MLA_ROPE), F32),
                                     kn[:, :lc], kr[:, :lc], vv[:, :lc])
        y_ctx = out_proj(hyena_mixer(_mm3(in_ctx, w_hy), *hy), att_c)
    return y_ctx, y_lat


def s5_mixer(in_ctx, in_lat, need_ctx, lam_re, lam_im, log_dt, b_re, b_im, c_re, c_im, d_skip, w_o, w_g):
    s_ctx, s_lat = s5_scan(in_ctx, in_lat, lam_re, lam_im, log_dt, b_re, b_im, c_re, c_im)

    def finish(ys, h):
        y = ys + d_skip * h.astype(F32)
        y = jax.nn.gelu(y).astype(h.dtype)
        b, l, dm = y.shape
        return glu_pallas(y.reshape(b * l, dm), w_o, w_g).reshape(b, l, dm)

    y_lat = finish(s_lat, in_lat)
    y_ctx = finish(s_ctx, in_ctx) if need_ctx else None
    return y_ctx, y_lat


def kernel(x, c, ctx, c_ctx, mod_w, mod_b, ln_mix_g, ln_mix_b, ln_ffn_g, ln_ffn_b,
           ev_w_in, ev_conv_w, ev_conv_b, hy_w1, hy_b1, hy_w2, hy_b2, hy_w3, hy_b3, hy_w4,
           hy_freq, hy_bias, mla_q_norm, mla_w_uq, mla_kv_norm, mla_w_ukv, ev_w_o,
           s5_lam_re, s5_lam_im, s5_log_dt, s5_b_re, s5_b_im, s5_c_re, s5_c_im, s5_d,
           od_w_o, od_w_g, peer_w_q, peer_keys, peer_u, peer_v):
    cos, sin = axial_rope(x.shape[1])
    act_lat = jax.nn.silu(c)
    act_ctx = jax.nn.silu(c_ctx)
    bn, ll, dm = x.shape
    lc = ctx.shape[1]
    mods_lat = [jnp.split((act_lat @ mod_w[l] + mod_b[l])[:, None, :], 6, axis=-1) for l in range(DEPTH)]
    mods_ctx = [jnp.split((act_ctx @ mod_w[l] + mod_b[l])[None, None, :], 6, axis=-1) for l in range(DEPTH)]
    h_lat = x
    h_ctx = ctx.reshape(1, bn * lc, dm)
    in_lat = h_lat * (1.0 + mods_lat[0][1]) + mods_lat[0][0]
    in_ctx = (h_ctx * (1.0 + mods_ctx[0][1]) + mods_ctx[0][0]).reshape(bn, lc, dm)
    zero_lat = jnp.zeros_like(mods_lat[0][0])
    u_bf = peer_u.astype(BF16)
    vt_bf = peer_v.transpose(0, 2, 1).astype(BF16)
    for layer in range(DEPTH):
        need_ctx = layer < DEPTH - 1
        m_lat, m_ctx = mods_lat[layer], mods_ctx[layer]
        i = layer // 2
        if layer % 2 == 0:
            hy = (ev_conv_w[i], ev_conv_b[i], hy_w1[i], hy_b1[i], hy_w2[i], hy_b2[i], hy_w3[i], hy_b3[i],
                  hy_w4[i], hy_freq[i], hy_bias[i])
            mla = (mla_q_norm[i], mla_w_uq[i], mla_kv_norm[i], mla_w_ukv[i])
            y_ctx, y_lat = even_mixer(in_ctx, in_lat, cos, sin, need_ctx, ev_w_in[i], hy, mla, ev_w_o[i])
        else:
            y_ctx, y_lat = s5_mixer(in_ctx, in_lat, need_ctx, s5_lam_re[i], s5_lam_im[i], s5_log_dt[i],
                                    s5_b_re[i], s5_b_im[i], s5_c_re[i], s5_c_im[i], s5_d[i],
                                    od_w_o[i], od_w_g[i])
        peer = lambda xm, xt: peer_ffn_pallas(xm.reshape(-1, dm), xt, peer_w_q[layer], peer_keys[layer],
                                              u_bf, vt_bf, layer).reshape(xm.shape)
        nxt_lat = (mods_lat[layer + 1][0], mods_lat[layer + 1][1]) if need_ctx else (zero_lat, zero_lat)
        h_lat, xin, xin_t = ln_mod_pallas(h_lat, y_lat, m_lat[2], ln_mix_g[layer], ln_mix_b[layer],
                                          m_lat[3], m_lat[4], True)
        h_lat, in_lat = ln_mod_pallas(h_lat, peer(xin, xin_t), m_lat[5], ln_ffn_g[layer], ln_ffn_b[layer],
                                      nxt_lat[0], nxt_lat[1], False)
        if need_ctx:
            h_ctx, xin, xin_t = ln_mod_pallas(h_ctx, y_ctx.reshape(h_ctx.shape), m_ctx[2],
                                              ln_mix_g[layer], ln_mix_b[layer], m_ctx[3], m_ctx[4], True)
            h_ctx, in_ctx = ln_mod_pallas(h_ctx, peer(xin, xin_t), m_ctx[5], ln_ffn_g[layer], ln_ffn_b[layer],
                                          mods_ctx[layer + 1][0], mods_ctx[layer + 1][1], False)
            in_ctx = in_ctx.reshape(bn, lc, dm)
    return h_lat
```
